```python
import jax, jax.numpy as jnp
from jax import lax
import numpy as np

D_MODEL = 1024
BATCH = 8
SEQ = 4096
DEPTH = 4

N_MIXERS = 2
N_CONV_LAYERS = (DEPTH + N_MIXERS - 1) // N_MIXERS
N_ATTN_LAYERS = DEPTH // N_MIXERS
NORM_EPS = 1e-6
CONV_WIDTH = 3
HEAD_DIM = 64
HEADS_PER_GROUP = D_MODEL // HEAD_DIM
DILATED_GROUPS = ((128, 1), (512, 4), (2048, 16))
N_ATTN_GROUPS = len(DILATED_GROUPS)
QKV_WIDTH = N_ATTN_GROUPS * 3 * HEADS_PER_GROUP * HEAD_DIM
ROPE_THETA = 10000.0
N_EXPERT_GROUPS = 4
EXPERTS_PER_GROUP = 4
N_EXPERTS = N_EXPERT_GROUPS * EXPERTS_PER_GROUP
FINE_TOP_K = 2
EXPERT_FF = D_MODEL // 2

kernel_name = "hybrid_shortconv_dilated_attn_hmoe"


def rms_norm(x, g):
    xf = x.astype(jnp.float32)
    y = xf * lax.rsqrt(jnp.mean(xf * xf, axis=-1, keepdims=True) + NORM_EPS)
    return (y * g.astype(jnp.float32)).astype(x.dtype)


def rope_tables(positions):
    inv_freq = ROPE_THETA ** (-jnp.arange(0, HEAD_DIM, 2, dtype=jnp.float32) / HEAD_DIM)
    ang = positions.astype(jnp.float32)[..., None] * inv_freq
    return jnp.cos(ang)[:, :, None, :], jnp.sin(ang)[:, :, None, :]


def apply_rope(t, cos, sin):
    tf = t.astype(jnp.float32)
    t1, t2 = jnp.split(tf, 2, axis=-1)
    return jnp.concatenate([t1 * cos - t2 * sin, t2 * cos + t1 * sin], axis=-1).astype(t.dtype)


def short_conv_mixer(x, w_in, conv_w, w_out):
    gate_b, gate_c, h = jnp.split(x @ w_in, 3, axis=-1)
    u = gate_c * h
    conv = lax.conv_general_dilated(
        u, conv_w.reshape(CONV_WIDTH, 1, D_MODEL).astype(u.dtype),
        window_strides=(1,), padding=[(CONV_WIDTH - 1, 0)],
        dimension_numbers=("NWC", "WIO", "NWC"), feature_group_count=D_MODEL)
    return (gate_b * conv) @ w_out


def dilated_window_attention(q, k, v, dilation, n_back):
    B, S, H, Dh = q.shape
    L = S // dilation
    blk = n_back
    nb = -(-L // blk)
    Lp = nb * blk

    def to_strided(t):
        t = t.reshape(B, L, dilation, H, Dh)
        t = jnp.pad(t, ((0, 0), (0, Lp - L), (0, 0), (0, 0), (0, 0)))
        return t.reshape(B, nb, blk, dilation, H, Dh)

    def with_prev(t):
        prev = jnp.pad(t, ((0, 0), (1, 0), (0, 0), (0, 0), (0, 0), (0, 0)))[:, :-1]
        return jnp.concatenate([prev, t], axis=2)

    qs = to_strided(q)
    kb = with_prev(to_strided(k))
    vb = with_prev(to_strided(v))
    scores = jnp.einsum("bnqrhd,bnkrhd->bnrhqk", qs, kb,
                        preferred_element_type=jnp.float32) * (Dh ** -0.5)
    qi = jnp.arange(blk)[:, None]
    ki = jnp.arange(2 * blk)[None, :]
    dist = qi + blk - ki
    band = (dist >= 0) & (dist <= n_back)
    key_valid = (jnp.arange(nb)[:, None] * blk - blk + ki) >= 0
    mask = band[None] & key_valid[:, None, :]
    scores = jnp.where(mask[None, :, None, None], scores, -jnp.inf)
    m = jnp.max(scores, axis=-1, keepdims=True)
    p = jnp.exp(scores - m)
    denom = jnp.sum(p, axis=-1)
    o = jnp.einsum("bnrhqk,bnkrhd->bnqrhd", p, vb.astype(jnp.float32))
    o = o / jnp.transpose(denom, (0, 1, 4, 2, 3))[..., None]
    lse = jnp.transpose(m[..., 0] + jnp.log(denom), (0, 1, 4, 2, 3))
    o = o.reshape(B, Lp, dilation, H, Dh)[:, :L].reshape(B, S, H, Dh)
    lse = lse.reshape(B, Lp, dilation, H)[:, :L].reshape(B, S, H)
    return o, lse


def dilated_attention_mixer(x, w_qkv, w_o, cos, sin):
    B, S, _ = x.shape
    qkv = (x @ w_qkv).reshape(B, S, N_ATTN_GROUPS, 3, HEADS_PER_GROUP, HEAD_DIM)
    outs, lses = [], []
    for g, (window, dilation) in enumerate(DILATED_GROUPS):
        q = apply_rope(qkv[:, :, g, 0], cos, sin)
        k = apply_rope(qkv[:, :, g, 1], cos, sin)
        o, lse = dilated_window_attention(q, k, qkv[:, :, g, 2], dilation, window // dilation)
        outs.append(o)
        lses.append(lse)
    o_all = jnp.stack(outs, axis=3)
    w_grp = jax.nn.softmax(jnp.stack(lses, axis=-1), axis=-1)
    merged = jnp.einsum("bshg,bshgd->bshd", w_grp, o_all).reshape(B, S, D_MODEL)
    return merged.astype(x.dtype) @ w_o


def hierarchical_moe(x, w_coarse, b_coarse, w_fine, b_fine, w_gate, w_up, w_down):
    B, S, D = x.shape
    t = x.reshape(-1, D)
    coarse_logits = (t @ w_coarse).astype(jnp.float32) + b_coarse.astype(jnp.float32)
    coarse_prob = jax.nn.softmax(coarse_logits, axis=-1)
    p_top, g_top = lax.top_k(coarse_prob, 1)
    fine_all = jnp.einsum("nd,gde->nge", t, w_fine).astype(jnp.float32) + b_fine.astype(jnp.float32)
    fine_logits = jnp.take_along_axis(fine_all, g_top[:, :, None], axis=1)[:, 0]
    f_val, f_idx = lax.top_k(fine_logits, FINE_TOP_K)
    gate = p_top * jax.nn.softmax(f_val, axis=-1)
    expert_idx = g_top * EXPERTS_PER_GROUP + f_idx
    combine = jnp.sum(jax.nn.one_hot(expert_idx, N_EXPERTS, dtype=jnp.float32) * gate[..., None], axis=1)
    out = jnp.zeros(t.shape, jnp.float32)
    for e in range(N_EXPERTS):
        h = jax.nn.silu(t @ w_gate[e]) * (t @ w_up[e])
        out = out + combine[:, e:e + 1] * (h @ w_down[e])
    return out.reshape(B, S, D).astype(x.dtype)


def setup_inputs(seed: int = 0) -> dict:
    key = jax.random.key(seed)
    ks = jax.random.split(key, 20)
    f32 = jnp.float32
    nrm = lambda k, shape, s: jax.random.normal(k, shape, f32) * s
    d_s = D_MODEL ** -0.5
    return {
        "x": nrm(ks[0], (BATCH, SEQ, D_MODEL), 1.0),
        "positions": jnp.broadcast_to(jnp.arange(SEQ, dtype=jnp.int32), (BATCH, SEQ)),
        "norm_mix": 1.0 + nrm(ks[1], (DEPTH, D_MODEL), 0.05),
        "norm_ffn": 1.0 + nrm(ks[2], (DEPTH, D_MODEL), 0.05),
        "norm_final": 1.0 + nrm(ks[3], (D_MODEL,), 0.05),
        "conv_w_in": nrm(ks[4], (N_CONV_LAYERS, D_MODEL, 3 * D_MODEL), d_s),
        "conv_w": nrm(ks[5], (N_CONV_LAYERS, CONV_WIDTH, D_MODEL), CONV_WIDTH ** -0.5),
        "conv_w_out": nrm(ks[6], (N_CONV_LAYERS, D_MODEL, D_MODEL), d_s),
        "attn_w_qkv": nrm(ks[7], (N_ATTN_LAYERS, D_MODEL, QKV_WIDTH), d_s),
        "attn_w_o": nrm(ks[8], (N_ATTN_LAYERS, D_MODEL, D_MODEL), d_s),
        "w_coarse": nrm(ks[9], (DEPTH, D_MODEL, N_EXPERT_GROUPS), d_s),
        "b_coarse": nrm(ks[10], (DEPTH, N_EXPERT_GROUPS), 0.01),
        "w_fine": nrm(ks[11], (DEPTH, N_EXPERT_GROUPS, D_MODEL, EXPERTS_PER_GROUP), d_s),
        "b_fine": nrm(ks[12], (DEPTH, N_EXPERT_GROUPS, EXPERTS_PER_GROUP), 0.01),
        "w_gate": nrm(ks[13], (DEPTH, N_EXPERTS, D_MODEL, EXPERT_FF), d_s),
        "w_up": nrm(ks[14], (DEPTH, N_EXPERTS, D_MODEL, EXPERT_FF), d_s),
        "w_down": nrm(ks[15], (DEPTH, N_EXPERTS, EXPERT_FF, D_MODEL), EXPERT_FF ** -0.5),
    }


def reference(x, positions, norm_mix, norm_ffn, norm_final, conv_w_in, conv_w, conv_w_out,
              attn_w_qkv, attn_w_o, w_coarse, b_coarse, w_fine, b_fine, w_gate, w_up, w_down):
    cos, sin = rope_tables(positions)
    h = x
    for i in range(DEPTH):
        hn = rms_norm(h, norm_mix[i])
        j = i // N_MIXERS
        if i % N_MIXERS == 0:
            h = h + short_conv_mixer(hn, conv_w_in[j], conv_w[j], conv_w_out[j])
        else:
            h = h + dilated_attention_mixer(hn, attn_w_qkv[j], attn_w_o[j], cos, sin)
        hn = rms_norm(h, norm_ffn[i])
        h = h + hierarchical_moe(hn, w_coarse[i], b_coarse[i], w_fine[i], b_fine[i],
                                 w_gate[i], w_up[i], w_down[i])
    return rms_norm(h, norm_final)
```

```python
import functools

import numpy as np
import jax
import jax.numpy as jnp
from jax import lax
from jax.experimental import pallas as pl
from jax.experimental.pallas import tpu as pltpu

D_MODEL = 1024
NORM_EPS = 1e-6
CONV_WIDTH = 3
HEAD_DIM = 64
HALF_DIM = HEAD_DIM // 2
N_HEADS = D_MODEL // HEAD_DIM
DILATED_GROUPS = ((128, 1), (512, 4), (2048, 16))
N_ATTN_GROUPS = len(DILATED_GROUPS)
QKV_WIDTH = N_ATTN_GROUPS * 3 * D_MODEL
ROPE_THETA = 10000.0
N_EXPERT_GROUPS = 4
EXPERTS_PER_GROUP = 4
N_EXPERTS = N_EXPERT_GROUPS * EXPERTS_PER_GROUP
EXPERT_FF = D_MODEL // 2

LANES = 128
ATT_BLK = 128
VMEM_LIMIT_BYTES = 56 * 1024 * 1024
ROW_TILE = 1024
CONV_TILE = 512
EXPERT_TILE = 512
NEG_INF = float("-inf")


def _cparams(*sem):
    return pltpu.CompilerParams(dimension_semantics=sem, vmem_limit_bytes=VMEM_LIMIT_BYTES)


def _rms(x, g):
    return x * lax.rsqrt(jnp.mean(x * x, axis=-1, keepdims=True) + NORM_EPS) * g


def _rope_kernel(pos_ref, freq_ref, cos_ref, sin_ref):
    ang = pos_ref[...].astype(jnp.float32) * freq_ref[...]
    cos_ref[...] = jnp.cos(ang)
    sin_ref[...] = jnp.sin(ang)


def rope_tables(positions):
    n = positions.shape[0]
    inv_freq = (ROPE_THETA ** (-np.arange(0, HEAD_DIM, 2, dtype=np.float32) / HEAD_DIM)).astype(np.float32)
    freq = jnp.asarray(np.tile(inv_freq, LANES // HALF_DIM)[None, :])
    tr = ROW_TILE
    return pl.pallas_call(
        _rope_kernel,
        grid=(n // tr,),
        in_specs=[pl.BlockSpec((tr, 1), lambda i: (i, 0)),
                  pl.BlockSpec((1, LANES), lambda i: (0, 0))],
        out_specs=[pl.BlockSpec((tr, LANES), lambda i: (i, 0)),
                   pl.BlockSpec((tr, LANES), lambda i: (i, 0))],
        out_shape=[jax.ShapeDtypeStruct((n, LANES), jnp.float32)] * 2,
        compiler_params=_cparams("parallel"),
        name="rope_tables",
    )(positions.reshape(n, 1), freq)


def _combine_kernel(h_ref, y0_ref, y1_ref, gate_ref, o_ref):
    g = gate_ref[...]
    o_ref[...] = (h_ref[...] + g[:, 0:1] * y0_ref[...].astype(jnp.float32)
                  + g[:, 1:2] * y1_ref[...].astype(jnp.float32))


def _combine_norm_kernel(h_ref, y0_ref, y1_ref, gate_ref, g_ref, o_ref):
    g = gate_ref[...]
    h = (h_ref[...] + g[:, 0:1] * y0_ref[...].astype(jnp.float32)
         + g[:, 1:2] * y1_ref[...].astype(jnp.float32))
    o_ref[...] = _rms(h, g_ref[...])


def moe_combine(h, y0, y1, gates, final_gain=None):
    n = h.shape[0]
    tr = ROW_TILE
    row = pl.BlockSpec((tr, D_MODEL), lambda i: (i, 0))
    in_specs = [row, row, row, pl.BlockSpec((tr, LANES), lambda i: (i, 0))]
    args = [h, y0, y1, gates]
    body = _combine_kernel
    if final_gain is not None:
        in_specs.append(pl.BlockSpec((1, D_MODEL), lambda i: (0, 0)))
        args.append(final_gain.reshape(1, D_MODEL))
        body = _combine_norm_kernel
    return pl.pallas_call(
        body, grid=(n // tr,), in_specs=in_specs, out_specs=row,
        out_shape=jax.ShapeDtypeStruct((n, D_MODEL), jnp.float32),
        compiler_params=_cparams("parallel"), name="moe_combine",
    )(*args)


def _conv_mixer_kernel(h_ref, g_ref, win_ref, cw_ref, wout_ref, o_ref, hn_buf, u_buf, v_buf):
    t = h_ref.shape[0]

    @pl.when(pl.program_id(1) == 0)
    def _():
        u_buf[0:8, :] = jnp.zeros((8, D_MODEL), jnp.float32)

    x = h_ref[...]
    hn_buf[...] = _rms(x, g_ref[...]).astype(jnp.bfloat16)
    cw = cw_ref[...]
    chunk = 512
    for c in range(0, D_MODEL, chunk):
        hn = hn_buf[...]
        gate_c = jnp.dot(hn, win_ref[:, D_MODEL + c:D_MODEL + c + chunk], preferred_element_type=jnp.float32)
        hh = jnp.dot(hn, win_ref[:, 2 * D_MODEL + c:2 * D_MODEL + c + chunk], preferred_element_type=jnp.float32)
        u_buf[8:8 + t, c:c + chunk] = gate_c * hh
        conv = (cw[0:1, c:c + chunk] * u_buf[6:6 + t, c:c + chunk]
                + cw[1:2, c:c + chunk] * u_buf[7:7 + t, c:c + chunk]
                + cw[2:3, c:c + chunk] * u_buf[8:8 + t, c:c + chunk])
        gate_b = jnp.dot(hn, win_ref[:, c:c + chunk], preferred_element_type=jnp.float32)
        v_buf[:, c:c + chunk] = (gate_b * conv).astype(jnp.bfloat16)
    u_buf[0:8, :] = u_buf[t:t + 8, :]
    o_ref[...] = x + jnp.dot(v_buf[...], wout_ref[...], preferred_element_type=jnp.float32)


def conv_mixer(h, batch, gain, w_in, conv_w, w_out):
    n = h.shape[0]
    seq = n // batch
    t = CONV_TILE
    ns = seq // t
    row = pl.BlockSpec((t, D_MODEL), lambda b, s: (b * ns + s, 0))
    return pl.pallas_call(
        _conv_mixer_kernel,
        grid=(batch, ns),
        in_specs=[row,
                  pl.BlockSpec((1, D_MODEL), lambda b, s: (0, 0)),
                  pl.BlockSpec((D_MODEL, 3 * D_MODEL), lambda b, s: (0, 0)),
                  pl.BlockSpec((CONV_WIDTH, D_MODEL), lambda b, s: (0, 0)),
                  pl.BlockSpec((D_MODEL, D_MODEL), lambda b, s: (0, 0))],
        out_specs=row,
        out_shape=jax.ShapeDtypeStruct((n, D_MODEL), jnp.float32),
        scratch_shapes=[pltpu.VMEM((t, D_MODEL), jnp.bfloat16),
                        pltpu.VMEM((t + 8, D_MODEL), jnp.float32),
                        pltpu.VMEM((t, D_MODEL), jnp.bfloat16)],
        compiler_params=_cparams("arbitrary", "arbitrary"),
        name="conv_mixer",
    )(h, gain.reshape(1, D_MODEL), w_in, conv_w, w_out)


def _qkv_kernel(h_ref, g_ref, w_ref, cos_ref, sin_ref, o_ref, hn_buf):
    j = pl.program_id(1)

    @pl.when(j == 0)
    def _():
        hn_buf[...] = _rms(h_ref[...], g_ref[...]).astype(jnp.bfloat16)

    z = jnp.dot(hn_buf[...], w_ref[...], preferred_element_type=jnp.float32)
    part = j % 3

    @pl.when(part == 2)
    def _():
        o_ref[...] = z.astype(o_ref.dtype)

    @pl.when(part < 2)
    def _():
        half = D_MODEL // 2
        reps = half // LANES
        scale = jnp.where(part == 0, HEAD_DIM ** -0.5, 1.0).astype(jnp.float32)
        c = jnp.concatenate([cos_ref[...]] * reps, axis=1) * scale
        s = jnp.concatenate([sin_ref[...]] * reps, axis=1) * scale
        z1 = z[:, :half]
        z2 = z[:, half:]
        o_ref[:, :half] = (z1 * c - z2 * s).astype(o_ref.dtype)
        o_ref[:, half:] = (z2 * c + z1 * s).astype(o_ref.dtype)


def qkv_project(h, gain, w_qkv, cos, sin):
    n = h.shape[0]
    tr = ROW_TILE
    ncol = QKV_WIDTH // D_MODEL
    return pl.pallas_call(
        _qkv_kernel,
        grid=(n // tr, ncol),
        in_specs=[pl.BlockSpec((tr, D_MODEL), lambda i, j: (i, 0)),
                  pl.BlockSpec((1, D_MODEL), lambda i, j: (0, 0)),
                  pl.BlockSpec((D_MODEL, D_MODEL), lambda i, j: (0, j)),
                  pl.BlockSpec((tr, LANES), lambda i, j: (i, 0)),
                  pl.BlockSpec((tr, LANES), lambda i, j: (i, 0))],
        out_specs=pl.BlockSpec((tr, D_MODEL), lambda i, j: (i, j)),
        out_shape=jax.ShapeDtypeStruct((n, QKV_WIDTH), jnp.bfloat16),
        scratch_shapes=[pltpu.VMEM((tr, D_MODEL), jnp.bfloat16)],
        compiler_params=_cparams("parallel", "arbitrary"),
        name="qkv_project",
    )(h, gain.reshape(1, D_MODEL), w_qkv, cos, sin)


def _attention_kernel(*refs, has_prev, emit_lse):
    if has_prev:
        q_ref, k_ref, v_ref, op_ref, lp_ref = refs[:5]
        refs = refs[5:]
    else:
        q_ref, k_ref, v_ref = refs[:3]
        refs = refs[3:]
    if emit_lse:
        o_ref, l_ref, k_win, v_win = refs
    else:
        o_ref, k_win, v_win = refs
    blk = ATT_BLK
    n = pl.program_id(2)

    @pl.when(n == 0)
    def _():
        k_win[0:blk, :] = jnp.zeros((blk, D_MODEL), k_win.dtype)
        v_win[0:blk, :] = jnp.zeros((blk, D_MODEL), v_win.dtype)

    k_win[blk:2 * blk, :] = k_ref[0]
    v_win[blk:2 * blk, :] = v_ref[0]

    qi = lax.broadcasted_iota(jnp.int32, (blk, 2 * blk), 0)
    ki = lax.broadcasted_iota(jnp.int32, (blk, 2 * blk), 1)
    dist = qi + blk - ki
    valid = (dist >= 0) & (dist <= blk) & ((ki >= blk) | (n > 0))
    bias = jnp.where(valid, 0.0, NEG_INF).astype(jnp.float32)

    half = D_MODEL // 2
    lane2 = lax.broadcasted_iota(jnp.int32, (blk, 2 * LANES), 1)
    lane1 = lax.broadcasted_iota(jnp.int32, (blk, LANES), 1)
    heads_per_slab = LANES // HALF_DIM
    if has_prev:
        lse_prev = lp_ref[0]
    lse_out = jnp.zeros((blk, LANES), jnp.float32)

    for m in range(N_HEADS // heads_per_slab):
        q_slab = jnp.concatenate([q_ref[0, :, m * LANES:(m + 1) * LANES],
                                  q_ref[0, :, half + m * LANES:half + (m + 1) * LANES]], axis=1)
        k_slab = jnp.concatenate([k_win[:, m * LANES:(m + 1) * LANES],
                                  k_win[:, half + m * LANES:half + (m + 1) * LANES]], axis=1)
        for pair in range(heads_per_slab // 2):
            col = (m * heads_per_slab + pair * 2) * HEAD_DIM
            v_pair = v_win[:, col:col + LANES]
            outs = []
            for sub in range(2):
                a = pair * 2 + sub
                hd = m * heads_per_slab + a
                head_lanes = ((lane2 % LANES) // HALF_DIM) == a
                qa = jnp.where(head_lanes, q_slab, jnp.zeros_like(q_slab))
                s = lax.dot_general(qa, k_slab, (((1,), (1,)), ((), ())),
                                    preferred_element_type=jnp.float32) + bias
                mx = jnp.max(s, axis=-1, keepdims=True)
                p = jnp.exp(s - mx)
                den = jnp.sum(p, axis=-1, keepdims=True)
                o = jnp.dot(p.astype(jnp.bfloat16), v_pair, preferred_element_type=jnp.float32)
                lse = mx + jnp.log(den)
                if has_prev:
                    lp = lse_prev[:, hd:hd + 1]
                    top = jnp.maximum(lp, lse)
                    wp = jnp.exp(lp - top)
                    wn = jnp.exp(lse - top)
                    tot = wp + wn
                    outs.append((o * (wn / (den * tot)), wp / tot))
                    lse = top + jnp.log(tot)
                else:
                    outs.append((o * (1.0 / den), None))
                lse_out = jnp.where(lane1 == hd, lse, lse_out)
            first = lane1 < HEAD_DIM
            o_pair = jnp.where(first, outs[0][0], outs[1][0])
            if has_prev:
                o_pair = o_pair + op_ref[0, :, col:col + LANES].astype(jnp.float32) * jnp.where(
                    first, outs[0][1], outs[1][1])
            o_ref[0, :, col:col + LANES] = o_pair.astype(o_ref.dtype)
    if emit_lse:
        l_ref[0] = lse_out
    k_win[0:blk, :] = k_win[blk:2 * blk, :]
    v_win[0:blk, :] = v_win[blk:2 * blk, :]


def dilated_attention_group(qkv, batch, group, o_prev, lse_prev, out_dtype, emit_lse):
    n = qkv.shape[0]
    seq = n // batch
    window, dil = DILATED_GROUPS[group]
    assert window // dil == ATT_BLK
    length = seq // dil
    nblk = length // ATT_BLK
    blk = ATT_BLK
    ncol = QKV_WIDTH // D_MODEL
    qkv_v = qkv.reshape(batch, length, dil * QKV_WIDTH)
    has_prev = o_prev is not None

    def qkv_spec(part):
        return pl.BlockSpec((1, blk, D_MODEL), lambda b, r, i: (b, i, r * ncol + group * 3 + part))

    o_spec = pl.BlockSpec((1, blk, D_MODEL), lambda b, r, i: (b, i, r))
    l_spec = pl.BlockSpec((1, blk, LANES), lambda b, r, i: (b, i, r))
    in_specs = [qkv_spec(0), qkv_spec(1), qkv_spec(2)]
    args = [qkv_v, qkv_v, qkv_v]
    if has_prev:
        in_specs += [o_spec, l_spec]
        args += [o_prev.reshape(batch, length, dil * D_MODEL), lse_prev.reshape(batch, length, dil * LANES)]
    out_specs = [o_spec]
    out_shape = [jax.ShapeDtypeStruct((batch, length, dil * D_MODEL), out_dtype)]
    if emit_lse:
        out_specs.append(l_spec)
        out_shape.append(jax.ShapeDtypeStruct((batch, length, dil * LANES), jnp.float32))
    res = pl.pallas_call(
        functools.partial(_attention_kernel, has_prev=has_prev, emit_lse=emit_lse),
        grid=(batch, dil, nblk),
        in_specs=in_specs, out_specs=out_specs, out_shape=out_shape,
        scratch_shapes=[pltpu.VMEM((2 * blk, D_MODEL), jnp.bfloat16),
                        pltpu.VMEM((2 * blk, D_MODEL), jnp.bfloat16)],
        compiler_params=_cparams("arbitrary", "arbitrary", "arbitrary"),
        name=f"dilated_attention_g{group}",
    )(*args)
    o = res[0].reshape(n, D_MODEL)
    lse = res[1].reshape(n, LANES) if emit_lse else None
    return o, lse


def _proj_residual_kernel(h_ref, a_ref, w_ref, o_ref):
    o_ref[...] = h_ref[...] + jnp.dot(a_ref[...], w_ref[...], preferred_element_type=jnp.float32)


def proj_residual(h, a, w):
    n = h.shape[0]
    tr = ROW_TILE
    row = pl.BlockSpec((tr, D_MODEL), lambda i: (i, 0))
    return pl.pallas_call(
        _proj_residual_kernel, grid=(n // tr,),
        in_specs=[row, row, pl.BlockSpec((D_MODEL, D_MODEL), lambda i: (0, 0))],
        out_specs=row, out_shape=jax.ShapeDtypeStruct((n, D_MODEL), jnp.float32),
        compiler_params=_cparams("parallel"), name="attn_out_proj",
    )(h, a, w)


def _router_kernel(h_ref, g_ref, w_ref, b_ref, hn_ref, route_ref):
    hn = _rms(h_ref[...], g_ref[...])
    hn_ref[...] = hn.astype(hn_ref.dtype)
    logits = jnp.dot(hn, w_ref[...], preferred_element_type=jnp.float32,
                     precision=lax.Precision.HIGHEST) + b_ref[...]
    lane = lax.broadcasted_iota(jnp.int32, logits.shape, 1)
    big = jnp.int32(LANES)

    def first_argmax(vals, vmax):
        return jnp.min(jnp.where(vals == vmax, lane, big), axis=-1, keepdims=True)

    coarse = jnp.where(lane < N_EXPERT_GROUPS, logits, NEG_INF)
    cmax = jnp.max(coarse, axis=-1, keepdims=True)
    p_top = 1.0 / jnp.sum(jnp.exp(coarse - cmax), axis=-1, keepdims=True)
    g_top = first_argmax(coarse, cmax)
    lo = N_EXPERT_GROUPS + g_top * EXPERTS_PER_GROUP
    fine = jnp.where((lane >= lo) & (lane < lo + EXPERTS_PER_GROUP), logits, NEG_INF)
    v1 = jnp.max(fine, axis=-1, keepdims=True)
    i1 = first_argmax(fine, v1)
    fine2 = jnp.where(lane == i1, NEG_INF, fine)
    v2 = jnp.max(fine2, axis=-1, keepdims=True)
    i2 = first_argmax(fine2, v2)
    e2 = jnp.exp(v2 - v1)
    w1 = 1.0 / (1.0 + e2)
    w2 = e2 / (1.0 + e2)
    route = jnp.where(lane == 0, (i1 - N_EXPERT_GROUPS).astype(jnp.float32), 0.0)
    route = jnp.where(lane == 1, (i2 - N_EXPERT_GROUPS).astype(jnp.float32), route)
    route = jnp.where(lane == 2, p_top * w1, route)
    route = jnp.where(lane == 3, p_top * w2, route)
    route_ref[...] = route


def moe_router(h, gain, w_route, b_route):
    n = h.shape[0]
    tr = ROW_TILE
    row = pl.BlockSpec((tr, D_MODEL), lambda i: (i, 0))
    return pl.pallas_call(
        _router_kernel, grid=(n // tr,),
        in_specs=[row,
                  pl.BlockSpec((1, D_MODEL), lambda i: (0, 0)),
                  pl.BlockSpec((D_MODEL, LANES), lambda i: (0, 0)),
                  pl.BlockSpec((1, LANES), lambda i: (0, 0))],
        out_specs=[row, pl.BlockSpec((tr, LANES), lambda i: (i, 0))],
        out_shape=[jax.ShapeDtypeStruct((n, D_MODEL), jnp.bfloat16),
                   jax.ShapeDtypeStruct((n, LANES), jnp.float32)],
        compiler_params=_cparams("parallel"), name="moe_router",
    )(h, gain.reshape(1, D_MODEL), w_route, b_route)


def _expert_kernel(te_ref, tr_ref, tv_ref, x_ref, wg_ref, wu_ref, wd_ref, o_ref):
    i = pl.program_id(0)

    @pl.when(tv_ref[i] > 0)
    def _():
        x = x_ref[...]
        g = jnp.dot(x, wg_ref[0], preferred_element_type=jnp.float32)
        u = jnp.dot(x, wu_ref[0], preferred_element_type=jnp.float32)
        hmid = (g * jax.nn.sigmoid(g) * u).astype(jnp.bfloat16)
        o_ref[...] = jnp.dot(hmid, wd_ref[0], preferred_element_type=jnp.float32).astype(o_ref.dtype)


def expert_mlp(x_sorted, tile_expert, tile_row, tile_valid, w_gate, w_up, w_down):
    p = x_sorted.shape[0]
    tm = EXPERT_TILE
    grid_spec = pltpu.PrefetchScalarGridSpec(
        num_scalar_prefetch=3,
        grid=(p // tm,),
        in_specs=[pl.BlockSpec((tm, D_MODEL), lambda i, te, tr, tv: (tr[i], 0)),
                  pl.BlockSpec((1, D_MODEL, EXPERT_FF), lambda i, te, tr, tv: (te[i], 0, 0)),
                  pl.BlockSpec((1, D_MODEL, EXPERT_FF), lambda i, te, tr, tv: (te[i], 0, 0)),
                  pl.BlockSpec((1, EXPERT_FF, D_MODEL), lambda i, te, tr, tv: (te[i], 0, 0))],
        out_specs=pl.BlockSpec((tm, D_MODEL), lambda i, te, tr, tv: (tr[i], 0)),
    )
    return pl.pallas_call(
        _expert_kernel, grid_spec=grid_spec,
        out_shape=jax.ShapeDtypeStruct((p, D_MODEL), jnp.bfloat16),
        compiler_params=_cparams("arbitrary"), name="expert_mlp",
    )(tile_expert, tile_row, tile_valid, x_sorted, w_gate, w_up, w_down)


def hierarchical_moe(h, gain, w_route, b_route, w_gate, w_up, w_down, final_gain=None):
    n = h.shape[0]
    tm = EXPERT_TILE
    hn, route = moe_router(h, gain, w_route, b_route)
    expert = route[:, 0:2].astype(jnp.int32).reshape(-1)
    onehot = (expert[:, None] == jnp.arange(N_EXPERTS, dtype=jnp.int32)[None, :]).astype(jnp.int32)
    csum = jnp.cumsum(onehot, axis=0)
    rank = jnp.sum(onehot * csum, axis=1) - 1
    counts = csum[-1]
    padded = ((counts + tm - 1) // tm) * tm
    seg_end = jnp.cumsum(padded)
    seg_start = seg_end - padded
    dest = seg_start[expert] + rank
    n_rows = 2 * n + N_EXPERTS * tm
    n_tiles = n_rows // tm
    used = seg_end[-1] // tm
    tile_id = jnp.arange(n_tiles, dtype=jnp.int32)
    tile_row = jnp.minimum(tile_id, used - 1).astype(jnp.int32)
    tile_expert = jnp.minimum(jnp.searchsorted(seg_end, tile_row * tm, side="right"),
                              N_EXPERTS - 1).astype(jnp.int32)
    tile_valid = (tile_id < used).astype(jnp.int32)
    token = jnp.arange(2 * n, dtype=jnp.int32) // 2
    src = jnp.zeros((n_rows,), jnp.int32).at[dest].set(token)
    x_sorted = jnp.take(hn, src, axis=0)
    y = expert_mlp(x_sorted, tile_expert, tile_row, tile_valid, w_gate, w_up, w_down)
    dest2 = dest.reshape(n, 2)
    y0 = jnp.take(y, dest2[:, 0], axis=0)
    y1 = jnp.take(y, dest2[:, 1], axis=0)
    gates = route[:, 2:4]
    gates = jnp.pad(gates, ((0, 0), (0, LANES - 2)))
    return moe_combine(h, y0, y1, gates, final_gain)


def _qk_column_permutation():
    perm = np.arange(QKV_WIDTH).reshape(N_ATTN_GROUPS, 3, N_HEADS, 2, HALF_DIM)
    perm[:, 0:2] = perm[:, 0:2].transpose(0, 1, 3, 2, 4).reshape(N_ATTN_GROUPS, 2, N_HEADS, 2, HALF_DIM)
    return perm.reshape(-1)


def kernel(x, positions, norm_mix, norm_ffn, norm_final, conv_w_in, conv_w, conv_w_out,
           attn_w_qkv, attn_w_o, w_coarse, b_coarse, w_fine, b_fine, w_gate, w_up, w_down):
    batch, seq, d = x.shape
    assert d == D_MODEL
    n = batch * seq
    depth = norm_mix.shape[0]
    bf = jnp.bfloat16
    h = x.reshape(n, d)
    cos, sin = rope_tables(positions.reshape(n))
    perm = jnp.asarray(_qk_column_permutation())
    for i in range(depth):
        j = i // 2
        if i % 2 == 0:
            h = conv_mixer(h, batch, norm_mix[i], conv_w_in[j].astype(bf), conv_w[j], conv_w_out[j].astype(bf))
        else:
            w_qkv = jnp.take(attn_w_qkv[j], perm, axis=1).astype(bf)
            qkv = qkv_project(h, norm_mix[i], w_qkv, cos, sin)
            o, lse = None, None
            for g in range(N_ATTN_GROUPS):
                last = g == N_ATTN_GROUPS - 1
                o, lse = dilated_attention_group(qkv, batch, g, o, lse,
                                                 bf if last else jnp.float32, not last)
            h = proj_residual(h, o, attn_w_o[j].astype(bf))
        w_route = jnp.concatenate(
            [w_coarse[i], jnp.transpose(w_fine[i], (1, 0, 2)).reshape(d, N_EXPERTS),
             jnp.zeros((d, LANES - N_EXPERT_GROUPS - N_EXPERTS), jnp.float32)], axis=1)
        b_route = jnp.concatenate(
            [b_coarse[i], b_fine[i].reshape(-1),
             jnp.zeros((LANES - N_EXPERT_GROUPS - N_EXPERTS,), jnp.float32)]).reshape(1, LANES)
        h = hierarchical_moe(h, norm_ffn[i], w_route, b_route,
                             w_gate[i].astype(bf), w_up[i].astype(bf), w_down[i].astype(bf),
                             norm_final if i == depth - 1 else None)
    return h.reshape(batch, seq, d)
```

```python
import functools
import math

import numpy as np
import jax
import jax.numpy as jnp
from jax import lax
from jax.experimental import pallas as pl
from jax.experimental.pallas import tpu as pltpu

D_MODEL = 1024
NORM_EPS = 1e-6
CONV_WIDTH = 3
HEAD_DIM = 64
HALF_DIM = HEAD_DIM // 2
N_HEADS = D_MODEL // HEAD_DIM
DILATED_GROUPS = ((128, 1), (512, 4), (2048, 16))
N_ATTN_GROUPS = len(DILATED_GROUPS)
QKV_WIDTH = N_ATTN_GROUPS * 3 * D_MODEL
ROPE_THETA = 10000.0
N_EXPERT_GROUPS = 4
EXPERTS_PER_GROUP = 4
N_EXPERTS = N_EXPERT_GROUPS * EXPERTS_PER_GROUP
EXPERT_FF = D_MODEL // 2

LANES = 128
ATT_BLK = 128
VMEM_LIMIT_BYTES = 56 * 1024 * 1024
ROW_TILE = 1024
PROJ_TILE = 512
CONV_TILE = 512
EXPERT_TILE = 512
NEG_INF = float("-inf")
LOG2_E = math.log2(math.e)
LN_2 = math.log(2.0)


def _cparams(*sem):
    return pltpu.CompilerParams(dimension_semantics=sem, vmem_limit_bytes=VMEM_LIMIT_BYTES)


def _rms(x, g):
    return x * lax.rsqrt(jnp.mean(x * x, axis=-1, keepdims=True) + NORM_EPS) * g


def _rope_kernel(pos_ref, freq_ref, cos_ref, sin_ref):
    ang = pos_ref[...].astype(jnp.float32) * freq_ref[...]
    cos_ref[...] = jnp.cos(ang)
    sin_ref[...] = jnp.sin(ang)


def rope_tables(positions):
    n = positions.shape[0]
    inv_freq = (ROPE_THETA ** (-np.arange(0, HEAD_DIM, 2, dtype=np.float32) / HEAD_DIM)).astype(np.float32)
    freq = jnp.asarray(np.tile(inv_freq, LANES // HALF_DIM)[None, :])
    tr = ROW_TILE
    return pl.pallas_call(
        _rope_kernel,
        grid=(n // tr,),
        in_specs=[pl.BlockSpec((tr, 1), lambda i: (i, 0)),
                  pl.BlockSpec((1, LANES), lambda i: (0, 0))],
        out_specs=[pl.BlockSpec((tr, LANES), lambda i: (i, 0)),
                   pl.BlockSpec((tr, LANES), lambda i: (i, 0))],
        out_shape=[jax.ShapeDtypeStruct((n, LANES), jnp.float32)] * 2,
        compiler_params=_cparams("parallel"),
        name="rope_tables",
    )(positions.reshape(n, 1), freq)


def _combine_kernel(h_ref, y0_ref, y1_ref, gate_ref, o_ref):
    g = gate_ref[...]
    o_ref[...] = (h_ref[...] + g[:, 0:1] * y0_ref[...].astype(jnp.float32)
                  + g[:, 1:2] * y1_ref[...].astype(jnp.float32))


def _combine_norm_kernel(h_ref, y0_ref, y1_ref, gate_ref, g_ref, o_ref):
    g = gate_ref[...]
    h = (h_ref[...] + g[:, 0:1] * y0_ref[...].astype(jnp.float32)
         + g[:, 1:2] * y1_ref[...].astype(jnp.float32))
    o_ref[...] = _rms(h, g_ref[...])


def moe_combine(h, y0, y1, gates, final_gain=None):
    n = h.shape[0]
    tr = ROW_TILE
    row = pl.BlockSpec((tr, D_MODEL), lambda i: (i, 0))
    in_specs = [row, row, row, pl.BlockSpec((tr, LANES), lambda i: (i, 0))]
    args = [h, y0, y1, gates]
    body = _combine_kernel
    if final_gain is not None:
        in_specs.append(pl.BlockSpec((1, D_MODEL), lambda i: (0, 0)))
        args.append(final_gain.reshape(1, D_MODEL))
        body = _combine_norm_kernel
    return pl.pallas_call(
        body, grid=(n // tr,), in_specs=in_specs, out_specs=row,
        out_shape=jax.ShapeDtypeStruct((n, D_MODEL), jnp.float32),
        compiler_params=_cparams("parallel"), name="moe_combine",
    )(*args)


def _conv_mixer_kernel(h_ref, g_ref, win_ref, cw_ref, wout_ref, o_ref, hn_buf, u_buf, v_buf):
    t = h_ref.shape[0]

    @pl.when(pl.program_id(1) == 0)
    def _():
        u_buf[0:8, :] = jnp.zeros((8, D_MODEL), jnp.float32)

    x = h_ref[...]
    hn_buf[...] = _rms(x, g_ref[...]).astype(jnp.bfloat16)
    cw = cw_ref[...]
    chunk = 512
    for c in range(0, D_MODEL, chunk):
        hn = hn_buf[...]
        gate_c = jnp.dot(hn, win_ref[:, D_MODEL + c:D_MODEL + c + chunk], preferred_element_type=jnp.float32)
        hh = jnp.dot(hn, win_ref[:, 2 * D_MODEL + c:2 * D_MODEL + c + chunk], preferred_element_type=jnp.float32)
        u_buf[8:8 + t, c:c + chunk] = gate_c * hh
        conv = (cw[0:1, c:c + chunk] * u_buf[6:6 + t, c:c + chunk]
                + cw[1:2, c:c + chunk] * u_buf[7:7 + t, c:c + chunk]
                + cw[2:3, c:c + chunk] * u_buf[8:8 + t, c:c + chunk])
        gate_b = jnp.dot(hn, win_ref[:, c:c + chunk], preferred_element_type=jnp.float32)
        v_buf[:, c:c + chunk] = (gate_b * conv).astype(jnp.bfloat16)
    u_buf[0:8, :] = u_buf[t:t + 8, :]
    o_ref[...] = x + jnp.dot(v_buf[...], wout_ref[...], preferred_element_type=jnp.float32)


def conv_mixer(h, batch, gain, w_in, conv_w, w_out):
    n = h.shape[0]
    seq = n // batch
    t = CONV_TILE
    ns = seq // t
    row = pl.BlockSpec((t, D_MODEL), lambda b, s: (b * ns + s, 0))
    return pl.pallas_call(
        _conv_mixer_kernel,
        grid=(batch, ns),
        in_specs=[row,
                  pl.BlockSpec((1, D_MODEL), lambda b, s: (0, 0)),
                  pl.BlockSpec((D_MODEL, 3 * D_MODEL), lambda b, s: (0, 0)),
                  pl.BlockSpec((CONV_WIDTH, D_MODEL), lambda b, s: (0, 0)),
                  pl.BlockSpec((D_MODEL, D_MODEL), lambda b, s: (0, 0))],
        out_specs=row,
        out_shape=jax.ShapeDtypeStruct((n, D_MODEL), jnp.float32),
        scratch_shapes=[pltpu.VMEM((t, D_MODEL), jnp.bfloat16),
                        pltpu.VMEM((t + 8, D_MODEL), jnp.float32),
                        pltpu.VMEM((t, D_MODEL), jnp.bfloat16)],
        compiler_params=_cparams("arbitrary", "arbitrary"),
        name="conv_mixer",
    )(h, gain.reshape(1, D_MODEL), w_in, conv_w, w_out)


def _qkv_kernel(h_ref, g_ref, w_ref, cos_ref, sin_ref, o_ref, z_buf, *, dil):
    t = h_ref.shape[0]
    half = D_MODEL // 2
    reps = half // LANES
    hn = _rms(h_ref[...], g_ref[...]).astype(jnp.bfloat16)
    cos = jnp.concatenate([cos_ref[...]] * reps, axis=1)
    sin = jnp.concatenate([sin_ref[...]] * reps, axis=1)
    for part in range(3):
        z = jnp.dot(hn, w_ref[:, part * D_MODEL:(part + 1) * D_MODEL], preferred_element_type=jnp.float32)
        if part < 2:
            z1 = z[:, :half]
            z2 = z[:, half:]
            z = jnp.concatenate([z1 * cos - z2 * sin, z2 * cos + z1 * sin], axis=1)
        if part == 0:
            z = z * (HEAD_DIM ** -0.5 * LOG2_E)
        cols = slice(part * D_MODEL, (part + 1) * D_MODEL)
        if dil == 1:
            o_ref[0, 0, :, cols] = z.astype(o_ref.dtype)
        else:
            for c in range(D_MODEL // LANES):
                z_buf[c] = z[:, c * LANES:(c + 1) * LANES]
            for r in range(dil):
                o_ref[0, r, :, cols] = jnp.concatenate(
                    [z_buf[c, pl.ds(r, t // dil, stride=dil), :] for c in range(D_MODEL // LANES)],
                    axis=1).astype(o_ref.dtype)


def qkv_project(h, batch, gain, w_group, cos, sin, dil):
    n = h.shape[0]
    seq = n // batch
    t = PROJ_TILE
    ns = seq // t
    row = lambda width: pl.BlockSpec((t, width), lambda b, s: (b * ns + s, 0))
    return pl.pallas_call(
        functools.partial(_qkv_kernel, dil=dil),
        grid=(batch, ns),
        in_specs=[row(D_MODEL),
                  pl.BlockSpec((1, D_MODEL), lambda b, s: (0, 0)),
                  pl.BlockSpec((D_MODEL, 3 * D_MODEL), lambda b, s: (0, 0)),
                  row(LANES), row(LANES)],
        out_specs=pl.BlockSpec((1, dil, t // dil, 3 * D_MODEL), lambda b, s: (b, 0, s, 0)),
        out_shape=jax.ShapeDtypeStruct((batch, dil, seq // dil, 3 * D_MODEL), jnp.bfloat16),
        scratch_shapes=[pltpu.VMEM((D_MODEL // LANES, t, LANES), jnp.float32)],
        compiler_params=_cparams("parallel", "parallel"),
        name=f"qkv_project_d{dil}",
    )(h, gain.reshape(1, D_MODEL), w_group, cos, sin)


def _attention_kernel(q_ref, k_ref, v_ref, o_ref, l_ref, k_win, v_win):
    blk = ATT_BLK
    n = pl.program_id(2)
    heads_per_slab = LANES // HALF_DIM
    n_slabs = N_HEADS // heads_per_slab
    rows = heads_per_slab * blk

    @pl.when(n == 0)
    def _():
        k_win[0:blk, :] = jnp.zeros((blk, D_MODEL), k_win.dtype)
        v_win[0:blk, :] = jnp.zeros((blk, D_MODEL), v_win.dtype)

    k_win[blk:2 * blk, :] = k_ref[0, 0]
    v_win[blk:2 * blk, :] = v_ref[0, 0]

    qi = lax.broadcasted_iota(jnp.int32, (rows, 2 * blk), 0) % blk
    ki = lax.broadcasted_iota(jnp.int32, (rows, 2 * blk), 1)
    dist = qi + blk - ki
    valid = (dist >= 0) & (dist <= blk) & ((ki >= blk) | (n > 0))
    bias = jnp.where(valid, 0.0, NEG_INF).astype(jnp.float32)

    half = D_MODEL // 2
    lane2 = lax.broadcasted_iota(jnp.int32, (blk, 2 * LANES), 1)
    lane1 = lax.broadcasted_iota(jnp.int32, (blk, LANES), 1)

    def scores(m):
        q_slab = jnp.concatenate([q_ref[0, 0, :, m * LANES:(m + 1) * LANES],
                                  q_ref[0, 0, :, half + m * LANES:half + (m + 1) * LANES]], axis=1)
        k_slab = jnp.concatenate([k_win[:, m * LANES:(m + 1) * LANES],
                                  k_win[:, half + m * LANES:half + (m + 1) * LANES]], axis=1)
        zero = jnp.zeros_like(q_slab)
        q_heads = jnp.concatenate(
            [jnp.where(((lane2 % LANES) // HALF_DIM) == a, q_slab, zero) for a in range(heads_per_slab)], axis=0)
        return lax.dot_general(q_heads, k_slab, (((1,), (1,)), ((), ())),
                               preferred_element_type=jnp.float32) + bias

    lse_out = jnp.zeros((blk, LANES), jnp.float32)
    s_next = scores(0)
    for m in range(n_slabs):
        s = s_next
        if m + 1 < n_slabs:
            s_next = scores(m + 1)
        mx = jnp.max(s, axis=-1, keepdims=True)
        p = jnp.exp2(s - mx)
        den = jnp.sum(p, axis=-1, keepdims=True)
        pb = p.astype(jnp.bfloat16)
        inv = 1.0 / den
        lse = (mx + jnp.log2(den)) * LN_2
        for a in range(heads_per_slab):
            lse_out = jnp.where(lane1 == m * heads_per_slab + a, lse[a * blk:(a + 1) * blk], lse_out)
        for pair in range(heads_per_slab // 2):
            col = (m * heads_per_slab + pair * 2) * HEAD_DIM
            r0 = pair * 2 * blk
            o2 = jnp.dot(pb[r0:r0 + 2 * blk], v_win[:, col:col + LANES],
                         preferred_element_type=jnp.float32) * inv[r0:r0 + 2 * blk]
            o_ref[0, 0, :, col:col + LANES] = jnp.where(lane1 < HEAD_DIM, o2[:blk], o2[blk:]).astype(o_ref.dtype)
    l_ref[0, 0] = lse_out
    k_win[0:blk, :] = k_win[blk:2 * blk, :]
    v_win[0:blk, :] = v_win[blk:2 * blk, :]


def dilated_attention_group(qkv_g, dil):
    batch, _, length, _ = qkv_g.shape
    blk = ATT_BLK
    nblk = length // blk

    def part_spec(part):
        return pl.BlockSpec((1, 1, blk, D_MODEL), lambda b, r, i: (b, r, i, part))

    return pl.pallas_call(
        _attention_kernel,
        grid=(batch, dil, nblk),
        in_specs=[part_spec(0), part_spec(1), part_spec(2)],
        out_specs=[pl.BlockSpec((1, 1, blk, D_MODEL), lambda b, r, i: (b, r, i, 0)),
                   pl.BlockSpec((1, 1, blk, LANES), lambda b, r, i: (b, r, i, 0))],
        out_shape=[jax.ShapeDtypeStruct((batch, dil, length, D_MODEL), jnp.bfloat16),
                   jax.ShapeDtypeStruct((batch, dil, length, LANES), jnp.float32)],
        scratch_shapes=[pltpu.VMEM((2 * blk, D_MODEL), jnp.bfloat16),
                        pltpu.VMEM((2 * blk, D_MODEL), jnp.bfloat16)],
        compiler_params=_cparams("arbitrary", "arbitrary", "arbitrary"),
        name=f"dilated_attention_d{dil}",
    )(qkv_g, qkv_g, qkv_g)


def _merge_proj_kernel(*refs, dils):
    ng = len(dils)
    h_ref = refs[0]
    o_refs = refs[1:1 + ng]
    l_refs = refs[1 + ng:1 + 2 * ng]
    e_ref, w_ref, out_ref, o_buf, l_buf = refs[1 + 2 * ng:]
    t = h_ref.shape[0]
    lses = []
    for g, dil in enumerate(dils):
        if dil == 1:
            lses.append(l_refs[g][0, 0])
        else:
            for r in range(dil):
                l_buf[pl.ds(r, t // dil, stride=dil), :] = l_refs[g][0, r]
            lses.append(l_buf[...])
    top = functools.reduce(jnp.maximum, lses)
    ex = [jnp.exp(l - top) for l in lses]
    inv = 1.0 / functools.reduce(jnp.add, ex)
    merged = jnp.zeros((t, D_MODEL), jnp.float32)
    for g, dil in enumerate(dils):
        wg = ex[g] * inv
        hi = wg.astype(jnp.bfloat16)
        lo = (wg - hi.astype(jnp.float32)).astype(jnp.bfloat16)
        wexp = (jnp.dot(hi, e_ref[...], preferred_element_type=jnp.float32)
                + jnp.dot(lo, e_ref[...], preferred_element_type=jnp.float32))
        if dil == 1:
            og = o_refs[g][0, 0].astype(jnp.float32)
        else:
            for r in range(dil):
                part = o_refs[g][0, r].astype(jnp.float32)
                for c in range(D_MODEL // LANES):
                    o_buf[c, pl.ds(r, t // dil, stride=dil), :] = part[:, c * LANES:(c + 1) * LANES]
            og = jnp.concatenate([o_buf[c] for c in range(D_MODEL // LANES)], axis=1)
        merged = merged + wexp * og
    out_ref[...] = h_ref[...] + jnp.dot(merged.astype(jnp.bfloat16), w_ref[...],
                                        preferred_element_type=jnp.float32)


def merge_proj_residual(h, batch, outs, lses, w_o):
    n = h.shape[0]
    seq = n // batch
    t = PROJ_TILE
    ns = seq // t
    dils = tuple(o.shape[1] for o in outs)
    row = pl.BlockSpec((t, D_MODEL), lambda b, s: (b * ns + s, 0))
    expand = np.zeros((LANES, D_MODEL), np.float32)
    for hd in range(N_HEADS):
        expand[hd, hd * HEAD_DIM:(hd + 1) * HEAD_DIM] = 1.0
    in_specs = [row]
    in_specs += [pl.BlockSpec((1, d, t // d, D_MODEL), lambda b, s: (b, 0, s, 0)) for d in dils]
    in_specs += [pl.BlockSpec((1, d, t // d, LANES), lambda b, s: (b, 0, s, 0)) for d in dils]
    in_specs += [pl.BlockSpec((LANES, D_MODEL), lambda b, s: (0, 0)),
                 pl.BlockSpec((D_MODEL, D_MODEL), lambda b, s: (0, 0))]
    return pl.pallas_call(
        functools.partial(_merge_proj_kernel, dils=dils),
        grid=(batch, ns), in_specs=in_specs, out_specs=row,
        out_shape=jax.ShapeDtypeStruct((n, D_MODEL), jnp.float32),
        scratch_shapes=[pltpu.VMEM((D_MODEL // LANES, t, LANES), jnp.float32),
                        pltpu.VMEM((t, LANES), jnp.float32)],
        compiler_params=_cparams("parallel", "parallel"), name="attn_merge_out_proj",
    )(h, *outs, *lses, jnp.asarray(expand, jnp.bfloat16), w_o)


def _router_kernel(h_ref, g_ref, w_ref, b_ref, hn_ref, route_ref):
    hn = _rms(h_ref[...], g_ref[...])
    hn_ref[...] = hn.astype(hn_ref.dtype)
    logits = jnp.dot(hn, w_ref[...], preferred_element_type=jnp.float32,
                     precision=lax.Precision.HIGHEST) + b_ref[...]
    lane = lax.broadcasted_iota(jnp.int32, logits.shape, 1)
    big = jnp.int32(LANES)

    def first_argmax(vals, vmax):
        return jnp.min(jnp.where(vals == vmax, lane, big), axis=-1, keepdims=True)

    coarse = jnp.where(lane < N_EXPERT_GROUPS, logits, NEG_INF)
    cmax = jnp.max(coarse, axis=-1, keepdims=True)
    p_top = 1.0 / jnp.sum(jnp.exp(coarse - cmax), axis=-1, keepdims=True)
    g_top = first_argmax(coarse, cmax)
    lo = N_EXPERT_GROUPS + g_top * EXPERTS_PER_GROUP
    fine = jnp.where((lane >= lo) & (lane < lo + EXPERTS_PER_GROUP), logits, NEG_INF)
    v1 = jnp.max(fine, axis=-1, keepdims=True)
    i1 = first_argmax(fine, v1)
    fine2 = jnp.where(lane == i1, NEG_INF, fine)
    v2 = jnp.max(fine2, axis=-1, keepdims=True)
    i2 = first_argmax(fine2, v2)
    e2 = jnp.exp(v2 - v1)
    w1 = 1.0 / (1.0 + e2)
    w2 = e2 / (1.0 + e2)
    route = jnp.where(lane == 0, (i1 - N_EXPERT_GROUPS).astype(jnp.float32), 0.0)
    route = jnp.where(lane == 1, (i2 - N_EXPERT_GROUPS).astype(jnp.float32), route)
    route = jnp.where(lane == 2, p_top * w1, route)
    route = jnp.where(lane == 3, p_top * w2, route)
    route_ref[...] = route


def moe_router(h, gain, w_route, b_route):
    n = h.shape[0]
    tr = ROW_TILE
    row = pl.BlockSpec((tr, D_MODEL), lambda i: (i, 0))
    return pl.pallas_call(
        _router_kernel, grid=(n // tr,),
        in_specs=[row,
                  pl.BlockSpec((1, D_MODEL), lambda i: (0, 0)),
                  pl.BlockSpec((D_MODEL, LANES), lambda i: (0, 0)),
                  pl.BlockSpec((1, LANES), lambda i: (0, 0))],
        out_specs=[row, pl.BlockSpec((tr, LANES), lambda i: (i, 0))],
        out_shape=[jax.ShapeDtypeStruct((n, D_MODEL), jnp.bfloat16),
                   jax.ShapeDtypeStruct((n, LANES), jnp.float32)],
        compiler_params=_cparams("parallel"), name="moe_router",
    )(h, gain.reshape(1, D_MODEL), w_route, b_route)


def _expert_kernel(te_ref, tr_ref, tv_ref, x_ref, wg_ref, wu_ref, wd_ref, o_ref):
    i = pl.program_id(0)

    @pl.when(tv_ref[i] > 0)
    def _():
        x = x_ref[...]
        g = jnp.dot(x, wg_ref[0], preferred_element_type=jnp.float32)
        u = jnp.dot(x, wu_ref[0], preferred_element_type=jnp.float32)
        hmid = (g * jax.nn.sigmoid(g) * u).astype(jnp.bfloat16)
        o_ref[...] = jnp.dot(hmid, wd_ref[0], preferred_element_type=jnp.float32).astype(o_ref.dtype)


def expert_mlp(x_sorted, tile_expert, tile_row, tile_valid, w_gate, w_up, w_down):
    p = x_sorted.shape[0]
    tm = EXPERT_TILE
    grid_spec = pltpu.PrefetchScalarGridSpec(
        num_scalar_prefetch=3,
        grid=(p // tm,),
        in_specs=[pl.BlockSpec((tm, D_MODEL), lambda i, te, tr, tv: (tr[i], 0)),
                  pl.BlockSpec((1, D_MODEL, EXPERT_FF), lambda i, te, tr, tv: (te[i], 0, 0)),
                  pl.BlockSpec((1, D_MODEL, EXPERT_FF), lambda i, te, tr, tv: (te[i], 0, 0)),
                  pl.BlockSpec((1, EXPERT_FF, D_MODEL), lambda i, te, tr, tv: (te[i], 0, 0))],
        out_specs=pl.BlockSpec((tm, D_MODEL), lambda i, te, tr, tv: (tr[i], 0)),
    )
    return pl.pallas_call(
        _expert_kernel, grid_spec=grid_spec,
        out_shape=jax.ShapeDtypeStruct((p, D_MODEL), jnp.bfloat16),
        compiler_params=_cparams("arbitrary"), name="expert_mlp",
    )(tile_expert, tile_row, tile_valid, x_sorted, w_gate, w_up, w_down)


def hierarchical_moe(h, gain, w_route, b_route, w_gate, w_up, w_down, final_gain=None):
    n = h.shape[0]
    tm = EXPERT_TILE
    hn, route = moe_router(h, gain, w_route, b_route)
    expert = route[:, 0:2].astype(jnp.int32).reshape(-1)
    onehot = (expert[:, None] == jnp.arange(N_EXPERTS, dtype=jnp.int32)[None, :]).astype(jnp.int32)
    csum = jnp.cumsum(onehot, axis=0)
    rank = jnp.sum(onehot * csum, axis=1) - 1
    counts = csum[-1]
    padded = ((counts + tm - 1) // tm) * tm
    seg_end = jnp.cumsum(padded)
    seg_start = seg_end - padded
    dest = seg_start[expert] + rank
    n_rows = 2 * n + N_EXPERTS * tm
    n_tiles = n_rows // tm
    used = seg_end[-1] // tm
    tile_id = jnp.arange(n_tiles, dtype=jnp.int32)
    tile_row = jnp.minimum(tile_id, used - 1).astype(jnp.int32)
    tile_expert = jnp.minimum(jnp.sum((seg_end[None, :] <= (tile_row * tm)[:, None]).astype(jnp.int32), axis=1),
                              N_EXPERTS - 1).astype(jnp.int32)
    tile_valid = (tile_id < used).astype(jnp.int32)
    token = jnp.arange(2 * n, dtype=jnp.int32) // 2
    src = jnp.zeros((n_rows,), jnp.int32).at[dest].set(token)
    x_sorted = jnp.take(hn, src, axis=0)
    y = expert_mlp(x_sorted, tile_expert, tile_row, tile_valid, w_gate, w_up, w_down)
    dest2 = dest.reshape(n, 2)
    y0 = jnp.take(y, dest2[:, 0], axis=0)
    y1 = jnp.take(y, dest2[:, 1], axis=0)
    gates = route[:, 2:4]
    gates = jnp.pad(gates, ((0, 0), (0, LANES - 2)))
    return moe_combine(h, y0, y1, gates, final_gain)


def _qk_column_permutation():
    perm = np.arange(QKV_WIDTH).reshape(N_ATTN_GROUPS, 3, N_HEADS, 2, HALF_DIM)
    perm[:, 0:2] = perm[:, 0:2].transpose(0, 1, 3, 2, 4).reshape(N_ATTN_GROUPS, 2, N_HEADS, 2, HALF_DIM)
    return perm.reshape(-1)


def kernel(x, positions, norm_mix, norm_ffn, norm_final, conv_w_in, conv_w, conv_w_out,
           attn_w_qkv, attn_w_o, w_coarse, b_coarse, w_fine, b_fine, w_gate, w_up, w_down):
    batch, seq, d = x.shape
    assert d == D_MODEL
    n = batch * seq
    depth = norm_mix.shape[0]
    bf = jnp.bfloat16
    h = x.reshape(n, d)
    cos, sin = rope_tables(positions.reshape(n))
    perm = jnp.asarray(_qk_column_permutation())
    for i in range(depth):
        j = i // 2
        if i % 2 == 0:
            h = conv_mixer(h, batch, norm_mix[i], conv_w_in[j].astype(bf), conv_w[j], conv_w_out[j].astype(bf))
        else:
            w_qkv = jnp.take(attn_w_qkv[j], perm, axis=1).astype(bf)
            outs, lses = [], []
            for g, (window, dil) in enumerate(DILATED_GROUPS):
                assert window // dil == ATT_BLK
                w_group = w_qkv[:, g * 3 * D_MODEL:(g + 1) * 3 * D_MODEL]
                o, lse = dilated_attention_group(
                    qkv_project(h, batch, norm_mix[i], w_group, cos, sin, dil), dil)
                outs.append(o)
                lses.append(lse)
            h = merge_proj_residual(h, batch, outs, lses, attn_w_o[j].astype(bf))
        w_route = jnp.concatenate(
            [w_coarse[i], jnp.transpose(w_fine[i], (1, 0, 2)).reshape(d, N_EXPERTS),
             jnp.zeros((d, LANES - N_EXPERT_GROUPS - N_EXPERTS), jnp.float32)], axis=1)
        b_route = jnp.concatenate(
            [b_coarse[i], b_fine[i].reshape(-1),
             jnp.zeros((LANES - N_EXPERT_GROUPS - N_EXPERTS,), jnp.float32)]).reshape(1, LANES)
        h = hierarchical_moe(h, norm_ffn[i], w_route, b_route,
                             w_gate[i].astype(bf), w_up[i].astype(bf), w_down[i].astype(bf),
                             norm_final if i == depth - 1 else None)
    return h.reshape(batch, seq, d)
```

```python
import functools
import math

import numpy as np
import jax
import jax.numpy as jnp
from jax import lax
from jax.experimental import pallas as pl
from jax.experimental.pallas import tpu as pltpu

D_MODEL = 1024
NORM_EPS = 1e-6
CONV_WIDTH = 3
HEAD_DIM = 64
HALF_DIM = HEAD_DIM // 2
N_HEADS = D_MODEL // HEAD_DIM
DILATED_GROUPS = ((128, 1), (512, 4), (2048, 16))
N_ATTN_GROUPS = len(DILATED_GROUPS)
QKV_WIDTH = N_ATTN_GROUPS * 3 * D_MODEL
ROPE_THETA = 10000.0
N_EXPERT_GROUPS = 4
EXPERTS_PER_GROUP = 4
N_EXPERTS = N_EXPERT_GROUPS * EXPERTS_PER_GROUP
EXPERT_FF = D_MODEL // 2
PACKED = D_MODEL // 2

LANES = 128
ATT_BLK = 128
VMEM_LIMIT_BYTES = 56 * 1024 * 1024
ROW_TILE = 1024
PROJ_TILE = 512
CONV_TILE = 512
EXPERT_TILE = 512
DISPATCH_TILE = 512
COMBINE_TILE = 256
RANK_BLOCK = 256
DMA_UNROLL = 8
NEG_INF = float("-inf")
LOG2_E = math.log2(math.e)
LN_2 = math.log(2.0)
R_E1, R_E2, R_G1, R_G2, R_RANK1, R_RANK2 = range(6)


def _cparams(*sem):
    return pltpu.CompilerParams(dimension_semantics=sem, vmem_limit_bytes=VMEM_LIMIT_BYTES)


def _rms(x, g):
    return x * lax.rsqrt(jnp.mean(x * x, axis=-1, keepdims=True) + NORM_EPS) * g


def _pack_rows(x):
    lo = pltpu.bitcast(x[:, :PACKED].astype(jnp.bfloat16).astype(jnp.float32), jnp.uint32)
    hi = pltpu.bitcast(x[:, PACKED:].astype(jnp.bfloat16).astype(jnp.float32), jnp.uint32)
    return (lo >> 16) | (hi & jnp.uint32(0xFFFF0000))


def _unpack_rows(u):
    lo = pltpu.bitcast(u << 16, jnp.float32)
    hi = pltpu.bitcast(u & jnp.uint32(0xFFFF0000), jnp.float32)
    return jnp.concatenate([lo, hi], axis=1)


def _rope_kernel(pos_ref, freq_ref, cos_ref, sin_ref):
    ang = pos_ref[...].astype(jnp.float32) * freq_ref[...]
    cos_ref[...] = jnp.cos(ang)
    sin_ref[...] = jnp.sin(ang)


def rope_tables(positions):
    n = positions.shape[0]
    inv_freq = (ROPE_THETA ** (-np.arange(0, HEAD_DIM, 2, dtype=np.float32) / HEAD_DIM)).astype(np.float32)
    freq = jnp.asarray(np.tile(inv_freq, LANES // HALF_DIM)[None, :])
    tr = ROW_TILE
    return pl.pallas_call(
        _rope_kernel,
        grid=(n // tr,),
        in_specs=[pl.BlockSpec((tr, 1), lambda i: (i, 0)),
                  pl.BlockSpec((1, LANES), lambda i: (0, 0))],
        out_specs=[pl.BlockSpec((tr, LANES), lambda i: (i, 0)),
                   pl.BlockSpec((tr, LANES), lambda i: (i, 0))],
        out_shape=[jax.ShapeDtypeStruct((n, LANES), jnp.float32)] * 2,
        compiler_params=_cparams("parallel"),
        name="rope_tables",
    )(positions.reshape(n, 1), freq)


def _conv_mixer_kernel(h_ref, g_ref, win_ref, cw_ref, wout_ref, o_ref, hn_buf, u_buf, v_buf):
    t = h_ref.shape[0]

    @pl.when(pl.program_id(1) == 0)
    def _():
        u_buf[0:8, :] = jnp.zeros((8, D_MODEL), jnp.float32)

    x = h_ref[...]
    hn_buf[...] = _rms(x, g_ref[...]).astype(jnp.bfloat16)
    cw = cw_ref[...]
    chunk = 512
    for c in range(0, D_MODEL, chunk):
        hn = hn_buf[...]
        gate_c = jnp.dot(hn, win_ref[:, D_MODEL + c:D_MODEL + c + chunk], preferred_element_type=jnp.float32)
        hh = jnp.dot(hn, win_ref[:, 2 * D_MODEL + c:2 * D_MODEL + c + chunk], preferred_element_type=jnp.float32)
        u_buf[8:8 + t, c:c + chunk] = gate_c * hh
        conv = (cw[0:1, c:c + chunk] * u_buf[6:6 + t, c:c + chunk]
                + cw[1:2, c:c + chunk] * u_buf[7:7 + t, c:c + chunk]
                + cw[2:3, c:c + chunk] * u_buf[8:8 + t, c:c + chunk])
        gate_b = jnp.dot(hn, win_ref[:, c:c + chunk], preferred_element_type=jnp.float32)
        v_buf[:, c:c + chunk] = (gate_b * conv).astype(jnp.bfloat16)
    u_buf[0:8, :] = u_buf[t:t + 8, :]
    o_ref[...] = x + jnp.dot(v_buf[...], wout_ref[...], preferred_element_type=jnp.float32)


def conv_mixer(h, batch, gain, w_in, conv_w, w_out):
    n = h.shape[0]
    seq = n // batch
    t = CONV_TILE
    ns = seq // t
    row = pl.BlockSpec((t, D_MODEL), lambda b, s: (b * ns + s, 0))
    return pl.pallas_call(
        _conv_mixer_kernel,
        grid=(batch, ns),
        in_specs=[row,
                  pl.BlockSpec((1, D_MODEL), lambda b, s: (0, 0)),
                  pl.BlockSpec((D_MODEL, 3 * D_MODEL), lambda b, s: (0, 0)),
                  pl.BlockSpec((CONV_WIDTH, D_MODEL), lambda b, s: (0, 0)),
                  pl.BlockSpec((D_MODEL, D_MODEL), lambda b, s: (0, 0))],
        out_specs=row,
        out_shape=jax.ShapeDtypeStruct((n, D_MODEL), jnp.float32),
        scratch_shapes=[pltpu.VMEM((t, D_MODEL), jnp.bfloat16),
                        pltpu.VMEM((t + 8, D_MODEL), jnp.float32),
                        pltpu.VMEM((t, D_MODEL), jnp.bfloat16)],
        compiler_params=_cparams("arbitrary", "arbitrary"),
        name="conv_mixer",
    )(h, gain.reshape(1, D_MODEL), w_in, conv_w, w_out)


def _qkv_kernel(h_ref, g_ref, w_ref, cos_ref, sin_ref, o_ref, z_buf, *, dil):
    t = h_ref.shape[0]
    half = D_MODEL // 2
    reps = half // LANES
    rows = t // dil

    def residue_major(x):
        if dil == 1:
            return x
        planes = x.shape[1] // LANES
        for c in range(planes):
            z_buf[c] = x[:, c * LANES:(c + 1) * LANES]
        return jnp.concatenate(
            [jnp.concatenate([z_buf[c, pl.ds(r, rows, stride=dil), :] for r in range(dil)], axis=0)
             for c in range(planes)], axis=1)

    cos128 = residue_major(cos_ref[...])
    sin128 = residue_major(sin_ref[...])
    cos = jnp.concatenate([cos128] * reps, axis=1)
    sin = jnp.concatenate([sin128] * reps, axis=1)
    hn = residue_major(_rms(h_ref[...], g_ref[...])).astype(jnp.bfloat16)
    for part in range(3):
        z = jnp.dot(hn, w_ref[:, part * D_MODEL:(part + 1) * D_MODEL], preferred_element_type=jnp.float32)
        if part < 2:
            z1 = z[:, :half]
            z2 = z[:, half:]
            z = jnp.concatenate([z1 * cos - z2 * sin, z2 * cos + z1 * sin], axis=1)
        if part == 0:
            z = z * (HEAD_DIM ** -0.5 * LOG2_E)
        z = z.astype(o_ref.dtype)
        for r in range(dil):
            o_ref[0, r, :, part * D_MODEL:(part + 1) * D_MODEL] = z[r * rows:(r + 1) * rows]


def qkv_project(h, batch, gain, w_group, cos, sin, dil):
    n = h.shape[0]
    seq = n // batch
    t = PROJ_TILE
    ns = seq // t
    row = lambda width: pl.BlockSpec((t, width), lambda b, s: (b * ns + s, 0))
    return pl.pallas_call(
        functools.partial(_qkv_kernel, dil=dil),
        grid=(batch, ns),
        in_specs=[row(D_MODEL),
                  pl.BlockSpec((1, D_MODEL), lambda b, s: (0, 0)),
                  pl.BlockSpec((D_MODEL, 3 * D_MODEL), lambda b, s: (0, 0)),
                  row(LANES), row(LANES)],
        out_specs=pl.BlockSpec((1, dil, t // dil, 3 * D_MODEL), lambda b, s: (b, 0, s, 0)),
        out_shape=jax.ShapeDtypeStruct((batch, dil, seq // dil, 3 * D_MODEL), jnp.bfloat16),
        scratch_shapes=[pltpu.VMEM((D_MODEL // LANES, t, LANES), jnp.float32)],
        compiler_params=_cparams("parallel", "parallel"),
        name=f"qkv_project_d{dil}",
    )(h, gain.reshape(1, D_MODEL), w_group, cos, sin)


def _attention_kernel(q_ref, k_ref, v_ref, o_ref, l_ref, k_win, v_win):
    blk = ATT_BLK
    n = pl.program_id(2)
    heads_per_slab = LANES // HALF_DIM
    n_slabs = N_HEADS // heads_per_slab
    rows = heads_per_slab * blk

    @pl.when(n == 0)
    def _():
        k_win[0:blk, :] = jnp.zeros((blk, D_MODEL), k_win.dtype)
        v_win[0:blk, :] = jnp.zeros((blk, D_MODEL), v_win.dtype)

    k_win[blk:2 * blk, :] = k_ref[0, 0]
    v_win[blk:2 * blk, :] = v_ref[0, 0]

    qi = lax.broadcasted_iota(jnp.int32, (rows, 2 * blk), 0) % blk
    ki = lax.broadcasted_iota(jnp.int32, (rows, 2 * blk), 1)
    dist = qi + blk - ki
    valid = (dist >= 0) & (dist <= blk) & ((ki >= blk) | (n > 0))
    bias = jnp.where(valid, 0.0, NEG_INF).astype(jnp.float32)

    half = D_MODEL // 2
    lane2 = lax.broadcasted_iota(jnp.int32, (blk, 2 * LANES), 1)
    lane1 = lax.broadcasted_iota(jnp.int32, (blk, LANES), 1)

    def scores(m):
        q_slab = jnp.concatenate([q_ref[0, 0, :, m * LANES:(m + 1) * LANES],
                                  q_ref[0, 0, :, half + m * LANES:half + (m + 1) * LANES]], axis=1)
        k_slab = jnp.concatenate([k_win[:, m * LANES:(m + 1) * LANES],
                                  k_win[:, half + m * LANES:half + (m + 1) * LANES]], axis=1)
        zero = jnp.zeros_like(q_slab)
        q_heads = jnp.concatenate(
            [jnp.where(((lane2 % LANES) // HALF_DIM) == a, q_slab, zero) for a in range(heads_per_slab)], axis=0)
        return lax.dot_general(q_heads, k_slab, (((1,), (1,)), ((), ())),
                               preferred_element_type=jnp.float32) + bias

    lse_out = jnp.zeros((blk, LANES), jnp.float32)
    s_next = scores(0)
    for m in range(n_slabs):
        s = s_next
        if m + 1 < n_slabs:
            s_next = scores(m + 1)
        mx = jnp.max(s, axis=-1, keepdims=True)
        p = jnp.exp2(s - mx)
        den = jnp.sum(p, axis=-1, keepdims=True)
        pb = p.astype(jnp.bfloat16)
        inv = 1.0 / den
        lse = (mx + jnp.log2(den)) * LN_2
        for a in range(heads_per_slab):
            lse_out = jnp.where(lane1 == m * heads_per_slab + a, lse[a * blk:(a + 1) * blk], lse_out)
        for pair in range(heads_per_slab // 2):
            col = (m * heads_per_slab + pair * 2) * HEAD_DIM
            r0 = pair * 2 * blk
            o2 = jnp.dot(pb[r0:r0 + 2 * blk], v_win[:, col:col + LANES],
                         preferred_element_type=jnp.float32) * inv[r0:r0 + 2 * blk]
            o_ref[0, 0, :, col:col + LANES] = jnp.where(lane1 < HEAD_DIM, o2[:blk], o2[blk:]).astype(o_ref.dtype)
    l_ref[0, 0] = lse_out
    k_win[0:blk, :] = k_win[blk:2 * blk, :]
    v_win[0:blk, :] = v_win[blk:2 * blk, :]


def dilated_attention_group(qkv_g, dil):
    batch, _, length, _ = qkv_g.shape
    blk = ATT_BLK
    nblk = length // blk

    def part_spec(part):
        return pl.BlockSpec((1, 1, blk, D_MODEL), lambda b, r, i: (b, r, i, part))

    return pl.pallas_call(
        _attention_kernel,
        grid=(batch, dil, nblk),
        in_specs=[part_spec(0), part_spec(1), part_spec(2)],
        out_specs=[pl.BlockSpec((1, 1, blk, D_MODEL), lambda b, r, i: (b, r, i, 0)),
                   pl.BlockSpec((1, 1, blk, LANES), lambda b, r, i: (b, r, i, 0))],
        out_shape=[jax.ShapeDtypeStruct((batch, dil, length, D_MODEL), jnp.bfloat16),
                   jax.ShapeDtypeStruct((batch, dil, length, LANES), jnp.float32)],
        scratch_shapes=[pltpu.VMEM((2 * blk, D_MODEL), jnp.bfloat16),
                        pltpu.VMEM((2 * blk, D_MODEL), jnp.bfloat16)],
        compiler_params=_cparams("arbitrary", "arbitrary", "arbitrary"),
        name=f"dilated_attention_d{dil}",
    )(qkv_g, qkv_g, qkv_g)


def _merge_proj_kernel(*refs, dils):
    ng = len(dils)
    h_ref = refs[0]
    o_refs = refs[1:1 + ng]
    l_refs = refs[1 + ng:1 + 2 * ng]
    e_ref, w_ref, out_ref, o_buf, l_buf = refs[1 + 2 * ng:]
    t = h_ref.shape[0]
    lses = []
    for g, dil in enumerate(dils):
        if dil == 1:
            lses.append(l_refs[g][0, 0])
        else:
            for r in range(dil):
                l_buf[pl.ds(r, t // dil, stride=dil), :] = l_refs[g][0, r]
            lses.append(l_buf[...])
    top = functools.reduce(jnp.maximum, lses)
    ex = [jnp.exp(l - top) for l in lses]
    inv = 1.0 / functools.reduce(jnp.add, ex)
    merged = jnp.zeros((t, D_MODEL), jnp.float32)
    for g, dil in enumerate(dils):
        wg = ex[g] * inv
        hi = wg.astype(jnp.bfloat16)
        lo = (wg - hi.astype(jnp.float32)).astype(jnp.bfloat16)
        wexp = (jnp.dot(hi, e_ref[...], preferred_element_type=jnp.float32)
                + jnp.dot(lo, e_ref[...], preferred_element_type=jnp.float32))
        if dil == 1:
            og = o_refs[g][0, 0].astype(jnp.float32)
        else:
            for r in range(dil):
                part = o_refs[g][0, r].astype(jnp.float32)
                for c in range(D_MODEL // LANES):
                    o_buf[c, pl.ds(r, t // dil, stride=dil), :] = part[:, c * LANES:(c + 1) * LANES]
            og = jnp.concatenate([o_buf[c] for c in range(D_MODEL // LANES)], axis=1)
        merged = merged + wexp * og
    out_ref[...] = h_ref[...] + jnp.dot(merged.astype(jnp.bfloat16), w_ref[...],
                                        preferred_element_type=jnp.float32)


def merge_proj_residual(h, batch, outs, lses, w_o):
    n = h.shape[0]
    seq = n // batch
    t = PROJ_TILE
    ns = seq // t
    dils = tuple(o.shape[1] for o in outs)
    row = pl.BlockSpec((t, D_MODEL), lambda b, s: (b * ns + s, 0))
    expand = np.zeros((LANES, D_MODEL), np.float32)
    for hd in range(N_HEADS):
        expand[hd, hd * HEAD_DIM:(hd + 1) * HEAD_DIM] = 1.0
    in_specs = [row]
    in_specs += [pl.BlockSpec((1, d, t // d, D_MODEL), lambda b, s: (b, 0, s, 0)) for d in dils]
    in_specs += [pl.BlockSpec((1, d, t // d, LANES), lambda b, s: (b, 0, s, 0)) for d in dils]
    in_specs += [pl.BlockSpec((LANES, D_MODEL), lambda b, s: (0, 0)),
                 pl.BlockSpec((D_MODEL, D_MODEL), lambda b, s: (0, 0))]
    return pl.pallas_call(
        functools.partial(_merge_proj_kernel, dils=dils),
        grid=(batch, ns), in_specs=in_specs, out_specs=row,
        out_shape=jax.ShapeDtypeStruct((n, D_MODEL), jnp.float32),
        scratch_shapes=[pltpu.VMEM((D_MODEL // LANES, t, LANES), jnp.float32),
                        pltpu.VMEM((t, LANES), jnp.float32)],
        compiler_params=_cparams("parallel", "parallel"), name="attn_merge_out_proj",
    )(h, *outs, *lses, jnp.asarray(expand, jnp.bfloat16), w_o)


def _router_kernel(h_ref, g_ref, w_ref, b_ref, tri_ref, hn_ref, route_ref, count_ref, run_ref):
    @pl.when(pl.program_id(0) == 0)
    def _():
        run_ref[...] = jnp.zeros_like(run_ref)

    hn = _rms(h_ref[...], g_ref[...])
    hn_ref[...] = _pack_rows(hn)
    hn_hi = hn.astype(jnp.bfloat16)
    hn_lo = (hn - hn_hi.astype(jnp.float32)).astype(jnp.bfloat16)
    both = jnp.dot(hn_hi, w_ref[...], preferred_element_type=jnp.float32)
    logits = (both[:, :LANES] + both[:, LANES:]
              + jnp.dot(hn_lo, w_ref[:, :LANES], preferred_element_type=jnp.float32) + b_ref[...])
    lane = lax.broadcasted_iota(jnp.int32, logits.shape, 1)
    big = jnp.int32(LANES)

    def first_argmax(vals, vmax):
        return jnp.min(jnp.where(vals == vmax, lane, big), axis=-1, keepdims=True)

    coarse = jnp.where(lane < N_EXPERT_GROUPS, logits, NEG_INF)
    cmax = jnp.max(coarse, axis=-1, keepdims=True)
    p_top = 1.0 / jnp.sum(jnp.exp(coarse - cmax), axis=-1, keepdims=True)
    g_top = first_argmax(coarse, cmax)
    lo = N_EXPERT_GROUPS + g_top * EXPERTS_PER_GROUP
    fine = jnp.where((lane >= lo) & (lane < lo + EXPERTS_PER_GROUP), logits, NEG_INF)
    v1 = jnp.max(fine, axis=-1, keepdims=True)
    i1 = first_argmax(fine, v1)
    fine2 = jnp.where(lane == i1, NEG_INF, fine)
    v2 = jnp.max(fine2, axis=-1, keepdims=True)
    i2 = first_argmax(fine2, v2)
    e2 = jnp.exp(v2 - v1)
    w1 = 1.0 / (1.0 + e2)
    w2 = e2 / (1.0 + e2)
    e_first = i1 - N_EXPERT_GROUPS
    e_second = i2 - N_EXPERT_GROUPS

    sel1 = lane == e_first
    sel2 = lane == e_second
    onehot = jnp.where(sel1 | sel2, 1.0, 0.0)
    sub = tri_ref.shape[0]
    run = run_ref[...]
    befores = []
    for r0 in range(0, onehot.shape[0], sub):
        oh = onehot[r0:r0 + sub]
        befores.append(jnp.dot(tri_ref[...], oh.astype(jnp.bfloat16), preferred_element_type=jnp.float32) + run)
        run = run + jnp.sum(oh, axis=0, keepdims=True)
    before = jnp.concatenate(befores, axis=0)
    rank1 = jnp.sum(jnp.where(sel1, before, 0.0), axis=-1, keepdims=True)
    rank2 = jnp.sum(jnp.where(sel2, before, 0.0), axis=-1, keepdims=True)
    run_ref[...] = run
    count_ref[...] = run

    route = jnp.where(lane == R_E1, e_first.astype(jnp.float32), 0.0)
    route = jnp.where(lane == R_E2, e_second.astype(jnp.float32), route)
    route = jnp.where(lane == R_G1, p_top * w1, route)
    route = jnp.where(lane == R_G2, p_top * w2, route)
    route = jnp.where(lane == R_RANK1, rank1, route)
    route = jnp.where(lane == R_RANK2, rank2, route)
    route_ref[...] = route


def moe_router(h, gain, w_route, b_route):
    n = h.shape[0]
    tr = ROW_TILE
    sub = RANK_BLOCK
    tri = jnp.asarray(np.tril(np.ones((sub, sub), np.float32), -1), jnp.bfloat16)
    w_hi = w_route.astype(jnp.bfloat16)
    w_lo = (w_route - w_hi.astype(jnp.float32)).astype(jnp.bfloat16)
    w_split = jnp.concatenate([w_hi, w_lo], axis=1)
    return pl.pallas_call(
        _router_kernel, grid=(n // tr,),
        in_specs=[pl.BlockSpec((tr, D_MODEL), lambda i: (i, 0)),
                  pl.BlockSpec((1, D_MODEL), lambda i: (0, 0)),
                  pl.BlockSpec((D_MODEL, 2 * LANES), lambda i: (0, 0)),
                  pl.BlockSpec((1, LANES), lambda i: (0, 0)),
                  pl.BlockSpec((sub, sub), lambda i: (0, 0))],
        out_specs=[pl.BlockSpec((tr, PACKED), lambda i: (i, 0)),
                   pl.BlockSpec((tr, LANES), lambda i: (i, 0)),
                   pl.BlockSpec((1, LANES), lambda i: (0, 0))],
        out_shape=[jax.ShapeDtypeStruct((n, PACKED), jnp.uint32),
                   jax.ShapeDtypeStruct((n, LANES), jnp.float32),
                   jax.ShapeDtypeStruct((1, LANES), jnp.float32)],
        scratch_shapes=[pltpu.VMEM((1, LANES), jnp.float32)],
        compiler_params=_cparams("arbitrary"), name="moe_router",
    )(h, gain.reshape(1, D_MODEL), w_split, b_route, tri)


def _dispatch_kernel(dest_ref, x_ref, init_ref, o_ref, sem):
    del init_ref
    t = x_ref.shape[0]

    def row_copy(j):
        return pltpu.make_async_copy(x_ref.at[pl.ds(j // 2, 1)], o_ref.at[pl.ds(dest_ref[0, 0, j], 1)], sem)

    for j in range(2 * t):
        row_copy(j).start()

    def wait(j, c):
        row_copy(0).wait()
        return c

    lax.fori_loop(0, 2 * t, wait, 0, unroll=DMA_UNROLL)


def moe_dispatch(x_packed, dest, n_rows):
    n = x_packed.shape[0]
    t = DISPATCH_TILE
    return pl.pallas_call(
        _dispatch_kernel, grid=(n // t,),
        in_specs=[pl.BlockSpec((1, 1, 2 * t), lambda i: (i, 0, 0), memory_space=pltpu.SMEM),
                  pl.BlockSpec((t, PACKED), lambda i: (i, 0)),
                  pl.BlockSpec(memory_space=pl.ANY)],
        out_specs=pl.BlockSpec(memory_space=pl.ANY),
        out_shape=jax.ShapeDtypeStruct((n_rows, PACKED), jnp.uint32),
        scratch_shapes=[pltpu.SemaphoreType.DMA(())],
        input_output_aliases={2: 0},
        compiler_params=_cparams("arbitrary"), name="moe_dispatch",
    )(dest.reshape(n // t, 1, 2 * t), x_packed, jnp.zeros((n_rows, PACKED), jnp.uint32))


def _expert_kernel(te_ref, tr_ref, tv_ref, tn_ref, x_ref, wg_ref, wu_ref, wd_ref, o_ref, wg_bf, wu_bf, wd_bf):
    i = pl.program_id(0)

    @pl.when(tn_ref[i] > 0)
    def _():
        wg_bf[...] = wg_ref[0, 0].astype(jnp.bfloat16)
        wu_bf[...] = wu_ref[0, 0].astype(jnp.bfloat16)
        wd_bf[...] = wd_ref[0, 0].astype(jnp.bfloat16)

    @pl.when(tv_ref[i] > 0)
    def _():
        x = _unpack_rows(x_ref[...]).astype(jnp.bfloat16)
        g = jnp.dot(x, wg_bf[...], preferred_element_type=jnp.float32)
        u = jnp.dot(x, wu_bf[...], preferred_element_type=jnp.float32)
        hmid = (g * jax.nn.sigmoid(g) * u).astype(jnp.bfloat16)
        o_ref[...] = _pack_rows(jnp.dot(hmid, wd_bf[...], preferred_element_type=jnp.float32))

    @pl.when(tv_ref[i] == 0)
    def _():
        o_ref[...] = jnp.zeros_like(o_ref)


def expert_mlp(x_sorted, tile_expert, tile_row, tile_valid, tile_new, layer, w_gate, w_up, w_down):
    p = x_sorted.shape[0]
    tm = EXPERT_TILE
    grid_spec = pltpu.PrefetchScalarGridSpec(
        num_scalar_prefetch=4,
        grid=(p // tm,),
        in_specs=[pl.BlockSpec((tm, PACKED), lambda i, te, tr, tv, tn: (tr[i], 0)),
                  pl.BlockSpec((1, 1, D_MODEL, EXPERT_FF), lambda i, te, tr, tv, tn: (layer, te[i], 0, 0)),
                  pl.BlockSpec((1, 1, D_MODEL, EXPERT_FF), lambda i, te, tr, tv, tn: (layer, te[i], 0, 0)),
                  pl.BlockSpec((1, 1, EXPERT_FF, D_MODEL), lambda i, te, tr, tv, tn: (layer, te[i], 0, 0))],
        out_specs=pl.BlockSpec((tm, PACKED), lambda i, te, tr, tv, tn: (i, 0)),
        scratch_shapes=[pltpu.VMEM((D_MODEL, EXPERT_FF), jnp.bfloat16),
                        pltpu.VMEM((D_MODEL, EXPERT_FF), jnp.bfloat16),
                        pltpu.VMEM((EXPERT_FF, D_MODEL), jnp.bfloat16)],
    )
    return pl.pallas_call(
        _expert_kernel, grid_spec=grid_spec,
        out_shape=jax.ShapeDtypeStruct((p, PACKED), jnp.uint32),
        compiler_params=_cparams("arbitrary"), name="expert_mlp",
    )(tile_expert, tile_row, tile_valid, tile_new, x_sorted, w_gate, w_up, w_down)


def _combine_kernel(dcur_ref, dnext_ref, h_ref, route_ref, gain_ref, y_ref, o_ref, y_buf, sem, *, final_norm):
    i = pl.program_id(0)
    last = pl.num_programs(0) - 1
    t = h_ref.shape[0]
    slot = i % 2

    def row_copy(dest_ref, s, j):
        return pltpu.make_async_copy(y_ref.at[pl.ds(dest_ref[0, 0, j], 1)],
                                     y_buf.at[s, j % 2, pl.ds(j // 2, 1)], sem.at[s])

    def start_all(dest_ref, s):
        for j in range(2 * t):
            row_copy(dest_ref, s, j).start()

    @pl.when(i == 0)
    def _():
        start_all(dcur_ref, 0)

    for s in range(2):
        @pl.when((i < last) & (slot == 1 - s))
        def _():
            start_all(dnext_ref, s)

    lax.fori_loop(0, 2 * t, lambda j, c: (row_copy(dcur_ref, slot, 0).wait(), c)[1], 0, unroll=DMA_UNROLL)
    route = route_ref[...]
    out = (h_ref[...] + route[:, R_G1:R_G1 + 1] * _unpack_rows(y_buf[slot, 0])
           + route[:, R_G2:R_G2 + 1] * _unpack_rows(y_buf[slot, 1]))
    if final_norm:
        out = _rms(out, gain_ref[...])
    o_ref[...] = out


def moe_combine(h, route, y, dest, final_gain):
    n = h.shape[0]
    t = COMBINE_TILE
    nt = n // t
    dest3 = dest.reshape(nt, 1, 2 * t)
    final_norm = final_gain is not None
    gain = final_gain if final_norm else jnp.ones((D_MODEL,), jnp.float32)
    return pl.pallas_call(
        functools.partial(_combine_kernel, final_norm=final_norm), grid=(nt,),
        in_specs=[pl.BlockSpec((1, 1, 2 * t), lambda i: (i, 0, 0), memory_space=pltpu.SMEM),
                  pl.BlockSpec((1, 1, 2 * t), lambda i: (jnp.minimum(i + 1, nt - 1), 0, 0),
                               memory_space=pltpu.SMEM),
                  pl.BlockSpec((t, D_MODEL), lambda i: (i, 0)),
                  pl.BlockSpec((t, LANES), lambda i: (i, 0)),
                  pl.BlockSpec((1, D_MODEL), lambda i: (0, 0)),
                  pl.BlockSpec(memory_space=pl.ANY)],
        out_specs=pl.BlockSpec((t, D_MODEL), lambda i: (i, 0)),
        out_shape=jax.ShapeDtypeStruct((n, D_MODEL), jnp.float32),
        scratch_shapes=[pltpu.VMEM((2, 2, t, PACKED), jnp.uint32), pltpu.SemaphoreType.DMA((2,))],
        compiler_params=_cparams("arbitrary"), name="moe_combine",
    )(dest3, dest3, h, route, gain.reshape(1, D_MODEL), y)


def hierarchical_moe(h, gain, w_route, b_route, layer, w_gate, w_up, w_down, final_gain=None):
    n = h.shape[0]
    tm = EXPERT_TILE
    x_packed, route, counts = moe_router(h, gain, w_route, b_route)
    counts = counts[0, :N_EXPERTS].astype(jnp.int32)
    padded = ((counts + tm - 1) // tm) * tm
    seg_end = jnp.cumsum(padded)
    seg_start = seg_end - padded
    expert = route[:, R_E1:R_E2 + 1].astype(jnp.int32)
    rank = route[:, R_RANK1:R_RANK2 + 1].astype(jnp.int32)
    onehot = expert[:, :, None] == jnp.arange(N_EXPERTS, dtype=jnp.int32)[None, None, :]
    dest = jnp.sum(jnp.where(onehot, seg_start[None, None, :], 0), axis=-1) + rank
    n_rows = 2 * n + N_EXPERTS * tm
    n_tiles = n_rows // tm
    used = seg_end[-1] // tm
    tile_id = jnp.arange(n_tiles, dtype=jnp.int32)
    tile_row = jnp.minimum(tile_id, used - 1).astype(jnp.int32)
    tile_expert = jnp.minimum(jnp.sum((seg_end[None, :] <= (tile_row * tm)[:, None]).astype(jnp.int32), axis=1),
                              N_EXPERTS - 1).astype(jnp.int32)
    tile_valid = (tile_id < used).astype(jnp.int32)
    tile_new = jnp.concatenate([jnp.ones((1,), jnp.int32),
                                (tile_expert[1:] != tile_expert[:-1]).astype(jnp.int32)])
    x_sorted = moe_dispatch(x_packed, dest, n_rows)
    y = expert_mlp(x_sorted, tile_expert, tile_row, tile_valid, tile_new, layer, w_gate, w_up, w_down)
    return moe_combine(h, route, y, dest, final_gain)


def _qk_column_permutation():
    perm = np.arange(QKV_WIDTH).reshape(N_ATTN_GROUPS, 3, N_HEADS, 2, HALF_DIM)
    perm[:, 0:2] = perm[:, 0:2].transpose(0, 1, 3, 2, 4).reshape(N_ATTN_GROUPS, 2, N_HEADS, 2, HALF_DIM)
    return perm.reshape(-1)


def kernel(x, positions, norm_mix, norm_ffn, norm_final, conv_w_in, conv_w, conv_w_out,
           attn_w_qkv, attn_w_o, w_coarse, b_coarse, w_fine, b_fine, w_gate, w_up, w_down):
    batch, seq, d = x.shape
    assert d == D_MODEL
    n = batch * seq
    depth = norm_mix.shape[0]
    bf = jnp.bfloat16
    h = x.reshape(n, d)
    cos, sin = rope_tables(positions.reshape(n))
    perm = jnp.asarray(_qk_column_permutation())
    for i in range(depth):
        j = i // 2
        if i % 2 == 0:
            h = conv_mixer(h, batch, norm_mix[i], conv_w_in[j].astype(bf), conv_w[j], conv_w_out[j].astype(bf))
        else:
            w_qkv = jnp.take(attn_w_qkv[j], perm, axis=1).astype(bf)
            outs, lses = [], []
            for g, (window, dil) in enumerate(DILATED_GROUPS):
                assert window // dil == ATT_BLK
                w_group = w_qkv[:, g * 3 * D_MODEL:(g + 1) * 3 * D_MODEL]
                o, lse = dilated_attention_group(
                    qkv_project(h, batch, norm_mix[i], w_group, cos, sin, dil), dil)
                outs.append(o)
                lses.append(lse)
            h = merge_proj_residual(h, batch, outs, lses, attn_w_o[j].astype(bf))
        w_route = jnp.concatenate(
            [w_coarse[i], jnp.transpose(w_fine[i], (1, 0, 2)).reshape(d, N_EXPERTS),
             jnp.zeros((d, LANES - N_EXPERT_GROUPS - N_EXPERTS), jnp.float32)], axis=1)
        b_route = jnp.concatenate(
            [b_coarse[i], b_fine[i].reshape(-1),
             jnp.zeros((LANES - N_EXPERT_GROUPS - N_EXPERTS,), jnp.float32)]).reshape(1, LANES)
        h = hierarchical_moe(h, norm_ffn[i], w_route, b_route,
                             i, w_gate, w_up, w_down,
                             norm_final if i == depth - 1 else None)
    return h.reshape(batch, seq, d)
```

```python
import functools
import math

import numpy as np
import jax
import jax.numpy as jnp
from jax import lax
from jax.experimental import pallas as pl
from jax.experimental.pallas import tpu as pltpu

D_MODEL = 1024
NORM_EPS = 1e-6
CONV_WIDTH = 3
HEAD_DIM = 64
HALF_DIM = HEAD_DIM // 2
N_HEADS = D_MODEL // HEAD_DIM
DILATED_GROUPS = ((128, 1), (512, 4), (2048, 16))
N_ATTN_GROUPS = len(DILATED_GROUPS)
QKV_WIDTH = N_ATTN_GROUPS * 3 * D_MODEL
ROPE_THETA = 10000.0
N_EXPERT_GROUPS = 4
EXPERTS_PER_GROUP = 4
N_EXPERTS = N_EXPERT_GROUPS * EXPERTS_PER_GROUP
EXPERT_FF = D_MODEL // 2
PACKED = D_MODEL // 2

LANES = 128
ATT_BLK = 128
VMEM_LIMIT_BYTES = 56 * 1024 * 1024
ROW_TILE = 1024
PROJ_TILE = 512
CONV_TILE = 512
EXPERT_TILE = 256
DISPATCH_TILE = 512
COMBINE_TILE = 512
RANK_BLOCK = 256
DMA_UNROLL = 8
NEG_INF = float("-inf")
LOG2_E = math.log2(math.e)
LN_2 = math.log(2.0)
R_CLASS, R_RANK = range(2)
PAIR_SLOTS = ((0, 1), (0, 2), (0, 3), (1, 3), (1, 2), (3, 2))
N_CLASSES = N_EXPERT_GROUPS * len(PAIR_SLOTS)
ROW_WORDS = PACKED + LANES


def _cparams(*sem):
    return pltpu.CompilerParams(dimension_semantics=sem, vmem_limit_bytes=VMEM_LIMIT_BYTES)


def _rms(x, g):
    return x * lax.rsqrt(jnp.mean(x * x, axis=-1, keepdims=True) + NORM_EPS) * g


def _pack_rows(x):
    lo = pltpu.bitcast(x[:, :PACKED].astype(jnp.bfloat16).astype(jnp.float32), jnp.uint32)
    hi = pltpu.bitcast(x[:, PACKED:].astype(jnp.bfloat16).astype(jnp.float32), jnp.uint32)
    return (lo >> 16) | (hi & jnp.uint32(0xFFFF0000))


def _unpack_rows(u):
    lo = pltpu.bitcast(u << 16, jnp.float32)
    hi = pltpu.bitcast(u & jnp.uint32(0xFFFF0000), jnp.float32)
    return jnp.concatenate([lo, hi], axis=1)


def _rope_kernel(pos_ref, freq_ref, cos_ref, sin_ref):
    ang = pos_ref[...].astype(jnp.float32) * freq_ref[...]
    cos_ref[...] = jnp.cos(ang)
    sin_ref[...] = jnp.sin(ang)


def rope_tables(positions):
    n = positions.shape[0]
    inv_freq = (ROPE_THETA ** (-np.arange(0, HEAD_DIM, 2, dtype=np.float32) / HEAD_DIM)).astype(np.float32)
    freq = jnp.asarray(np.tile(inv_freq, LANES // HALF_DIM)[None, :])
    tr = ROW_TILE
    return pl.pallas_call(
        _rope_kernel,
        grid=(n // tr,),
        in_specs=[pl.BlockSpec((tr, 1), lambda i: (i, 0)),
                  pl.BlockSpec((1, LANES), lambda i: (0, 0))],
        out_specs=[pl.BlockSpec((tr, LANES), lambda i: (i, 0)),
                   pl.BlockSpec((tr, LANES), lambda i: (i, 0))],
        out_shape=[jax.ShapeDtypeStruct((n, LANES), jnp.float32)] * 2,
        compiler_params=_cparams("parallel"),
        name="rope_tables",
    )(positions.reshape(n, 1), freq)


def _conv_mixer_kernel(h_ref, g_ref, win_ref, cw_ref, wout_ref, o_ref, hn_buf, u_buf, v_buf):
    t = h_ref.shape[0]

    @pl.when(pl.program_id(1) == 0)
    def _():
        u_buf[0:8, :] = jnp.zeros((8, D_MODEL), jnp.float32)

    x = h_ref[...]
    hn_buf[...] = _rms(x, g_ref[...]).astype(jnp.bfloat16)
    cw = cw_ref[...]
    chunk = 512
    for c in range(0, D_MODEL, chunk):
        hn = hn_buf[...]
        gate_c = jnp.dot(hn, win_ref[:, D_MODEL + c:D_MODEL + c + chunk], preferred_element_type=jnp.float32)
        hh = jnp.dot(hn, win_ref[:, 2 * D_MODEL + c:2 * D_MODEL + c + chunk], preferred_element_type=jnp.float32)
        u_buf[8:8 + t, c:c + chunk] = gate_c * hh
        conv = (cw[0:1, c:c + chunk] * u_buf[6:6 + t, c:c + chunk]
                + cw[1:2, c:c + chunk] * u_buf[7:7 + t, c:c + chunk]
                + cw[2:3, c:c + chunk] * u_buf[8:8 + t, c:c + chunk])
        gate_b = jnp.dot(hn, win_ref[:, c:c + chunk], preferred_element_type=jnp.float32)
        v_buf[:, c:c + chunk] = (gate_b * conv).astype(jnp.bfloat16)
    u_buf[0:8, :] = u_buf[t:t + 8, :]
    o_ref[...] = x + jnp.dot(v_buf[...], wout_ref[...], preferred_element_type=jnp.float32)


def conv_mixer(h, batch, gain, w_in, conv_w, w_out):
    n = h.shape[0]
    seq = n // batch
    t = CONV_TILE
    ns = seq // t
    row = pl.BlockSpec((t, D_MODEL), lambda b, s: (b * ns + s, 0))
    return pl.pallas_call(
        _conv_mixer_kernel,
        grid=(batch, ns),
        in_specs=[row,
                  pl.BlockSpec((1, D_MODEL), lambda b, s: (0, 0)),
                  pl.BlockSpec((D_MODEL, 3 * D_MODEL), lambda b, s: (0, 0)),
                  pl.BlockSpec((CONV_WIDTH, D_MODEL), lambda b, s: (0, 0)),
                  pl.BlockSpec((D_MODEL, D_MODEL), lambda b, s: (0, 0))],
        out_specs=row,
        out_shape=jax.ShapeDtypeStruct((n, D_MODEL), jnp.float32),
        scratch_shapes=[pltpu.VMEM((t, D_MODEL), jnp.bfloat16),
                        pltpu.VMEM((t + 8, D_MODEL), jnp.float32),
                        pltpu.VMEM((t, D_MODEL), jnp.bfloat16)],
        compiler_params=_cparams("arbitrary", "arbitrary"),
        name="conv_mixer",
    )(h, gain.reshape(1, D_MODEL), w_in, conv_w, w_out)


def _qkv_kernel(h_ref, g_ref, w_ref, cos_ref, sin_ref, o_ref, z_buf, *, dil):
    t = h_ref.shape[0]
    half = D_MODEL // 2
    reps = half // LANES
    rows = t // dil

    def residue_major(x):
        if dil == 1:
            return x
        planes = x.shape[1] // LANES
        for c in range(planes):
            z_buf[c] = x[:, c * LANES:(c + 1) * LANES]
        return jnp.concatenate(
            [jnp.concatenate([z_buf[c, pl.ds(r, rows, stride=dil), :] for r in range(dil)], axis=0)
             for c in range(planes)], axis=1)

    cos128 = residue_major(cos_ref[...])
    sin128 = residue_major(sin_ref[...])
    cos = jnp.concatenate([cos128] * reps, axis=1)
    sin = jnp.concatenate([sin128] * reps, axis=1)
    hn = residue_major(_rms(h_ref[...], g_ref[...])).astype(jnp.bfloat16)
    for part in range(3):
        z = jnp.dot(hn, w_ref[:, part * D_MODEL:(part + 1) * D_MODEL], preferred_element_type=jnp.float32)
        if part < 2:
            z1 = z[:, :half]
            z2 = z[:, half:]
            z = jnp.concatenate([z1 * cos - z2 * sin, z2 * cos + z1 * sin], axis=1)
        if part == 0:
            z = z * (HEAD_DIM ** -0.5 * LOG2_E)
        z = z.astype(o_ref.dtype)
        for r in range(dil):
            o_ref[0, r, :, part * D_MODEL:(part + 1) * D_MODEL] = z[r * rows:(r + 1) * rows]


def qkv_project(h, batch, gain, w_group, cos, sin, dil):
    n = h.shape[0]
    seq = n // batch
    t = PROJ_TILE
    ns = seq // t
    row = lambda width: pl.BlockSpec((t, width), lambda b, s: (b * ns + s, 0))
    return pl.pallas_call(
        functools.partial(_qkv_kernel, dil=dil),
        grid=(batch, ns),
        in_specs=[row(D_MODEL),
                  pl.BlockSpec((1, D_MODEL), lambda b, s: (0, 0)),
                  pl.BlockSpec((D_MODEL, 3 * D_MODEL), lambda b, s: (0, 0)),
                  row(LANES), row(LANES)],
        out_specs=pl.BlockSpec((1, dil, t // dil, 3 * D_MODEL), lambda b, s: (b, 0, s, 0)),
        out_shape=jax.ShapeDtypeStruct((batch, dil, seq // dil, 3 * D_MODEL), jnp.bfloat16),
        scratch_shapes=[pltpu.VMEM((D_MODEL // LANES, t, LANES), jnp.float32)],
        compiler_params=_cparams("parallel", "parallel"),
        name=f"qkv_project_d{dil}",
    )(h, gain.reshape(1, D_MODEL), w_group, cos, sin)


def _attention_kernel(q_ref, k_ref, v_ref, o_ref, l_ref, k_win, v_win):
    blk = ATT_BLK
    n = pl.program_id(2)
    heads_per_slab = LANES // HALF_DIM
    n_slabs = N_HEADS // heads_per_slab
    rows = heads_per_slab * blk

    @pl.when(n == 0)
    def _():
        k_win[0:blk, :] = jnp.zeros((blk, D_MODEL), k_win.dtype)
        v_win[0:blk, :] = jnp.zeros((blk, D_MODEL), v_win.dtype)

    k_win[blk:2 * blk, :] = k_ref[0, 0]
    v_win[blk:2 * blk, :] = v_ref[0, 0]

    qi = lax.broadcasted_iota(jnp.int32, (rows, 2 * blk), 0) % blk
    ki = lax.broadcasted_iota(jnp.int32, (rows, 2 * blk), 1)
    dist = qi + blk - ki
    valid = (dist >= 0) & (dist <= blk) & ((ki >= blk) | (n > 0))
    bias = jnp.where(valid, 0.0, NEG_INF).astype(jnp.float32)

    half = D_MODEL // 2
    lane2 = lax.broadcasted_iota(jnp.int32, (blk, 2 * LANES), 1)
    lane1 = lax.broadcasted_iota(jnp.int32, (blk, LANES), 1)

    def scores(m):
        q_slab = jnp.concatenate([q_ref[0, 0, :, m * LANES:(m + 1) * LANES],
                                  q_ref[0, 0, :, half + m * LANES:half + (m + 1) * LANES]], axis=1)
        k_slab = jnp.concatenate([k_win[:, m * LANES:(m + 1) * LANES],
                                  k_win[:, half + m * LANES:half + (m + 1) * LANES]], axis=1)
        zero = jnp.zeros_like(q_slab)
        q_heads = jnp.concatenate(
            [jnp.where(((lane2 % LANES) // HALF_DIM) == a, q_slab, zero) for a in range(heads_per_slab)], axis=0)
        return lax.dot_general(q_heads, k_slab, (((1,), (1,)), ((), ())),
                               preferred_element_type=jnp.float32) + bias

    lse_out = jnp.zeros((blk, LANES), jnp.float32)
    s_next = scores(0)
    for m in range(n_slabs):
        s = s_next
        if m + 1 < n_slabs:
            s_next = scores(m + 1)
        mx = jnp.max(s, axis=-1, keepdims=True)
        p = jnp.exp2(s - mx)
        den = jnp.sum(p, axis=-1, keepdims=True)
        pb = p.astype(jnp.bfloat16)
        inv = 1.0 / den
        lse = (mx + jnp.log2(den)) * LN_2
        for a in range(heads_per_slab):
            lse_out = jnp.where(lane1 == m * heads_per_slab + a, lse[a * blk:(a + 1) * blk], lse_out)
        for pair in range(heads_per_slab // 2):
            col = (m * heads_per_slab + pair * 2) * HEAD_DIM
            r0 = pair * 2 * blk
            o2 = jnp.dot(pb[r0:r0 + 2 * blk], v_win[:, col:col + LANES],
                         preferred_element_type=jnp.float32) * inv[r0:r0 + 2 * blk]
            o_ref[0, 0, :, col:col + LANES] = jnp.where(lane1 < HEAD_DIM, o2[:blk], o2[blk:]).astype(o_ref.dtype)
    l_ref[0, 0] = lse_out
    k_win[0:blk, :] = k_win[blk:2 * blk, :]
    v_win[0:blk, :] = v_win[blk:2 * blk, :]


def dilated_attention_group(qkv_g, dil):
    batch, _, length, _ = qkv_g.shape
    blk = ATT_BLK
    nblk = length // blk

    def part_spec(part):
        return pl.BlockSpec((1, 1, blk, D_MODEL), lambda b, r, i: (b, r, i, part))

    return pl.pallas_call(
        _attention_kernel,
        grid=(batch, dil, nblk),
        in_specs=[part_spec(0), part_spec(1), part_spec(2)],
        out_specs=[pl.BlockSpec((1, 1, blk, D_MODEL), lambda b, r, i: (b, r, i, 0)),
                   pl.BlockSpec((1, 1, blk, LANES), lambda b, r, i: (b, r, i, 0))],
        out_shape=[jax.ShapeDtypeStruct((batch, dil, length, D_MODEL), jnp.bfloat16),
                   jax.ShapeDtypeStruct((batch, dil, length, LANES), jnp.float32)],
        scratch_shapes=[pltpu.VMEM((2 * blk, D_MODEL), jnp.bfloat16),
                        pltpu.VMEM((2 * blk, D_MODEL), jnp.bfloat16)],
        compiler_params=_cparams("arbitrary", "arbitrary", "arbitrary"),
        name=f"dilated_attention_d{dil}",
    )(qkv_g, qkv_g, qkv_g)


def _merge_proj_kernel(*refs, dils):
    ng = len(dils)
    h_ref = refs[0]
    o_refs = refs[1:1 + ng]
    l_refs = refs[1 + ng:1 + 2 * ng]
    e_ref, w_ref, out_ref, o_buf, l_buf = refs[1 + 2 * ng:]
    t = h_ref.shape[0]
    lses = []
    for g, dil in enumerate(dils):
        if dil == 1:
            lses.append(l_refs[g][0, 0])
        else:
            for r in range(dil):
                l_buf[pl.ds(r, t // dil, stride=dil), :] = l_refs[g][0, r]
            lses.append(l_buf[...])
    top = functools.reduce(jnp.maximum, lses)
    ex = [jnp.exp(l - top) for l in lses]
    inv = 1.0 / functools.reduce(jnp.add, ex)
    merged = jnp.zeros((t, D_MODEL), jnp.float32)
    for g, dil in enumerate(dils):
        wg = ex[g] * inv
        hi = wg.astype(jnp.bfloat16)
        lo = (wg - hi.astype(jnp.float32)).astype(jnp.bfloat16)
        wexp = (jnp.dot(hi, e_ref[...], preferred_element_type=jnp.float32)
                + jnp.dot(lo, e_ref[...], preferred_element_type=jnp.float32))
        if dil == 1:
            og = o_refs[g][0, 0].astype(jnp.float32)
        else:
            for r in range(dil):
                part = o_refs[g][0, r].astype(jnp.float32)
                for c in range(D_MODEL // LANES):
                    o_buf[c, pl.ds(r, t // dil, stride=dil), :] = part[:, c * LANES:(c + 1) * LANES]
            og = jnp.concatenate([o_buf[c] for c in range(D_MODEL // LANES)], axis=1)
        merged = merged + wexp * og
    out_ref[...] = h_ref[...] + jnp.dot(merged.astype(jnp.bfloat16), w_ref[...],
                                        preferred_element_type=jnp.float32)


def merge_proj_residual(h, batch, outs, lses, w_o):
    n = h.shape[0]
    seq = n // batch
    t = PROJ_TILE
    ns = seq // t
    dils = tuple(o.shape[1] for o in outs)
    row = pl.BlockSpec((t, D_MODEL), lambda b, s: (b * ns + s, 0))
    expand = np.zeros((LANES, D_MODEL), np.float32)
    for hd in range(N_HEADS):
        expand[hd, hd * HEAD_DIM:(hd + 1) * HEAD_DIM] = 1.0
    in_specs = [row]
    in_specs += [pl.BlockSpec((1, d, t // d, D_MODEL), lambda b, s: (b, 0, s, 0)) for d in dils]
    in_specs += [pl.BlockSpec((1, d, t // d, LANES), lambda b, s: (b, 0, s, 0)) for d in dils]
    in_specs += [pl.BlockSpec((LANES, D_MODEL), lambda b, s: (0, 0)),
                 pl.BlockSpec((D_MODEL, D_MODEL), lambda b, s: (0, 0))]
    return pl.pallas_call(
        functools.partial(_merge_proj_kernel, dils=dils),
        grid=(batch, ns), in_specs=in_specs, out_specs=row,
        out_shape=jax.ShapeDtypeStruct((n, D_MODEL), jnp.float32),
        scratch_shapes=[pltpu.VMEM((D_MODEL // LANES, t, LANES), jnp.float32),
                        pltpu.VMEM((t, LANES), jnp.float32)],
        compiler_params=_cparams("parallel", "parallel"), name="attn_merge_out_proj",
    )(h, *outs, *lses, jnp.asarray(expand, jnp.bfloat16), w_o)


def _router_kernel(h_ref, g_ref, w_ref, b_ref, tri_ref, hn_ref, route_ref, count_ref, run_ref):
    @pl.when(pl.program_id(0) == 0)
    def _():
        run_ref[...] = jnp.zeros_like(run_ref)

    hn = _rms(h_ref[...], g_ref[...])
    hn_ref[:, :PACKED] = _pack_rows(hn)
    hn_hi = hn.astype(jnp.bfloat16)
    hn_lo = (hn - hn_hi.astype(jnp.float32)).astype(jnp.bfloat16)
    both = jnp.dot(hn_hi, w_ref[...], preferred_element_type=jnp.float32)
    logits = (both[:, :LANES] + both[:, LANES:]
              + jnp.dot(hn_lo, w_ref[:, :LANES], preferred_element_type=jnp.float32) + b_ref[...])
    lane = lax.broadcasted_iota(jnp.int32, logits.shape, 1)
    big = jnp.int32(LANES)

    def first_argmax(vals, vmax):
        return jnp.min(jnp.where(vals == vmax, lane, big), axis=-1, keepdims=True)

    coarse = jnp.where(lane < N_EXPERT_GROUPS, logits, NEG_INF)
    cmax = jnp.max(coarse, axis=-1, keepdims=True)
    p_top = 1.0 / jnp.sum(jnp.exp(coarse - cmax), axis=-1, keepdims=True)
    g_top = first_argmax(coarse, cmax)
    lo = N_EXPERT_GROUPS + g_top * EXPERTS_PER_GROUP
    fine = jnp.where((lane >= lo) & (lane < lo + EXPERTS_PER_GROUP), logits, NEG_INF)
    v1 = jnp.max(fine, axis=-1, keepdims=True)
    i1 = first_argmax(fine, v1)
    fine2 = jnp.where(lane == i1, NEG_INF, fine)
    v2 = jnp.max(fine2, axis=-1, keepdims=True)
    i2 = first_argmax(fine2, v2)
    e2 = jnp.exp(v2 - v1)
    w1 = 1.0 / (1.0 + e2)
    w2 = e2 / (1.0 + e2)
    first_lo = i1 < i2
    a = jnp.where(first_lo, i1, i2) - lo
    b = jnp.where(first_lo, i2, i1) - lo
    pair = jnp.where(a == 0, b - 1, jnp.where(a == 1, jnp.where(b == 3, 3, 4), 5))
    cls = g_top * len(PAIR_SLOTS) + pair
    gate_lo = p_top * jnp.where(first_lo, w1, w2)
    gate_hi = p_top * jnp.where(first_lo, w2, w1)
    swapped = pair == 5
    gate_a = jnp.where(swapped, gate_hi, gate_lo)
    gate_b = jnp.where(swapped, gate_lo, gate_hi)
    gates = jnp.where(lane == 0, gate_a, jnp.where(lane == 1, gate_b, 0.0))
    hn_ref[:, PACKED:] = pltpu.bitcast(gates, jnp.uint32)

    sel = lane == cls
    onehot = jnp.where(sel, 1.0, 0.0)
    sub = tri_ref.shape[0]
    run = run_ref[...]
    befores = []
    for r0 in range(0, onehot.shape[0], sub):
        oh = onehot[r0:r0 + sub]
        befores.append(jnp.dot(tri_ref[...], oh.astype(jnp.bfloat16), preferred_element_type=jnp.float32) + run)
        run = run + jnp.sum(oh, axis=0, keepdims=True)
    before = jnp.concatenate(befores, axis=0)
    rank = jnp.sum(jnp.where(sel, before, 0.0), axis=-1, keepdims=True)
    run_ref[...] = run
    count_ref[...] = run

    route = jnp.where(lane == R_CLASS, cls.astype(jnp.float32), 0.0)
    route = jnp.where(lane == R_RANK, rank, route)
    route_ref[...] = route


def moe_router(h, gain, w_route, b_route):
    n = h.shape[0]
    tr = ROW_TILE
    sub = RANK_BLOCK
    tri = jnp.asarray(np.tril(np.ones((sub, sub), np.float32), -1), jnp.bfloat16)
    w_hi = w_route.astype(jnp.bfloat16)
    w_lo = (w_route - w_hi.astype(jnp.float32)).astype(jnp.bfloat16)
    w_split = jnp.concatenate([w_hi, w_lo], axis=1)
    return pl.pallas_call(
        _router_kernel, grid=(n // tr,),
        in_specs=[pl.BlockSpec((tr, D_MODEL), lambda i: (i, 0)),
                  pl.BlockSpec((1, D_MODEL), lambda i: (0, 0)),
                  pl.BlockSpec((D_MODEL, 2 * LANES), lambda i: (0, 0)),
                  pl.BlockSpec((1, LANES), lambda i: (0, 0)),
                  pl.BlockSpec((sub, sub), lambda i: (0, 0))],
        out_specs=[pl.BlockSpec((tr, ROW_WORDS), lambda i: (i, 0)),
                   pl.BlockSpec((tr, LANES), lambda i: (i, 0)),
                   pl.BlockSpec((1, LANES), lambda i: (0, 0))],
        out_shape=[jax.ShapeDtypeStruct((n, ROW_WORDS), jnp.uint32),
                   jax.ShapeDtypeStruct((n, LANES), jnp.float32),
                   jax.ShapeDtypeStruct((1, LANES), jnp.float32)],
        scratch_shapes=[pltpu.VMEM((1, LANES), jnp.float32)],
        compiler_params=_cparams("arbitrary"), name="moe_router",
    )(h, gain.reshape(1, D_MODEL), w_split, b_route, tri)


def _dispatch_kernel(dest_ref, x_ref, init_ref, o_ref, sem):
    del init_ref
    t = x_ref.shape[0]

    def row_copy(j):
        return pltpu.make_async_copy(x_ref.at[pl.ds(j, 1)], o_ref.at[pl.ds(dest_ref[0, 0, j], 1)], sem)

    for j in range(t):
        row_copy(j).start(priority=j % 2)

    def wait(j, c):
        row_copy(0).wait()
        return c

    lax.fori_loop(0, t, wait, 0, unroll=DMA_UNROLL)


def moe_dispatch(rows, dest, n_rows):
    n = rows.shape[0]
    t = DISPATCH_TILE
    return pl.pallas_call(
        _dispatch_kernel, grid=(n // t,),
        in_specs=[pl.BlockSpec((1, 1, t), lambda i: (i, 0, 0), memory_space=pltpu.SMEM),
                  pl.BlockSpec((t, ROW_WORDS), lambda i: (i, 0)),
                  pl.BlockSpec(memory_space=pl.ANY)],
        out_specs=pl.BlockSpec(memory_space=pl.ANY),
        out_shape=jax.ShapeDtypeStruct((n_rows, ROW_WORDS), jnp.uint32),
        scratch_shapes=[pltpu.SemaphoreType.DMA(())],
        input_output_aliases={2: 0},
        compiler_params=_cparams("arbitrary"), name="moe_dispatch",
    )(dest.reshape(n // t, 1, t), rows, jnp.zeros((n_rows, ROW_WORDS), jnp.uint32))


def _expert_kernel(ea_ref, eb_ref, na_ref, nb_ref, tr_ref, tv_ref, x_ref,
                   wga_ref, wua_ref, wda_ref, wgb_ref, wub_ref, wdb_ref, o_ref,
                   wga_bf, wua_bf, wda_bf, wgb_bf, wub_bf, wdb_bf):
    del ea_ref, eb_ref, tr_ref
    i = pl.program_id(0)

    @pl.when(na_ref[i] > 0)
    def _():
        wga_bf[...] = wga_ref[0, 0].astype(jnp.bfloat16)
        wua_bf[...] = wua_ref[0, 0].astype(jnp.bfloat16)
        wda_bf[...] = wda_ref[0, 0].astype(jnp.bfloat16)

    @pl.when(nb_ref[i] > 0)
    def _():
        wgb_bf[...] = wgb_ref[0, 0].astype(jnp.bfloat16)
        wub_bf[...] = wub_ref[0, 0].astype(jnp.bfloat16)
        wdb_bf[...] = wdb_ref[0, 0].astype(jnp.bfloat16)

    @pl.when(tv_ref[i] > 0)
    def _():
        x = _unpack_rows(x_ref[:, :PACKED]).astype(jnp.bfloat16)
        gates = pltpu.bitcast(x_ref[:, PACKED:], jnp.float32)

        def mlp(wg, wu, wd):
            g = jnp.dot(x, wg[...], preferred_element_type=jnp.float32)
            u = jnp.dot(x, wu[...], preferred_element_type=jnp.float32)
            hmid = (g * jax.nn.sigmoid(g) * u).astype(jnp.bfloat16)
            return jnp.dot(hmid, wd[...], preferred_element_type=jnp.float32)

        o_ref[...] = _pack_rows(gates[:, 0:1] * mlp(wga_bf, wua_bf, wda_bf)
                                + gates[:, 1:2] * mlp(wgb_bf, wub_bf, wdb_bf))

    @pl.when(tv_ref[i] == 0)
    def _():
        o_ref[...] = jnp.zeros_like(o_ref)


def expert_mlp(x_sorted, tables, layer, w_gate, w_up, w_down):
    p = x_sorted.shape[0]
    tm = EXPERT_TILE

    def weight_spec(shape, slot):
        return pl.BlockSpec((1, 1) + shape, lambda i, ea, eb, na, nb, tr, tv: (layer, (ea, eb)[slot][i], 0, 0))

    up_shape, down_shape = (D_MODEL, EXPERT_FF), (EXPERT_FF, D_MODEL)
    grid_spec = pltpu.PrefetchScalarGridSpec(
        num_scalar_prefetch=6,
        grid=(p // tm,),
        in_specs=[pl.BlockSpec((tm, ROW_WORDS), lambda i, ea, eb, na, nb, tr, tv: (tr[i], 0)),
                  weight_spec(up_shape, 0), weight_spec(up_shape, 0), weight_spec(down_shape, 0),
                  weight_spec(up_shape, 1), weight_spec(up_shape, 1), weight_spec(down_shape, 1)],
        out_specs=pl.BlockSpec((tm, PACKED), lambda i, ea, eb, na, nb, tr, tv: (i, 0)),
        scratch_shapes=[pltpu.VMEM(up_shape, jnp.bfloat16), pltpu.VMEM(up_shape, jnp.bfloat16),
                        pltpu.VMEM(down_shape, jnp.bfloat16)] * 2,
    )
    return pl.pallas_call(
        _expert_kernel, grid_spec=grid_spec,
        out_shape=jax.ShapeDtypeStruct((p, PACKED), jnp.uint32),
        compiler_params=_cparams("arbitrary"), name="expert_mlp",
    )(*tables, x_sorted, w_gate, w_up, w_down, w_gate, w_up, w_down)


def _combine_kernel(dcur_ref, dnext_ref, h_ref, gain_ref, y_ref, o_ref, y_buf, sem, *, final_norm):
    i = pl.program_id(0)
    last = pl.num_programs(0) - 1
    t = h_ref.shape[0]
    slot = i % 2

    def row_copy(dest_ref, s, j):
        return pltpu.make_async_copy(y_ref.at[pl.ds(dest_ref[0, 0, j], 1)],
                                     y_buf.at[s, pl.ds(j, 1)], sem.at[s])

    def start_all(dest_ref, s):
        for j in range(t):
            row_copy(dest_ref, s, j).start(priority=j % 2)

    @pl.when(i == 0)
    def _():
        start_all(dcur_ref, 0)

    for s in range(2):
        @pl.when((i < last) & (slot == 1 - s))
        def _():
            start_all(dnext_ref, s)

    lax.fori_loop(0, t, lambda j, c: (row_copy(dcur_ref, slot, 0).wait(), c)[1], 0, unroll=DMA_UNROLL)
    out = h_ref[...] + _unpack_rows(y_buf[slot])
    if final_norm:
        out = _rms(out, gain_ref[...])
    o_ref[...] = out


def moe_combine(h, y, dest, final_gain):
    n = h.shape[0]
    t = COMBINE_TILE
    nt = n // t
    dest3 = dest.reshape(nt, 1, t)
    final_norm = final_gain is not None
    gain = final_gain if final_norm else jnp.ones((D_MODEL,), jnp.float32)
    return pl.pallas_call(
        functools.partial(_combine_kernel, final_norm=final_norm), grid=(nt,),
        in_specs=[pl.BlockSpec((1, 1, t), lambda i: (i, 0, 0), memory_space=pltpu.SMEM),
                  pl.BlockSpec((1, 1, t), lambda i: (jnp.minimum(i + 1, nt - 1), 0, 0),
                               memory_space=pltpu.SMEM),
                  pl.BlockSpec((t, D_MODEL), lambda i: (i, 0)),
                  pl.BlockSpec((1, D_MODEL), lambda i: (0, 0)),
                  pl.BlockSpec(memory_space=pl.ANY)],
        out_specs=pl.BlockSpec((t, D_MODEL), lambda i: (i, 0)),
        out_shape=jax.ShapeDtypeStruct((n, D_MODEL), jnp.float32),
        scratch_shapes=[pltpu.VMEM((2, t, PACKED), jnp.uint32), pltpu.SemaphoreType.DMA((2,))],
        compiler_params=_cparams("arbitrary"), name="moe_combine",
    )(dest3, dest3, h, gain.reshape(1, D_MODEL), y)


def hierarchical_moe(h, gain, w_route, b_route, layer, w_gate, w_up, w_down, final_gain=None):
    n = h.shape[0]
    tm = EXPERT_TILE
    rows, route, counts = moe_router(h, gain, w_route, b_route)
    counts = counts[0, :N_CLASSES].astype(jnp.int32)
    padded = ((counts + tm - 1) // tm) * tm
    seg_end = jnp.cumsum(padded)
    seg_start = seg_end - padded
    cls = route[:, R_CLASS].astype(jnp.int32)
    rank = route[:, R_RANK].astype(jnp.int32)
    class_ids = jnp.arange(N_CLASSES, dtype=jnp.int32)
    dest = jnp.sum(jnp.where(cls[:, None] == class_ids[None, :], seg_start[None, :], 0), axis=-1) + rank
    n_rows = n + N_CLASSES * tm
    n_tiles = n_rows // tm
    used = seg_end[-1] // tm
    tile_id = jnp.arange(n_tiles, dtype=jnp.int32)
    tile_row = jnp.minimum(tile_id, used - 1).astype(jnp.int32)
    tile_class = jnp.minimum(jnp.sum((seg_end[None, :] <= (tile_row * tm)[:, None]).astype(jnp.int32), axis=1),
                             N_CLASSES - 1)
    tile_valid = (tile_id < used).astype(jnp.int32)
    slots = np.asarray([[g * EXPERTS_PER_GROUP + s for s in pair]
                        for g in range(N_EXPERT_GROUPS) for pair in PAIR_SLOTS], np.int32)
    tile_onehot = (tile_class[:, None] == class_ids[None, :]).astype(jnp.int32)
    expert_a = jnp.sum(tile_onehot * slots[None, :, 0], axis=1).astype(jnp.int32)
    expert_b = jnp.sum(tile_onehot * slots[None, :, 1], axis=1).astype(jnp.int32)

    def changed(e):
        return jnp.concatenate([jnp.ones((1,), jnp.int32), (e[1:] != e[:-1]).astype(jnp.int32)])

    tables = (expert_a, expert_b, changed(expert_a), changed(expert_b), tile_row, tile_valid)
    x_sorted = moe_dispatch(rows, dest, n_rows)
    y = expert_mlp(x_sorted, tables, layer, w_gate, w_up, w_down)
    return moe_combine(h, y, dest, final_gain)


def _qk_column_permutation():
    perm = np.arange(QKV_WIDTH).reshape(N_ATTN_GROUPS, 3, N_HEADS, 2, HALF_DIM)
    perm[:, 0:2] = perm[:, 0:2].transpose(0, 1, 3, 2, 4).reshape(N_ATTN_GROUPS, 2, N_HEADS, 2, HALF_DIM)
    return perm.reshape(-1)


def kernel(x, positions, norm_mix, norm_ffn, norm_final, conv_w_in, conv_w, conv_w_out,
           attn_w_qkv, attn_w_o, w_coarse, b_coarse, w_fine, b_fine, w_gate, w_up, w_down):
    batch, seq, d = x.shape
    assert d == D_MODEL
    n = batch * seq
    depth = norm_mix.shape[0]
    bf = jnp.bfloat16
    h = x.reshape(n, d)
    cos, sin = rope_tables(positions.reshape(n))
    perm = jnp.asarray(_qk_column_permutation())
    for i in range(depth):
        j = i // 2
        if i % 2 == 0:
            h = conv_mixer(h, batch, norm_mix[i], conv_w_in[j].astype(bf), conv_w[j], conv_w_out[j].astype(bf))
        else:
            w_qkv = jnp.take(attn_w_qkv[j], perm, axis=1).astype(bf)
            outs, lses = [], []
            for g, (window, dil) in enumerate(DILATED_GROUPS):
                assert window // dil == ATT_BLK
                w_group = w_qkv[:, g * 3 * D_MODEL:(g + 1) * 3 * D_MODEL]
                o, lse = dilated_attention_group(
                    qkv_project(h, batch, norm_mix[i], w_group, cos, sin, dil), dil)
                outs.append(o)
                lses.append(lse)
            h = merge_proj_residual(h, batch, outs, lses, attn_w_o[j].astype(bf))
        w_route = jnp.concatenate(
            [w_coarse[i], jnp.transpose(w_fine[i], (1, 0, 2)).reshape(d, N_EXPERTS),
             jnp.zeros((d, LANES - N_EXPERT_GROUPS - N_EXPERTS), jnp.float32)], axis=1)
        b_route = jnp.concatenate(
            [b_coarse[i], b_fine[i].reshape(-1),
             jnp.zeros((LANES - N_EXPERT_GROUPS - N_EXPERTS,), jnp.float32)]).reshape(1, LANES)
        h = hierarchical_moe(h, norm_ffn[i], w_route, b_route,
                             i, w_gate, w_up, w_down,
                             norm_final if i == depth - 1 else None)
    return h.reshape(batch, seq, d)
```

```python
import functools
import math

import numpy as np
import jax
import jax.numpy as jnp
from jax import lax
from jax.experimental import pallas as pl
from jax.experimental.pallas import tpu as pltpu

D_MODEL = 1024
NORM_EPS = 1e-6
CONV_WIDTH = 3
HEAD_DIM = 64
HALF_DIM = HEAD_DIM // 2
N_HEADS = D_MODEL // HEAD_DIM
DILATED_GROUPS = ((128, 1), (512, 4), (2048, 16))
N_ATTN_GROUPS = len(DILATED_GROUPS)
QKV_WIDTH = N_ATTN_GROUPS * 3 * D_MODEL
ROPE_THETA = 10000.0
N_EXPERT_GROUPS = 4
EXPERTS_PER_GROUP = 4
N_EXPERTS = N_EXPERT_GROUPS * EXPERTS_PER_GROUP
EXPERT_FF = D_MODEL // 2
PACKED = D_MODEL // 2

LANES = 128
ATT_BLK = 128
ATT_BLOCKS_PER_STEP = 4
VMEM_LIMIT_BYTES = 56 * 1024 * 1024
ROW_TILE = 1024
PROJ_TILE = 512
CONV_TILE = 512
EXPERT_TILE = 256
DISPATCH_TILE = 512
COMBINE_TILE = 512
RANK_BLOCK = 256
DMA_UNROLL = 8
NEG_INF = float("-inf")
LOG2_E = math.log2(math.e)
LN_2 = math.log(2.0)
R_CLASS, R_RANK = range(2)
PAIR_SLOTS = ((0, 1), (0, 2), (0, 3), (1, 3), (1, 2), (3, 2))
N_CLASSES = N_EXPERT_GROUPS * len(PAIR_SLOTS)
ROW_SUBLANES = 8
DATA_SUBLANES = PACKED // LANES
X_SUBLANES = DATA_SUBLANES + 1


def _cparams(*sem):
    return pltpu.CompilerParams(dimension_semantics=sem, vmem_limit_bytes=VMEM_LIMIT_BYTES)


def _rms(x, g):
    return x * lax.rsqrt(jnp.mean(x * x, axis=-1, keepdims=True) + NORM_EPS) * g


def _pack_rows(x):
    lo = pltpu.bitcast(x[:, :PACKED].astype(jnp.bfloat16).astype(jnp.float32), jnp.uint32)
    hi = pltpu.bitcast(x[:, PACKED:].astype(jnp.bfloat16).astype(jnp.float32), jnp.uint32)
    return (lo >> 16) | (hi & jnp.uint32(0xFFFF0000))


def _unpack_rows(u):
    lo = pltpu.bitcast(u << 16, jnp.float32)
    hi = pltpu.bitcast(u & jnp.uint32(0xFFFF0000), jnp.float32)
    return jnp.concatenate([lo, hi], axis=1)


def _store_row_tiles(ref, packed, extra=None):
    t = packed.shape[0]
    planes = [packed[:, s * LANES:(s + 1) * LANES] for s in range(DATA_SUBLANES)]
    if extra is not None:
        planes.append(extra)
    planes += [jnp.zeros((t, LANES), jnp.uint32)] * (ROW_SUBLANES - len(planes))
    for s, plane in enumerate(planes):
        ref[pl.ds(s, t, stride=ROW_SUBLANES), :] = plane


def _load_row_tiles(ref, t):
    return jnp.concatenate([ref[pl.ds(s, t, stride=ROW_SUBLANES), :] for s in range(DATA_SUBLANES)], axis=1)


def _rope_kernel(pos_ref, freq_ref, cos_ref, sin_ref):
    ang = pos_ref[...].astype(jnp.float32) * freq_ref[...]
    cos_ref[...] = jnp.cos(ang)
    sin_ref[...] = jnp.sin(ang)


def rope_tables(positions):
    n = positions.shape[0]
    inv_freq = (ROPE_THETA ** (-np.arange(0, HEAD_DIM, 2, dtype=np.float32) / HEAD_DIM)).astype(np.float32)
    freq = jnp.asarray(np.tile(inv_freq, LANES // HALF_DIM)[None, :])
    tr = ROW_TILE
    return pl.pallas_call(
        _rope_kernel,
        grid=(n // tr,),
        in_specs=[pl.BlockSpec((tr, 1), lambda i: (i, 0)),
                  pl.BlockSpec((1, LANES), lambda i: (0, 0))],
        out_specs=[pl.BlockSpec((tr, LANES), lambda i: (i, 0)),
                   pl.BlockSpec((tr, LANES), lambda i: (i, 0))],
        out_shape=[jax.ShapeDtypeStruct((n, LANES), jnp.float32)] * 2,
        compiler_params=_cparams("parallel"),
        name="rope_tables",
    )(positions.reshape(n, 1), freq)


def _conv_mixer_kernel(h_ref, g_ref, win_ref, cw_ref, wout_ref, o_ref, hn_buf, u_buf, v_buf):
    t = h_ref.shape[0]

    @pl.when(pl.program_id(1) == 0)
    def _():
        u_buf[0:8, :] = jnp.zeros((8, D_MODEL), jnp.float32)

    x = h_ref[...]
    hn_buf[...] = _rms(x, g_ref[...]).astype(jnp.bfloat16)
    cw = cw_ref[...]
    chunk = 512
    for c in range(0, D_MODEL, chunk):
        hn = hn_buf[...]
        gate_c = jnp.dot(hn, win_ref[:, D_MODEL + c:D_MODEL + c + chunk], preferred_element_type=jnp.float32)
        hh = jnp.dot(hn, win_ref[:, 2 * D_MODEL + c:2 * D_MODEL + c + chunk], preferred_element_type=jnp.float32)
        u_buf[8:8 + t, c:c + chunk] = gate_c * hh
        conv = (cw[0:1, c:c + chunk] * u_buf[6:6 + t, c:c + chunk]
                + cw[1:2, c:c + chunk] * u_buf[7:7 + t, c:c + chunk]
                + cw[2:3, c:c + chunk] * u_buf[8:8 + t, c:c + chunk])
        gate_b = jnp.dot(hn, win_ref[:, c:c + chunk], preferred_element_type=jnp.float32)
        v_buf[:, c:c + chunk] = (gate_b * conv).astype(jnp.bfloat16)
    u_buf[0:8, :] = u_buf[t:t + 8, :]
    o_ref[...] = x + jnp.dot(v_buf[...], wout_ref[...], preferred_element_type=jnp.float32)


def conv_mixer(h, batch, gain, w_in, conv_w, w_out):
    n = h.shape[0]
    seq = n // batch
    t = CONV_TILE
    ns = seq // t
    row = pl.BlockSpec((t, D_MODEL), lambda b, s: (b * ns + s, 0))
    return pl.pallas_call(
        _conv_mixer_kernel,
        grid=(batch, ns),
        in_specs=[row,
                  pl.BlockSpec((1, D_MODEL), lambda b, s: (0, 0)),
                  pl.BlockSpec((D_MODEL, 3 * D_MODEL), lambda b, s: (0, 0)),
                  pl.BlockSpec((CONV_WIDTH, D_MODEL), lambda b, s: (0, 0)),
                  pl.BlockSpec((D_MODEL, D_MODEL), lambda b, s: (0, 0))],
        out_specs=row,
        out_shape=jax.ShapeDtypeStruct((n, D_MODEL), jnp.float32),
        scratch_shapes=[pltpu.VMEM((t, D_MODEL), jnp.bfloat16),
                        pltpu.VMEM((t + 8, D_MODEL), jnp.float32),
                        pltpu.VMEM((t, D_MODEL), jnp.bfloat16)],
        compiler_params=_cparams("arbitrary", "arbitrary"),
        name="conv_mixer",
    )(h, gain.reshape(1, D_MODEL), w_in, conv_w, w_out)


def _qkv_kernel(h_ref, g_ref, w_ref, cos_ref, sin_ref, o_ref, z_buf, *, dil):
    t = h_ref.shape[0]
    half = D_MODEL // 2
    reps = half // LANES
    rows = t // dil

    def residue_major(x):
        if dil == 1:
            return x
        planes = x.shape[1] // LANES
        for c in range(planes):
            z_buf[c] = x[:, c * LANES:(c + 1) * LANES]
        return jnp.concatenate(
            [jnp.concatenate([z_buf[c, pl.ds(r, rows, stride=dil), :] for r in range(dil)], axis=0)
             for c in range(planes)], axis=1)

    cos128 = residue_major(cos_ref[...])
    sin128 = residue_major(sin_ref[...])
    cos = jnp.concatenate([cos128] * reps, axis=1)
    sin = jnp.concatenate([sin128] * reps, axis=1)
    hn = residue_major(_rms(h_ref[...], g_ref[...])).astype(jnp.bfloat16)
    for part in range(3):
        z = jnp.dot(hn, w_ref[:, part * D_MODEL:(part + 1) * D_MODEL], preferred_element_type=jnp.float32)
        if part < 2:
            z1 = z[:, :half]
            z2 = z[:, half:]
            z = jnp.concatenate([z1 * cos - z2 * sin, z2 * cos + z1 * sin], axis=1)
        if part == 0:
            z = z * (HEAD_DIM ** -0.5 * LOG2_E)
        z = z.astype(o_ref.dtype)
        for r in range(dil):
            o_ref[0, r, :, part * D_MODEL:(part + 1) * D_MODEL] = z[r * rows:(r + 1) * rows]


def qkv_project(h, batch, gain, w_group, cos, sin, dil):
    n = h.shape[0]
    seq = n // batch
    t = PROJ_TILE
    ns = seq // t
    row = lambda width: pl.BlockSpec((t, width), lambda b, s: (b * ns + s, 0))
    return pl.pallas_call(
        functools.partial(_qkv_kernel, dil=dil),
        grid=(batch, ns),
        in_specs=[row(D_MODEL),
                  pl.BlockSpec((1, D_MODEL), lambda b, s: (0, 0)),
                  pl.BlockSpec((D_MODEL, 3 * D_MODEL), lambda b, s: (0, 0)),
                  row(LANES), row(LANES)],
        out_specs=pl.BlockSpec((1, dil, t // dil, 3 * D_MODEL), lambda b, s: (b, 0, s, 0)),
        out_shape=jax.ShapeDtypeStruct((batch, dil, seq // dil, 3 * D_MODEL), jnp.bfloat16),
        scratch_shapes=[pltpu.VMEM((D_MODEL // LANES, t, LANES), jnp.float32)],
        compiler_params=_cparams("parallel", "parallel"),
        name=f"qkv_project_d{dil}",
    )(h, gain.reshape(1, D_MODEL), w_group, cos, sin)


def _attention_kernel(q_ref, k_ref, v_ref, o_ref, l_ref, k_win, v_win, bias_buf):
    blk = ATT_BLK
    n = pl.program_id(2)
    n_sub = q_ref.shape[2] // blk
    heads_per_slab = LANES // HALF_DIM
    n_slabs = N_HEADS // heads_per_slab
    rows = heads_per_slab * blk

    @pl.when(n == 0)
    def _():
        k_win[0:blk, :] = jnp.zeros((blk, D_MODEL), k_win.dtype)
        v_win[0:blk, :] = jnp.zeros((blk, D_MODEL), v_win.dtype)

    k_win[blk:(n_sub + 1) * blk, :] = k_ref[0, 0]
    v_win[blk:(n_sub + 1) * blk, :] = v_ref[0, 0]

    @pl.when((pl.program_id(0) == 0) & (pl.program_id(1) == 0) & (n == 0))
    def _():
        qi = lax.broadcasted_iota(jnp.int32, (rows, 2 * blk), 0) % blk
        ki = lax.broadcasted_iota(jnp.int32, (rows, 2 * blk), 1)
        own_ok = (ki >= blk) & (ki - blk <= qi)
        back_ok = (ki < blk) & (ki >= qi)
        bias_buf[0] = jnp.where(own_ok, 0.0, NEG_INF).astype(jnp.float32)
        bias_buf[1] = jnp.where(own_ok | back_ok, 0.0, NEG_INF).astype(jnp.float32)

    first_plane = jnp.minimum(n, 1)

    half = D_MODEL // 2
    lane2 = lax.broadcasted_iota(jnp.int32, (blk, 2 * LANES), 1)
    lane1 = lax.broadcasted_iota(jnp.int32, (blk, LANES), 1)

    def scores(unit):
        sub, m = unit
        q_rows = slice(sub * blk, (sub + 1) * blk)
        k_rows = slice(sub * blk, (sub + 2) * blk)
        q_slab = jnp.concatenate([q_ref[0, 0, q_rows, m * LANES:(m + 1) * LANES],
                                  q_ref[0, 0, q_rows, half + m * LANES:half + (m + 1) * LANES]], axis=1)
        k_slab = jnp.concatenate([k_win[k_rows, m * LANES:(m + 1) * LANES],
                                  k_win[k_rows, half + m * LANES:half + (m + 1) * LANES]], axis=1)
        zero = jnp.zeros_like(q_slab)
        q_heads = jnp.concatenate(
            [jnp.where(((lane2 % LANES) // HALF_DIM) == a, q_slab, zero) for a in range(heads_per_slab)], axis=0)
        bias = bias_buf[first_plane] if sub == 0 else bias_buf[1]
        return lax.dot_general(q_heads, k_slab, (((1,), (1,)), ((), ())),
                               preferred_element_type=jnp.float32) + bias

    units = [(sub, m) for sub in range(n_sub) for m in range(n_slabs)]
    lse_out = [jnp.zeros((blk, LANES), jnp.float32) for _ in range(n_sub)]
    s_next = scores(units[0])
    for idx, (sub, m) in enumerate(units):
        s = s_next
        if idx + 1 < len(units):
            s_next = scores(units[idx + 1])
        mx = jnp.max(s, axis=-1, keepdims=True)
        p = jnp.exp2(s - mx)
        den = jnp.sum(p, axis=-1, keepdims=True)
        pb = p.astype(jnp.bfloat16)
        inv = 1.0 / den
        lse = (mx + jnp.log2(den)) * LN_2
        for a in range(heads_per_slab):
            lse_out[sub] = jnp.where(lane1 == m * heads_per_slab + a, lse[a * blk:(a + 1) * blk], lse_out[sub])
        q_rows = slice(sub * blk, (sub + 1) * blk)
        k_rows = slice(sub * blk, (sub + 2) * blk)
        for pair in range(heads_per_slab // 2):
            col = (m * heads_per_slab + pair * 2) * HEAD_DIM
            r0 = pair * 2 * blk
            o2 = jnp.dot(pb[r0:r0 + 2 * blk], v_win[k_rows, col:col + LANES],
                         preferred_element_type=jnp.float32) * inv[r0:r0 + 2 * blk]
            o_ref[0, 0, q_rows, col:col + LANES] = jnp.where(
                lane1 < HEAD_DIM, o2[:blk], o2[blk:]).astype(o_ref.dtype)
    for sub in range(n_sub):
        l_ref[0, 0, sub * blk:(sub + 1) * blk, :] = lse_out[sub]
    k_win[0:blk, :] = k_win[n_sub * blk:(n_sub + 1) * blk, :]
    v_win[0:blk, :] = v_win[n_sub * blk:(n_sub + 1) * blk, :]


def dilated_attention_group(qkv_g, dil):
    batch, _, length, _ = qkv_g.shape
    blk = ATT_BLK
    rows = min(ATT_BLOCKS_PER_STEP * blk, length)
    nstep = length // rows

    def part_spec(part):
        return pl.BlockSpec((1, 1, rows, D_MODEL), lambda b, r, i: (b, r, i, part))

    return pl.pallas_call(
        _attention_kernel,
        grid=(batch, dil, nstep),
        in_specs=[part_spec(0), part_spec(1), part_spec(2)],
        out_specs=[pl.BlockSpec((1, 1, rows, D_MODEL), lambda b, r, i: (b, r, i, 0)),
                   pl.BlockSpec((1, 1, rows, LANES), lambda b, r, i: (b, r, i, 0))],
        out_shape=[jax.ShapeDtypeStruct((batch, dil, length, D_MODEL), jnp.bfloat16),
                   jax.ShapeDtypeStruct((batch, dil, length, LANES), jnp.float32)],
        scratch_shapes=[pltpu.VMEM((rows + blk, D_MODEL), jnp.bfloat16),
                        pltpu.VMEM((rows + blk, D_MODEL), jnp.bfloat16),
                        pltpu.VMEM((2, (LANES // HALF_DIM) * blk, 2 * blk), jnp.float32)],
        compiler_params=_cparams("arbitrary", "arbitrary", "arbitrary"),
        name=f"dilated_attention_d{dil}",
    )(qkv_g, qkv_g, qkv_g)


def _merge_proj_kernel(*refs, dils):
    ng = len(dils)
    h_ref = refs[0]
    o_refs = refs[1:1 + ng]
    l_refs = refs[1 + ng:1 + 2 * ng]
    e_ref, w_ref, out_ref, o_buf, l_buf = refs[1 + 2 * ng:]
    t = h_ref.shape[0]
    lses = []
    for g, dil in enumerate(dils):
        if dil == 1:
            lses.append(l_refs[g][0, 0])
        else:
            for r in range(dil):
                l_buf[pl.ds(r, t // dil, stride=dil), :] = l_refs[g][0, r]
            lses.append(l_buf[...])
    top = functools.reduce(jnp.maximum, lses)
    ex = [jnp.exp(l - top) for l in lses]
    inv = 1.0 / functools.reduce(jnp.add, ex)
    merged = jnp.zeros((t, D_MODEL), jnp.float32)
    for g, dil in enumerate(dils):
        wg = ex[g] * inv
        hi = wg.astype(jnp.bfloat16)
        lo = (wg - hi.astype(jnp.float32)).astype(jnp.bfloat16)
        wexp = jnp.dot(jnp.concatenate([hi, lo], axis=1), e_ref[...], preferred_element_type=jnp.float32)
        if dil == 1:
            og = o_refs[g][0, 0].astype(jnp.float32)
        else:
            for r in range(dil):
                part = o_refs[g][0, r].astype(jnp.float32)
                for c in range(D_MODEL // LANES):
                    o_buf[c, pl.ds(r, t // dil, stride=dil), :] = part[:, c * LANES:(c + 1) * LANES]
            og = jnp.concatenate([o_buf[c] for c in range(D_MODEL // LANES)], axis=1)
        merged = merged + wexp * og
    out_ref[...] = h_ref[...] + jnp.dot(merged.astype(jnp.bfloat16), w_ref[...],
                                        preferred_element_type=jnp.float32)


def merge_proj_residual(h, batch, outs, lses, w_o):
    n = h.shape[0]
    seq = n // batch
    t = PROJ_TILE
    ns = seq // t
    dils = tuple(o.shape[1] for o in outs)
    row = pl.BlockSpec((t, D_MODEL), lambda b, s: (b * ns + s, 0))
    expand = np.zeros((2 * LANES, D_MODEL), np.float32)
    for hd in range(N_HEADS):
        expand[hd, hd * HEAD_DIM:(hd + 1) * HEAD_DIM] = 1.0
        expand[LANES + hd, hd * HEAD_DIM:(hd + 1) * HEAD_DIM] = 1.0
    in_specs = [row]
    in_specs += [pl.BlockSpec((1, d, t // d, D_MODEL), lambda b, s: (b, 0, s, 0)) for d in dils]
    in_specs += [pl.BlockSpec((1, d, t // d, LANES), lambda b, s: (b, 0, s, 0)) for d in dils]
    in_specs += [pl.BlockSpec((2 * LANES, D_MODEL), lambda b, s: (0, 0)),
                 pl.BlockSpec((D_MODEL, D_MODEL), lambda b, s: (0, 0))]
    return pl.pallas_call(
        functools.partial(_merge_proj_kernel, dils=dils),
        grid=(batch, ns), in_specs=in_specs, out_specs=row,
        out_shape=jax.ShapeDtypeStruct((n, D_MODEL), jnp.float32),
        scratch_shapes=[pltpu.VMEM((D_MODEL // LANES, t, LANES), jnp.float32),
                        pltpu.VMEM((t, LANES), jnp.float32)],
        compiler_params=_cparams("parallel", "parallel"), name="attn_merge_out_proj",
    )(h, *outs, *lses, jnp.asarray(expand, jnp.bfloat16), w_o)


def _router_kernel(h_ref, g_ref, w_ref, b_ref, tri_ref, hn_ref, route_ref, count_ref, run_ref):
    @pl.when(pl.program_id(0) == 0)
    def _():
        run_ref[...] = jnp.zeros_like(run_ref)

    hn = _rms(h_ref[...], g_ref[...])
    hn_hi = hn.astype(jnp.bfloat16)
    hn_lo = (hn - hn_hi.astype(jnp.float32)).astype(jnp.bfloat16)
    both = jnp.dot(hn_hi, w_ref[...], preferred_element_type=jnp.float32)
    logits = (both[:, :LANES] + both[:, LANES:]
              + jnp.dot(hn_lo, w_ref[:, :LANES], preferred_element_type=jnp.float32) + b_ref[...])
    lane = lax.broadcasted_iota(jnp.int32, logits.shape, 1)
    big = jnp.int32(LANES)

    def first_argmax(vals, vmax):
        return jnp.min(jnp.where(vals == vmax, lane, big), axis=-1, keepdims=True)

    coarse = jnp.where(lane < N_EXPERT_GROUPS, logits, NEG_INF)
    cmax = jnp.max(coarse, axis=-1, keepdims=True)
    p_top = 1.0 / jnp.sum(jnp.exp(coarse - cmax), axis=-1, keepdims=True)
    g_top = first_argmax(coarse, cmax)
    lo = N_EXPERT_GROUPS + g_top * EXPERTS_PER_GROUP
    fine = jnp.where((lane >= lo) & (lane < lo + EXPERTS_PER_GROUP), logits, NEG_INF)
    v1 = jnp.max(fine, axis=-1, keepdims=True)
    i1 = first_argmax(fine, v1)
    fine2 = jnp.where(lane == i1, NEG_INF, fine)
    v2 = jnp.max(fine2, axis=-1, keepdims=True)
    i2 = first_argmax(fine2, v2)
    e2 = jnp.exp(v2 - v1)
    w1 = 1.0 / (1.0 + e2)
    w2 = e2 / (1.0 + e2)
    first_lo = i1 < i2
    a = jnp.where(first_lo, i1, i2) - lo
    b = jnp.where(first_lo, i2, i1) - lo
    pair = jnp.where(a == 0, b - 1, jnp.where(a == 1, jnp.where(b == 3, 3, 4), 5))
    cls = g_top * len(PAIR_SLOTS) + pair
    gate_lo = p_top * jnp.where(first_lo, w1, w2)
    gate_hi = p_top * jnp.where(first_lo, w2, w1)
    swapped = pair == 5
    gate_a = jnp.where(swapped, gate_hi, gate_lo)
    gate_b = jnp.where(swapped, gate_lo, gate_hi)
    gates = jnp.where(lane == 0, gate_a, jnp.where(lane == 1, gate_b, 0.0))
    _store_row_tiles(hn_ref, _pack_rows(hn), pltpu.bitcast(gates, jnp.uint32))

    sel = lane == cls
    onehot = jnp.where(sel, 1.0, 0.0)
    sub = tri_ref.shape[0]
    run = run_ref[...]
    befores = []
    for r0 in range(0, onehot.shape[0], sub):
        oh = onehot[r0:r0 + sub]
        befores.append(jnp.dot(tri_ref[...], oh.astype(jnp.bfloat16), preferred_element_type=jnp.float32) + run)
        run = run + jnp.sum(oh, axis=0, keepdims=True)
    before = jnp.concatenate(befores, axis=0)
    rank = jnp.sum(jnp.where(sel, before, 0.0), axis=-1, keepdims=True)
    run_ref[...] = run
    count_ref[...] = run

    route = jnp.where(lane == R_CLASS, cls.astype(jnp.float32), 0.0)
    route = jnp.where(lane == R_RANK, rank, route)
    route_ref[...] = route


def moe_router(h, gain, w_route, b_route):
    n = h.shape[0]
    tr = ROW_TILE
    sub = RANK_BLOCK
    tri = jnp.asarray(np.tril(np.ones((sub, sub), np.float32), -1), jnp.bfloat16)
    w_hi = w_route.astype(jnp.bfloat16)
    w_lo = (w_route - w_hi.astype(jnp.float32)).astype(jnp.bfloat16)
    w_split = jnp.concatenate([w_hi, w_lo], axis=1)
    return pl.pallas_call(
        _router_kernel, grid=(n // tr,),
        in_specs=[pl.BlockSpec((tr, D_MODEL), lambda i: (i, 0)),
                  pl.BlockSpec((1, D_MODEL), lambda i: (0, 0)),
                  pl.BlockSpec((D_MODEL, 2 * LANES), lambda i: (0, 0)),
                  pl.BlockSpec((1, LANES), lambda i: (0, 0)),
                  pl.BlockSpec((sub, sub), lambda i: (0, 0))],
        out_specs=[pl.BlockSpec((tr * ROW_SUBLANES, LANES), lambda i: (i, 0)),
                   pl.BlockSpec((tr, LANES), lambda i: (i, 0)),
                   pl.BlockSpec((1, LANES), lambda i: (0, 0))],
        out_shape=[jax.ShapeDtypeStruct((n * ROW_SUBLANES, LANES), jnp.uint32),
                   jax.ShapeDtypeStruct((n, LANES), jnp.float32),
                   jax.ShapeDtypeStruct((1, LANES), jnp.float32)],
        scratch_shapes=[pltpu.VMEM((1, LANES), jnp.float32)],
        compiler_params=_cparams("arbitrary"), name="moe_router",
    )(h, gain.reshape(1, D_MODEL), w_split, b_route, tri)


def _dispatch_kernel(dest_ref, x_ref, init_ref, o_ref, sem):
    del init_ref
    t = x_ref.shape[0] // ROW_SUBLANES

    def row_copy(j):
        dst = pl.multiple_of(dest_ref[0, 0, j], ROW_SUBLANES)
        return pltpu.make_async_copy(x_ref.at[pl.ds(j * ROW_SUBLANES, X_SUBLANES)],
                                     o_ref.at[pl.ds(dst, X_SUBLANES)], sem)

    for j in range(t):
        row_copy(j).start(priority=j % 2)

    def wait(j, c):
        row_copy(0).wait()
        return c

    lax.fori_loop(0, t, wait, 0, unroll=DMA_UNROLL)


def moe_dispatch(rows, dest8, n_rows):
    n = rows.shape[0] // ROW_SUBLANES
    t = DISPATCH_TILE
    return pl.pallas_call(
        _dispatch_kernel, grid=(n // t,),
        in_specs=[pl.BlockSpec((1, 1, t), lambda i: (i, 0, 0), memory_space=pltpu.SMEM),
                  pl.BlockSpec((t * ROW_SUBLANES, LANES), lambda i: (i, 0)),
                  pl.BlockSpec(memory_space=pl.ANY)],
        out_specs=pl.BlockSpec(memory_space=pl.ANY),
        out_shape=jax.ShapeDtypeStruct((n_rows * ROW_SUBLANES, LANES), jnp.uint32),
        scratch_shapes=[pltpu.SemaphoreType.DMA(())],
        input_output_aliases={2: 0},
        compiler_params=_cparams("arbitrary"), name="moe_dispatch",
    )(dest8.reshape(n // t, 1, t), rows, jnp.zeros((n_rows * ROW_SUBLANES, LANES), jnp.uint32))


def _expert_kernel(ea_ref, eb_ref, na_ref, nb_ref, tr_ref, tv_ref, x_ref,
                   wga_ref, wua_ref, wda_ref, wgb_ref, wub_ref, wdb_ref, o_ref,
                   wga_bf, wua_bf, wda_bf, wgb_bf, wub_bf, wdb_bf):
    del ea_ref, eb_ref, tr_ref
    i = pl.program_id(0)

    @pl.when(na_ref[i] > 0)
    def _():
        wga_bf[...] = wga_ref[0, 0].astype(jnp.bfloat16)
        wua_bf[...] = wua_ref[0, 0].astype(jnp.bfloat16)
        wda_bf[...] = wda_ref[0, 0].astype(jnp.bfloat16)

    @pl.when(nb_ref[i] > 0)
    def _():
        wgb_bf[...] = wgb_ref[0, 0].astype(jnp.bfloat16)
        wub_bf[...] = wub_ref[0, 0].astype(jnp.bfloat16)
        wdb_bf[...] = wdb_ref[0, 0].astype(jnp.bfloat16)

    @pl.when(tv_ref[i] > 0)
    def _():
        tm = o_ref.shape[0] // ROW_SUBLANES
        x = _unpack_rows(_load_row_tiles(x_ref, tm)).astype(jnp.bfloat16)
        gates = pltpu.bitcast(x_ref[pl.ds(DATA_SUBLANES, tm, stride=ROW_SUBLANES), :], jnp.float32)

        def mlp(wg, wu, wd):
            g = jnp.dot(x, wg[...], preferred_element_type=jnp.float32)
            u = jnp.dot(x, wu[...], preferred_element_type=jnp.float32)
            hmid = (g * jax.nn.sigmoid(g) * u).astype(jnp.bfloat16)
            return jnp.dot(hmid, wd[...], preferred_element_type=jnp.float32)

        _store_row_tiles(o_ref, _pack_rows(gates[:, 0:1] * mlp(wga_bf, wua_bf, wda_bf)
                                           + gates[:, 1:2] * mlp(wgb_bf, wub_bf, wdb_bf)))

    @pl.when(tv_ref[i] == 0)
    def _():
        o_ref[...] = jnp.zeros_like(o_ref)


def expert_mlp(x_sorted, tables, layer, w_gate, w_up, w_down):
    p = x_sorted.shape[0] // ROW_SUBLANES
    tm = EXPERT_TILE
    row_block = (tm * ROW_SUBLANES, LANES)

    def weight_spec(shape, slot):
        return pl.BlockSpec((1, 1) + shape, lambda i, ea, eb, na, nb, tr, tv: (layer, (ea, eb)[slot][i], 0, 0))

    up_shape, down_shape = (D_MODEL, EXPERT_FF), (EXPERT_FF, D_MODEL)
    grid_spec = pltpu.PrefetchScalarGridSpec(
        num_scalar_prefetch=6,
        grid=(p // tm,),
        in_specs=[pl.BlockSpec(row_block, lambda i, ea, eb, na, nb, tr, tv: (tr[i], 0)),
                  weight_spec(up_shape, 0), weight_spec(up_shape, 0), weight_spec(down_shape, 0),
                  weight_spec(up_shape, 1), weight_spec(up_shape, 1), weight_spec(down_shape, 1)],
        out_specs=pl.BlockSpec(row_block, lambda i, ea, eb, na, nb, tr, tv: (i, 0)),
        scratch_shapes=[pltpu.VMEM(up_shape, jnp.bfloat16), pltpu.VMEM(up_shape, jnp.bfloat16),
                        pltpu.VMEM(down_shape, jnp.bfloat16)] * 2,
    )
    return pl.pallas_call(
        _expert_kernel, grid_spec=grid_spec,
        out_shape=jax.ShapeDtypeStruct((p * ROW_SUBLANES, LANES), jnp.uint32),
        compiler_params=_cparams("arbitrary"), name="expert_mlp",
    )(*tables, x_sorted, w_gate, w_up, w_down, w_gate, w_up, w_down)


def _combine_kernel(dcur_ref, dnext_ref, h_ref, gain_ref, y_ref, o_ref, y_buf, sem, *, final_norm):
    i = pl.program_id(0)
    last = pl.num_programs(0) - 1
    t = h_ref.shape[0]
    slot = i % 2

    def row_copy(dest_ref, s, j):
        src = pl.multiple_of(dest_ref[0, 0, j], ROW_SUBLANES)
        return pltpu.make_async_copy(y_ref.at[pl.ds(src, DATA_SUBLANES)],
                                     y_buf.at[s, pl.ds(j * ROW_SUBLANES, DATA_SUBLANES)], sem.at[s])

    def start_all(dest_ref, s):
        for j in range(t):
            row_copy(dest_ref, s, j).start(priority=j % 2)

    @pl.when(i == 0)
    def _():
        start_all(dcur_ref, 0)

    for s in range(2):
        @pl.when((i < last) & (slot == 1 - s))
        def _():
            start_all(dnext_ref, s)

    lax.fori_loop(0, t, lambda j, c: (row_copy(dcur_ref, slot, 0).wait(), c)[1], 0, unroll=DMA_UNROLL)
    out = h_ref[...] + _unpack_rows(_load_row_tiles(y_buf.at[slot], t))
    if final_norm:
        out = _rms(out, gain_ref[...])
    o_ref[...] = out


def moe_combine(h, y, dest8, final_gain):
    n = h.shape[0]
    t = COMBINE_TILE
    nt = n // t
    dest3 = dest8.reshape(nt, 1, t)
    final_norm = final_gain is not None
    gain = final_gain if final_norm else jnp.ones((D_MODEL,), jnp.float32)
    return pl.pallas_call(
        functools.partial(_combine_kernel, final_norm=final_norm), grid=(nt,),
        in_specs=[pl.BlockSpec((1, 1, t), lambda i: (i, 0, 0), memory_space=pltpu.SMEM),
                  pl.BlockSpec((1, 1, t), lambda i: (jnp.minimum(i + 1, nt - 1), 0, 0),
                               memory_space=pltpu.SMEM),
                  pl.BlockSpec((t, D_MODEL), lambda i: (i, 0)),
                  pl.BlockSpec((1, D_MODEL), lambda i: (0, 0)),
                  pl.BlockSpec(memory_space=pl.ANY)],
        out_specs=pl.BlockSpec((t, D_MODEL), lambda i: (i, 0)),
        out_shape=jax.ShapeDtypeStruct((n, D_MODEL), jnp.float32),
        scratch_shapes=[pltpu.VMEM((2, t * ROW_SUBLANES, LANES), jnp.uint32), pltpu.SemaphoreType.DMA((2,))],
        compiler_params=_cparams("arbitrary"), name="moe_combine",
    )(dest3, dest3, h, gain.reshape(1, D_MODEL), y)


def hierarchical_moe(h, gain, w_route, b_route, layer, w_gate, w_up, w_down, final_gain=None):
    n = h.shape[0]
    tm = EXPERT_TILE
    rows, route, counts = moe_router(h, gain, w_route, b_route)
    counts = counts[0, :N_CLASSES].astype(jnp.int32)
    padded = ((counts + tm - 1) // tm) * tm
    seg_end = jnp.cumsum(padded)
    seg_start = seg_end - padded
    cls = route[:, R_CLASS].astype(jnp.int32)
    rank = route[:, R_RANK].astype(jnp.int32)
    class_ids = jnp.arange(N_CLASSES, dtype=jnp.int32)
    dest = jnp.sum(jnp.where(cls[:, None] == class_ids[None, :], seg_start[None, :], 0), axis=-1) + rank
    dest8 = dest * ROW_SUBLANES
    n_rows = n + N_CLASSES * tm
    n_tiles = n_rows // tm
    used = seg_end[-1] // tm
    tile_id = jnp.arange(n_tiles, dtype=jnp.int32)
    tile_row = jnp.minimum(tile_id, used - 1).astype(jnp.int32)
    tile_class = jnp.minimum(jnp.sum((seg_end[None, :] <= (tile_row * tm)[:, None]).astype(jnp.int32), axis=1),
                             N_CLASSES - 1)
    tile_valid = (tile_id < used).astype(jnp.int32)
    slots = np.asarray([[g * EXPERTS_PER_GROUP + s for s in pair]
                        for g in range(N_EXPERT_GROUPS) for pair in PAIR_SLOTS], np.int32)
    tile_onehot = (tile_class[:, None] == class_ids[None, :]).astype(jnp.int32)
    expert_a = jnp.sum(tile_onehot * slots[None, :, 0], axis=1).astype(jnp.int32)
    expert_b = jnp.sum(tile_onehot * slots[None, :, 1], axis=1).astype(jnp.int32)

    def changed(e):
        return jnp.concatenate([jnp.ones((1,), jnp.int32), (e[1:] != e[:-1]).astype(jnp.int32)])

    tables = (expert_a, expert_b, changed(expert_a), changed(expert_b), tile_row, tile_valid)
    x_sorted = moe_dispatch(rows, dest8, n_rows)
    y = expert_mlp(x_sorted, tables, layer, w_gate, w_up, w_down)
    return moe_combine(h, y, dest8, final_gain)


def _permute_qk_columns(w_qkv):
    w = w_qkv.astype(jnp.bfloat16).reshape(D_MODEL, N_ATTN_GROUPS, 3, N_HEADS, 2, HALF_DIM)
    qk = jnp.transpose(w[:, :, 0:2], (0, 1, 2, 4, 3, 5)).reshape(D_MODEL, N_ATTN_GROUPS, 2, D_MODEL)
    v = w[:, :, 2:3].reshape(D_MODEL, N_ATTN_GROUPS, 1, D_MODEL)
    return jnp.concatenate([qk, v], axis=2).reshape(D_MODEL, QKV_WIDTH)


def kernel(x, positions, norm_mix, norm_ffn, norm_final, conv_w_in, conv_w, conv_w_out,
           attn_w_qkv, attn_w_o, w_coarse, b_coarse, w_fine, b_fine, w_gate, w_up, w_down):
    batch, seq, d = x.shape
    assert d == D_MODEL
    n = batch * seq
    depth = norm_mix.shape[0]
    bf = jnp.bfloat16
    h = x.reshape(n, d)
    cos, sin = rope_tables(positions.reshape(n))
    for i in range(depth):
        j = i // 2
        if i % 2 == 0:
            h = conv_mixer(h, batch, norm_mix[i], conv_w_in[j].astype(bf), conv_w[j], conv_w_out[j].astype(bf))
        else:
            w_qkv = _permute_qk_columns(attn_w_qkv[j])
            outs, lses = [], []
            for g, (window, dil) in enumerate(DILATED_GROUPS):
                assert window // dil == ATT_BLK
                w_group = w_qkv[:, g * 3 * D_MODEL:(g + 1) * 3 * D_MODEL]
                o, lse = dilated_attention_group(
                    qkv_project(h, batch, norm_mix[i], w_group, cos, sin, dil), dil)
                outs.append(o)
                lses.append(lse)
            h = merge_proj_residual(h, batch, outs, lses, attn_w_o[j].astype(bf))
        w_route = jnp.concatenate(
            [w_coarse[i], jnp.transpose(w_fine[i], (1, 0, 2)).reshape(d, N_EXPERTS),
             jnp.zeros((d, LANES - N_EXPERT_GROUPS - N_EXPERTS), jnp.float32)], axis=1)
        b_route = jnp.concatenate(
            [b_coarse[i], b_fine[i].reshape(-1),
             jnp.zeros((LANES - N_EXPERT_GROUPS - N_EXPERTS,), jnp.float32)]).reshape(1, LANES)
        h = hierarchical_moe(h, norm_ffn[i], w_route, b_route,
                             i, w_gate, w_up, w_down,
                             norm_final if i == depth - 1 else None)
    return h.reshape(batch, seq, d)
```

```python
import functools
import math

import numpy as np
import jax
import jax.numpy as jnp
from jax import lax
from jax.experimental import pallas as pl
from jax.experimental.pallas import tpu as pltpu

D_MODEL = 1024
NORM_EPS = 1e-6
CONV_WIDTH = 3
HEAD_DIM = 64
HALF_DIM = HEAD_DIM // 2
N_HEADS = D_MODEL // HEAD_DIM
DILATED_GROUPS = ((128, 1), (512, 4), (2048, 16))
N_ATTN_GROUPS = len(DILATED_GROUPS)
QKV_WIDTH = N_ATTN_GROUPS * 3 * D_MODEL
ROPE_THETA = 10000.0
N_EXPERT_GROUPS = 4
EXPERTS_PER_GROUP = 4
N_EXPERTS = N_EXPERT_GROUPS * EXPERTS_PER_GROUP
EXPERT_FF = D_MODEL // 2
PACKED = D_MODEL // 2

LANES = 128
ATT_BLK = 128
ATT_BLOCKS_PER_STEP = 8
VMEM_LIMIT_BYTES = 56 * 1024 * 1024
ROW_TILE = 1024
PROJ_TILE = 512
CONV_TILE = 512
EXPERT_TILE = 256
DISPATCH_TILE = 512
COMBINE_TILE = 512
RANK_BLOCK = 256
DMA_UNROLL = 8
NEG_INF = float("-inf")
LOG2_E = math.log2(math.e)
LN_2 = math.log(2.0)
R_CLASS, R_RANK = range(2)
PAIR_SLOTS = ((0, 1), (0, 2), (0, 3), (1, 3), (1, 2), (3, 2))
N_CLASSES = N_EXPERT_GROUPS * len(PAIR_SLOTS)
ROW_WORDS = PACKED + LANES


def _cparams(*sem):
    return pltpu.CompilerParams(dimension_semantics=sem, vmem_limit_bytes=VMEM_LIMIT_BYTES)


def _rms(x, g):
    return x * lax.rsqrt(jnp.mean(x * x, axis=-1, keepdims=True) + NORM_EPS) * g


def _pack_rows(x):
    lo = pltpu.bitcast(x[:, :PACKED].astype(jnp.bfloat16).astype(jnp.float32), jnp.uint32)
    hi = pltpu.bitcast(x[:, PACKED:].astype(jnp.bfloat16).astype(jnp.float32), jnp.uint32)
    return (lo >> 16) | (hi & jnp.uint32(0xFFFF0000))


def _unpack_rows(u):
    lo = pltpu.bitcast(u << 16, jnp.float32)
    hi = pltpu.bitcast(u & jnp.uint32(0xFFFF0000), jnp.float32)
    return jnp.concatenate([lo, hi], axis=1)


def _rope_kernel(pos_ref, freq_ref, cos_ref, sin_ref):
    ang = pos_ref[...].astype(jnp.float32) * freq_ref[...]
    cos_ref[...] = jnp.cos(ang)
    sin_ref[...] = jnp.sin(ang)


def rope_tables(positions):
    n = positions.shape[0]
    inv_freq = (ROPE_THETA ** (-np.arange(0, HEAD_DIM, 2, dtype=np.float32) / HEAD_DIM)).astype(np.float32)
    freq = jnp.asarray(np.tile(inv_freq, LANES // HALF_DIM)[None, :])
    tr = ROW_TILE
    return pl.pallas_call(
        _rope_kernel,
        grid=(n // tr,),
        in_specs=[pl.BlockSpec((tr, 1), lambda i: (i, 0)),
                  pl.BlockSpec((1, LANES), lambda i: (0, 0))],
        out_specs=[pl.BlockSpec((tr, LANES), lambda i: (i, 0)),
                   pl.BlockSpec((tr, LANES), lambda i: (i, 0))],
        out_shape=[jax.ShapeDtypeStruct((n, LANES), jnp.float32)] * 2,
        compiler_params=_cparams("parallel"),
        name="rope_tables",
    )(positions.reshape(n, 1), freq)


def _conv_mixer_kernel(h_ref, g_ref, win_ref, cw_ref, wout_ref, o_ref, hn_buf, u_buf, v_buf):
    t = h_ref.shape[0]

    @pl.when(pl.program_id(1) == 0)
    def _():
        u_buf[0:8, :] = jnp.zeros((8, D_MODEL), jnp.float32)

    x = h_ref[...]
    hn_buf[...] = _rms(x, g_ref[...]).astype(jnp.bfloat16)
    cw = cw_ref[...]
    chunk = 512
    for c in range(0, D_MODEL, chunk):
        hn = hn_buf[...]
        gate_c = jnp.dot(hn, win_ref[:, D_MODEL + c:D_MODEL + c + chunk], preferred_element_type=jnp.float32)
        hh = jnp.dot(hn, win_ref[:, 2 * D_MODEL + c:2 * D_MODEL + c + chunk], preferred_element_type=jnp.float32)
        u_buf[8:8 + t, c:c + chunk] = gate_c * hh
        conv = (cw[0:1, c:c + chunk] * u_buf[6:6 + t, c:c + chunk]
                + cw[1:2, c:c + chunk] * u_buf[7:7 + t, c:c + chunk]
                + cw[2:3, c:c + chunk] * u_buf[8:8 + t, c:c + chunk])
        gate_b = jnp.dot(hn, win_ref[:, c:c + chunk], preferred_element_type=jnp.float32)
        v_buf[:, c:c + chunk] = (gate_b * conv).astype(jnp.bfloat16)
    u_buf[0:8, :] = u_buf[t:t + 8, :]
    o_ref[...] = x + jnp.dot(v_buf[...], wout_ref[...], preferred_element_type=jnp.float32)


def conv_mixer(h, batch, gain, w_in, conv_w, w_out):
    n = h.shape[0]
    seq = n // batch
    t = CONV_TILE
    ns = seq // t
    row = pl.BlockSpec((t, D_MODEL), lambda b, s: (b * ns + s, 0))
    return pl.pallas_call(
        _conv_mixer_kernel,
        grid=(batch, ns),
        in_specs=[row,
                  pl.BlockSpec((1, D_MODEL), lambda b, s: (0, 0)),
                  pl.BlockSpec((D_MODEL, 3 * D_MODEL), lambda b, s: (0, 0)),
                  pl.BlockSpec((CONV_WIDTH, D_MODEL), lambda b, s: (0, 0)),
                  pl.BlockSpec((D_MODEL, D_MODEL), lambda b, s: (0, 0))],
        out_specs=row,
        out_shape=jax.ShapeDtypeStruct((n, D_MODEL), jnp.float32),
        scratch_shapes=[pltpu.VMEM((t, D_MODEL), jnp.bfloat16),
                        pltpu.VMEM((t + 8, D_MODEL), jnp.float32),
                        pltpu.VMEM((t, D_MODEL), jnp.bfloat16)],
        compiler_params=_cparams("arbitrary", "arbitrary"),
        name="conv_mixer",
    )(h, gain.reshape(1, D_MODEL), w_in, conv_w, w_out)


def _qkv_kernel(h_ref, g_ref, w_ref, cos_ref, sin_ref, o_ref, z_buf, *, dil):
    t = h_ref.shape[0]
    half = D_MODEL // 2
    reps = half // LANES
    rows = t // dil

    def residue_major(x):
        if dil == 1:
            return x
        planes = x.shape[1] // LANES
        for c in range(planes):
            z_buf[c] = x[:, c * LANES:(c + 1) * LANES]
        return jnp.concatenate(
            [jnp.concatenate([z_buf[c, pl.ds(r, rows, stride=dil), :] for r in range(dil)], axis=0)
             for c in range(planes)], axis=1)

    cos128 = residue_major(cos_ref[...])
    sin128 = residue_major(sin_ref[...])
    cos = jnp.concatenate([cos128] * reps, axis=1)
    sin = jnp.concatenate([sin128] * reps, axis=1)
    hn = residue_major(_rms(h_ref[...], g_ref[...])).astype(jnp.bfloat16)
    for part in range(3):
        z = jnp.dot(hn, w_ref[:, part * D_MODEL:(part + 1) * D_MODEL], preferred_element_type=jnp.float32)
        if part < 2:
            z1 = z[:, :half]
            z2 = z[:, half:]
            z = jnp.concatenate([z1 * cos - z2 * sin, z2 * cos + z1 * sin], axis=1)
        if part == 0:
            z = z * (HEAD_DIM ** -0.5 * LOG2_E)
        z = z.astype(o_ref.dtype)
        for r in range(dil):
            o_ref[0, r, :, part * D_MODEL:(part + 1) * D_MODEL] = z[r * rows:(r + 1) * rows]


def qkv_project(h, batch, gain, w_group, cos, sin, dil):
    n = h.shape[0]
    seq = n // batch
    t = PROJ_TILE
    ns = seq // t
    row = lambda width: pl.BlockSpec((t, width), lambda b, s: (b * ns + s, 0))
    return pl.pallas_call(
        functools.partial(_qkv_kernel, dil=dil),
        grid=(batch, ns),
        in_specs=[row(D_MODEL),
                  pl.BlockSpec((1, D_MODEL), lambda b, s: (0, 0)),
                  pl.BlockSpec((D_MODEL, 3 * D_MODEL), lambda b, s: (0, 0)),
                  row(LANES), row(LANES)],
        out_specs=pl.BlockSpec((1, dil, t // dil, 3 * D_MODEL), lambda b, s: (b, 0, s, 0)),
        out_shape=jax.ShapeDtypeStruct((batch, dil, seq // dil, 3 * D_MODEL), jnp.bfloat16),
        scratch_shapes=[pltpu.VMEM((D_MODEL // LANES, t, LANES), jnp.float32)],
        compiler_params=_cparams("parallel", "parallel"),
        name=f"qkv_project_d{dil}",
    )(h, gain.reshape(1, D_MODEL), w_group, cos, sin)


def _attention_kernel(q_ref, k_ref, v_ref, o_ref, l_ref, k_win, v_win, bias_buf):
    blk = ATT_BLK
    n = pl.program_id(2)
    n_sub = q_ref.shape[2] // blk
    heads_per_slab = LANES // HALF_DIM
    n_slabs = N_HEADS // heads_per_slab
    rows = heads_per_slab * blk

    @pl.when(n == 0)
    def _():
        k_win[0:blk, :] = jnp.zeros((blk, D_MODEL), k_win.dtype)
        v_win[0:blk, :] = jnp.zeros((blk, D_MODEL), v_win.dtype)

    k_win[blk:(n_sub + 1) * blk, :] = k_ref[0, 0]
    v_win[blk:(n_sub + 1) * blk, :] = v_ref[0, 0]

    @pl.when((pl.program_id(0) == 0) & (pl.program_id(1) == 0) & (n == 0))
    def _():
        qi = lax.broadcasted_iota(jnp.int32, (rows, 2 * blk), 0) % blk
        ki = lax.broadcasted_iota(jnp.int32, (rows, 2 * blk), 1)
        own_ok = (ki >= blk) & (ki - blk <= qi)
        back_ok = (ki < blk) & (ki >= qi)
        bias_buf[0] = jnp.where(own_ok, 0.0, NEG_INF).astype(jnp.float32)
        bias_buf[1] = jnp.where(own_ok | back_ok, 0.0, NEG_INF).astype(jnp.float32)

    first_plane = jnp.minimum(n, 1)

    half = D_MODEL // 2
    lane2 = lax.broadcasted_iota(jnp.int32, (blk, 2 * LANES), 1)
    lane1 = lax.broadcasted_iota(jnp.int32, (blk, LANES), 1)

    def scores(unit):
        sub, m = unit
        q_rows = slice(sub * blk, (sub + 1) * blk)
        k_rows = slice(sub * blk, (sub + 2) * blk)
        q_slab = jnp.concatenate([q_ref[0, 0, q_rows, m * LANES:(m + 1) * LANES],
                                  q_ref[0, 0, q_rows, half + m * LANES:half + (m + 1) * LANES]], axis=1)
        k_slab = jnp.concatenate([k_win[k_rows, m * LANES:(m + 1) * LANES],
                                  k_win[k_rows, half + m * LANES:half + (m + 1) * LANES]], axis=1)
        zero = jnp.zeros_like(q_slab)
        q_heads = jnp.concatenate(
            [jnp.where(((lane2 % LANES) // HALF_DIM) == a, q_slab, zero) for a in range(heads_per_slab)], axis=0)
        bias = bias_buf[first_plane] if sub == 0 else bias_buf[1]
        return lax.dot_general(q_heads, k_slab, (((1,), (1,)), ((), ())),
                               preferred_element_type=jnp.float32) + bias

    units = [(sub, m) for sub in range(n_sub) for m in range(n_slabs)]
    l_ref[0, 0] = jnp.zeros(l_ref.shape[2:], jnp.float32)
    s_next = scores(units[0])
    for idx, (sub, m) in enumerate(units):
        s = s_next
        if idx + 1 < len(units):
            s_next = scores(units[idx + 1])
        mx = jnp.max(s, axis=-1, keepdims=True)
        p = jnp.exp2(s - mx)
        den = jnp.sum(p, axis=-1, keepdims=True)
        pb = p.astype(jnp.bfloat16)
        inv = 1.0 / den
        lse = (mx + jnp.log2(den)) * LN_2
        q_rows = slice(sub * blk, (sub + 1) * blk)
        k_rows = slice(sub * blk, (sub + 2) * blk)
        for a in range(heads_per_slab):
            hd = m * heads_per_slab + a
            l_ref[0, 0, q_rows, hd:hd + 1] = lse[a * blk:(a + 1) * blk]
        for pair in range(heads_per_slab // 2):
            col = (m * heads_per_slab + pair * 2) * HEAD_DIM
            r0 = pair * 2 * blk
            o2 = jnp.dot(pb[r0:r0 + 2 * blk], v_win[k_rows, col:col + LANES],
                         preferred_element_type=jnp.float32) * inv[r0:r0 + 2 * blk]
            o_ref[0, 0, q_rows, col:col + LANES] = jnp.where(
                lane1 < HEAD_DIM, o2[:blk], o2[blk:]).astype(o_ref.dtype)
    k_win[0:blk, :] = k_win[n_sub * blk:(n_sub + 1) * blk, :]
    v_win[0:blk, :] = v_win[n_sub * blk:(n_sub + 1) * blk, :]


def dilated_attention_group(qkv_g, dil):
    batch, _, length, _ = qkv_g.shape
    blk = ATT_BLK
    rows = min(ATT_BLOCKS_PER_STEP * blk, length)
    nstep = length // rows

    def part_spec(part):
        return pl.BlockSpec((1, 1, rows, D_MODEL), lambda b, r, i: (b, r, i, part))

    return pl.pallas_call(
        _attention_kernel,
        grid=(batch, dil, nstep),
        in_specs=[part_spec(0), part_spec(1), part_spec(2)],
        out_specs=[pl.BlockSpec((1, 1, rows, D_MODEL), lambda b, r, i: (b, r, i, 0)),
                   pl.BlockSpec((1, 1, rows, LANES), lambda b, r, i: (b, r, i, 0))],
        out_shape=[jax.ShapeDtypeStruct((batch, dil, length, D_MODEL), jnp.bfloat16),
                   jax.ShapeDtypeStruct((batch, dil, length, LANES), jnp.float32)],
        scratch_shapes=[pltpu.VMEM((rows + blk, D_MODEL), jnp.bfloat16),
                        pltpu.VMEM((rows + blk, D_MODEL), jnp.bfloat16),
                        pltpu.VMEM((2, (LANES // HALF_DIM) * blk, 2 * blk), jnp.float32)],
        compiler_params=_cparams("arbitrary", "arbitrary", "arbitrary"),
        name=f"dilated_attention_d{dil}",
    )(qkv_g, qkv_g, qkv_g)


def _merge_proj_kernel(*refs, dils):
    ng = len(dils)
    h_ref = refs[0]
    o_refs = refs[1:1 + ng]
    l_refs = refs[1 + ng:1 + 2 * ng]
    e_ref, w_ref, out_ref, o_buf, l_buf = refs[1 + 2 * ng:]
    t = h_ref.shape[0]
    lses = []
    for g, dil in enumerate(dils):
        if dil == 1:
            lses.append(l_refs[g][0, 0])
        else:
            for r in range(dil):
                l_buf[pl.ds(r, t // dil, stride=dil), :] = l_refs[g][0, r]
            lses.append(l_buf[...])
    top = functools.reduce(jnp.maximum, lses)
    ex = [jnp.exp(l - top) for l in lses]
    inv = 1.0 / functools.reduce(jnp.add, ex)
    merged = jnp.zeros((t, D_MODEL), jnp.float32)
    for g, dil in enumerate(dils):
        wg = ex[g] * inv
        hi = wg.astype(jnp.bfloat16)
        lo = (wg - hi.astype(jnp.float32)).astype(jnp.bfloat16)
        wexp = jnp.dot(jnp.concatenate([hi, lo], axis=1), e_ref[...], preferred_element_type=jnp.float32)
        if dil == 1:
            og = o_refs[g][0, 0].astype(jnp.float32)
        else:
            for r in range(dil):
                part = o_refs[g][0, r].astype(jnp.float32)
                for c in range(D_MODEL // LANES):
                    o_buf[c, pl.ds(r, t // dil, stride=dil), :] = part[:, c * LANES:(c + 1) * LANES]
            og = jnp.concatenate([o_buf[c] for c in range(D_MODEL // LANES)], axis=1)
        merged = merged + wexp * og
    out_ref[...] = h_ref[...] + jnp.dot(merged.astype(jnp.bfloat16), w_ref[...],
                                        preferred_element_type=jnp.float32)


def merge_proj_residual(h, batch, outs, lses, w_o):
    n = h.shape[0]
    seq = n // batch
    t = PROJ_TILE
    ns = seq // t
    dils = tuple(o.shape[1] for o in outs)
    row = pl.BlockSpec((t, D_MODEL), lambda b, s: (b * ns + s, 0))
    expand = np.zeros((2 * LANES, D_MODEL), np.float32)
    for hd in range(N_HEADS):
        expand[hd, hd * HEAD_DIM:(hd + 1) * HEAD_DIM] = 1.0
        expand[LANES + hd, hd * HEAD_DIM:(hd + 1) * HEAD_DIM] = 1.0
    in_specs = [row]
    in_specs += [pl.BlockSpec((1, d, t // d, D_MODEL), lambda b, s: (b, 0, s, 0)) for d in dils]
    in_specs += [pl.BlockSpec((1, d, t // d, LANES), lambda b, s: (b, 0, s, 0)) for d in dils]
    in_specs += [pl.BlockSpec((2 * LANES, D_MODEL), lambda b, s: (0, 0)),
                 pl.BlockSpec((D_MODEL, D_MODEL), lambda b, s: (0, 0))]
    return pl.pallas_call(
        functools.partial(_merge_proj_kernel, dils=dils),
        grid=(batch, ns), in_specs=in_specs, out_specs=row,
        out_shape=jax.ShapeDtypeStruct((n, D_MODEL), jnp.float32),
        scratch_shapes=[pltpu.VMEM((D_MODEL // LANES, t, LANES), jnp.float32),
                        pltpu.VMEM((t, LANES), jnp.float32)],
        compiler_params=_cparams("parallel", "parallel"), name="attn_merge_out_proj",
    )(h, *outs, *lses, jnp.asarray(expand, jnp.bfloat16), w_o)


def _router_kernel(h_ref, g_ref, w_ref, b_ref, tri_ref, hn_ref, route_ref, count_ref, run_ref):
    @pl.when(pl.program_id(0) == 0)
    def _():
        run_ref[...] = jnp.zeros_like(run_ref)

    hn = _rms(h_ref[...], g_ref[...])
    hn_ref[:, :PACKED] = _pack_rows(hn)
    hn_hi = hn.astype(jnp.bfloat16)
    hn_lo = (hn - hn_hi.astype(jnp.float32)).astype(jnp.bfloat16)
    both = jnp.dot(hn_hi, w_ref[...], preferred_element_type=jnp.float32)
    logits = (both[:, :LANES] + both[:, LANES:]
              + jnp.dot(hn_lo, w_ref[:, :LANES], preferred_element_type=jnp.float32) + b_ref[...])
    lane = lax.broadcasted_iota(jnp.int32, logits.shape, 1)
    big = jnp.int32(LANES)

    def first_argmax(vals, vmax):
        return jnp.min(jnp.where(vals == vmax, lane, big), axis=-1, keepdims=True)

    coarse = jnp.where(lane < N_EXPERT_GROUPS, logits, NEG_INF)
    cmax = jnp.max(coarse, axis=-1, keepdims=True)
    p_top = 1.0 / jnp.sum(jnp.exp(coarse - cmax), axis=-1, keepdims=True)
    g_top = first_argmax(coarse, cmax)
    lo = N_EXPERT_GROUPS + g_top * EXPERTS_PER_GROUP
    fine = jnp.where((lane >= lo) & (lane < lo + EXPERTS_PER_GROUP), logits, NEG_INF)
    v1 = jnp.max(fine, axis=-1, keepdims=True)
    i1 = first_argmax(fine, v1)
    fine2 = jnp.where(lane == i1, NEG_INF, fine)
    v2 = jnp.max(fine2, axis=-1, keepdims=True)
    i2 = first_argmax(fine2, v2)
    e2 = jnp.exp(v2 - v1)
    w1 = 1.0 / (1.0 + e2)
    w2 = e2 / (1.0 + e2)
    first_lo = i1 < i2
    a = jnp.where(first_lo, i1, i2) - lo
    b = jnp.where(first_lo, i2, i1) - lo
    pair = jnp.where(a == 0, b - 1, jnp.where(a == 1, jnp.where(b == 3, 3, 4), 5))
    cls = g_top * len(PAIR_SLOTS) + pair
    gate_lo = p_top * jnp.where(first_lo, w1, w2)
    gate_hi = p_top * jnp.where(first_lo, w2, w1)
    swapped = pair == 5
    gate_a = jnp.where(swapped, gate_hi, gate_lo)
    gate_b = jnp.where(swapped, gate_lo, gate_hi)
    gates = jnp.where(lane == 0, gate_a, jnp.where(lane == 1, gate_b, 0.0))
    hn_ref[:, PACKED:] = pltpu.bitcast(gates, jnp.uint32)

    sel = lane == cls
    onehot = jnp.where(sel, 1.0, 0.0)
    sub = tri_ref.shape[0]
    run = run_ref[...]
    befores = []
    for r0 in range(0, onehot.shape[0], sub):
        oh = onehot[r0:r0 + sub]
        befores.append(jnp.dot(tri_ref[...], oh.astype(jnp.bfloat16), preferred_element_type=jnp.float32) + run)
        run = run + jnp.sum(oh, axis=0, keepdims=True)
    before = jnp.concatenate(befores, axis=0)
    rank = jnp.sum(jnp.where(sel, before, 0.0), axis=-1, keepdims=True)
    run_ref[...] = run
    count_ref[...] = run

    route = jnp.where(lane == R_CLASS, cls.astype(jnp.float32), 0.0)
    route = jnp.where(lane == R_RANK, rank, route)
    route_ref[...] = route


def moe_router(h, gain, w_route, b_route):
    n = h.shape[0]
    tr = ROW_TILE
    sub = RANK_BLOCK
    tri = jnp.asarray(np.tril(np.ones((sub, sub), np.float32), -1), jnp.bfloat16)
    w_hi = w_route.astype(jnp.bfloat16)
    w_lo = (w_route - w_hi.astype(jnp.float32)).astype(jnp.bfloat16)
    w_split = jnp.concatenate([w_hi, w_lo], axis=1)
    return pl.pallas_call(
        _router_kernel, grid=(n // tr,),
        in_specs=[pl.BlockSpec((tr, D_MODEL), lambda i: (i, 0)),
                  pl.BlockSpec((1, D_MODEL), lambda i: (0, 0)),
                  pl.BlockSpec((D_MODEL, 2 * LANES), lambda i: (0, 0)),
                  pl.BlockSpec((1, LANES), lambda i: (0, 0)),
                  pl.BlockSpec((sub, sub), lambda i: (0, 0))],
        out_specs=[pl.BlockSpec((tr, ROW_WORDS), lambda i: (i, 0)),
                   pl.BlockSpec((tr, LANES), lambda i: (i, 0)),
                   pl.BlockSpec((1, LANES), lambda i: (0, 0))],
        out_shape=[jax.ShapeDtypeStruct((n, ROW_WORDS), jnp.uint32),
                   jax.ShapeDtypeStruct((n, LANES), jnp.float32),
                   jax.ShapeDtypeStruct((1, LANES), jnp.float32)],
        scratch_shapes=[pltpu.VMEM((1, LANES), jnp.float32)],
        compiler_params=_cparams("arbitrary"), name="moe_router",
    )(h, gain.reshape(1, D_MODEL), w_split, b_route, tri)


def _dispatch_kernel(dest_ref, x_ref, init_ref, o_ref, sem):
    del init_ref
    t = x_ref.shape[0]

    def row_copy(j):
        return pltpu.make_async_copy(x_ref.at[pl.ds(j, 1)], o_ref.at[pl.ds(dest_ref[0, 0, j], 1)], sem)

    for j in range(t):
        row_copy(j).start(priority=j % 2)

    def wait(j, c):
        row_copy(0).wait()
        return c

    lax.fori_loop(0, t, wait, 0, unroll=DMA_UNROLL)


def moe_dispatch(rows, dest, n_rows):
    n = rows.shape[0]
    t = DISPATCH_TILE
    return pl.pallas_call(
        _dispatch_kernel, grid=(n // t,),
        in_specs=[pl.BlockSpec((1, 1, t), lambda i: (i, 0, 0), memory_space=pltpu.SMEM),
                  pl.BlockSpec((t, ROW_WORDS), lambda i: (i, 0)),
                  pl.BlockSpec(memory_space=pl.ANY)],
        out_specs=pl.BlockSpec(memory_space=pl.ANY),
        out_shape=jax.ShapeDtypeStruct((n_rows, ROW_WORDS), jnp.uint32),
        scratch_shapes=[pltpu.SemaphoreType.DMA(())],
        input_output_aliases={2: 0},
        compiler_params=_cparams("arbitrary"), name="moe_dispatch",
    )(dest.reshape(n // t, 1, t), rows, jnp.zeros((n_rows, ROW_WORDS), jnp.uint32))


def _expert_kernel(ea_ref, eb_ref, na_ref, nb_ref, tr_ref, tv_ref, x_ref,
                   wga_ref, wua_ref, wda_ref, wgb_ref, wub_ref, wdb_ref, o_ref,
                   wga_bf, wua_bf, wda_bf, wgb_bf, wub_bf, wdb_bf):
    del ea_ref, eb_ref, tr_ref
    i = pl.program_id(0)

    @pl.when(na_ref[i] > 0)
    def _():
        wga_bf[...] = wga_ref[0, 0].astype(jnp.bfloat16)
        wua_bf[...] = wua_ref[0, 0].astype(jnp.bfloat16)
        wda_bf[...] = wda_ref[0, 0].astype(jnp.bfloat16)

    @pl.when(nb_ref[i] > 0)
    def _():
        wgb_bf[...] = wgb_ref[0, 0].astype(jnp.bfloat16)
        wub_bf[...] = wub_ref[0, 0].astype(jnp.bfloat16)
        wdb_bf[...] = wdb_ref[0, 0].astype(jnp.bfloat16)

    @pl.when(tv_ref[i] > 0)
    def _():
        x = _unpack_rows(x_ref[:, :PACKED]).astype(jnp.bfloat16)
        gates = pltpu.bitcast(x_ref[:, PACKED:], jnp.float32)

        def mlp(wg, wu, wd):
            g = jnp.dot(x, wg[...], preferred_element_type=jnp.float32)
            u = jnp.dot(x, wu[...], preferred_element_type=jnp.float32)
            hmid = (g * jax.nn.sigmoid(g) * u).astype(jnp.bfloat16)
            return jnp.dot(hmid, wd[...], preferred_element_type=jnp.float32)

        o_ref[...] = _pack_rows(gates[:, 0:1] * mlp(wga_bf, wua_bf, wda_bf)
                                + gates[:, 1:2] * mlp(wgb_bf, wub_bf, wdb_bf))

    @pl.when(tv_ref[i] == 0)
    def _():
        o_ref[...] = jnp.zeros_like(o_ref)


def expert_mlp(x_sorted, tables, layer, w_gate, w_up, w_down):
    p = x_sorted.shape[0]
    tm = EXPERT_TILE

    def weight_spec(shape, slot):
        return pl.BlockSpec((1, 1) + shape, lambda i, ea, eb, na, nb, tr, tv: (layer, (ea, eb)[slot][i], 0, 0))

    up_shape, down_shape = (D_MODEL, EXPERT_FF), (EXPERT_FF, D_MODEL)
    grid_spec = pltpu.PrefetchScalarGridSpec(
        num_scalar_prefetch=6,
        grid=(p // tm,),
        in_specs=[pl.BlockSpec((tm, ROW_WORDS), lambda i, ea, eb, na, nb, tr, tv: (tr[i], 0)),
                  weight_spec(up_shape, 0), weight_spec(up_shape, 0), weight_spec(down_shape, 0),
                  weight_spec(up_shape, 1), weight_spec(up_shape, 1), weight_spec(down_shape, 1)],
        out_specs=pl.BlockSpec((tm, PACKED), lambda i, ea, eb, na, nb, tr, tv: (i, 0)),
        scratch_shapes=[pltpu.VMEM(up_shape, jnp.bfloat16), pltpu.VMEM(up_shape, jnp.bfloat16),
                        pltpu.VMEM(down_shape, jnp.bfloat16)] * 2,
    )
    return pl.pallas_call(
        _expert_kernel, grid_spec=grid_spec,
        out_shape=jax.ShapeDtypeStruct((p, PACKED), jnp.uint32),
        compiler_params=_cparams("arbitrary"), name="expert_mlp",
    )(*tables, x_sorted, w_gate, w_up, w_down, w_gate, w_up, w_down)


def _combine_kernel(dcur_ref, dnext_ref, h_ref, gain_ref, y_ref, o_ref, y_buf, sem, *, final_norm):
    i = pl.program_id(0)
    last = pl.num_programs(0) - 1
    t = h_ref.shape[0]
    slot = i % 2

    def row_copy(dest_ref, s, j):
        return pltpu.make_async_copy(y_ref.at[pl.ds(dest_ref[0, 0, j], 1)],
                                     y_buf.at[s, pl.ds(j, 1)], sem.at[s])

    def start_all(dest_ref, s):
        for j in range(t):
            row_copy(dest_ref, s, j).start(priority=j % 2)

    @pl.when(i == 0)
    def _():
        start_all(dcur_ref, 0)

    for s in range(2):
        @pl.when((i < last) & (slot == 1 - s))
        def _():
            start_all(dnext_ref, s)

    lax.fori_loop(0, t, lambda j, c: (row_copy(dcur_ref, slot, 0).wait(), c)[1], 0, unroll=DMA_UNROLL)
    out = h_ref[...] + _unpack_rows(y_buf[slot])
    if final_norm:
        out = _rms(out, gain_ref[...])
    o_ref[...] = out


def moe_combine(h, y, dest, final_gain):
    n = h.shape[0]
    t = COMBINE_TILE
    nt = n // t
    dest3 = dest.reshape(nt, 1, t)
    final_norm = final_gain is not None
    gain = final_gain if final_norm else jnp.ones((D_MODEL,), jnp.float32)
    return pl.pallas_call(
        functools.partial(_combine_kernel, final_norm=final_norm), grid=(nt,),
        in_specs=[pl.BlockSpec((1, 1, t), lambda i: (i, 0, 0), memory_space=pltpu.SMEM),
                  pl.BlockSpec((1, 1, t), lambda i: (jnp.minimum(i + 1, nt - 1), 0, 0),
                               memory_space=pltpu.SMEM),
                  pl.BlockSpec((t, D_MODEL), lambda i: (i, 0)),
                  pl.BlockSpec((1, D_MODEL), lambda i: (0, 0)),
                  pl.BlockSpec(memory_space=pl.ANY)],
        out_specs=pl.BlockSpec((t, D_MODEL), lambda i: (i, 0)),
        out_shape=jax.ShapeDtypeStruct((n, D_MODEL), jnp.float32),
        scratch_shapes=[pltpu.VMEM((2, t, PACKED), jnp.uint32), pltpu.SemaphoreType.DMA((2,))],
        compiler_params=_cparams("arbitrary"), name="moe_combine",
    )(dest3, dest3, h, gain.reshape(1, D_MODEL), y)


def hierarchical_moe(h, gain, w_route, b_route, layer, w_gate, w_up, w_down, final_gain=None):
    n = h.shape[0]
    tm = EXPERT_TILE
    rows, route, counts = moe_router(h, gain, w_route, b_route)
    counts = counts[0, :N_CLASSES].astype(jnp.int32)
    padded = ((counts + tm - 1) // tm) * tm
    seg_end = jnp.cumsum(padded)
    seg_start = seg_end - padded
    cls = route[:, R_CLASS].astype(jnp.int32)
    rank = route[:, R_RANK].astype(jnp.int32)
    class_ids = jnp.arange(N_CLASSES, dtype=jnp.int32)
    dest = jnp.sum(jnp.where(cls[:, None] == class_ids[None, :], seg_start[None, :], 0), axis=-1) + rank
    n_rows = n + N_CLASSES * tm
    n_tiles = n_rows // tm
    used = seg_end[-1] // tm
    tile_id = jnp.arange(n_tiles, dtype=jnp.int32)
    tile_row = jnp.minimum(tile_id, used - 1).astype(jnp.int32)
    tile_class = jnp.minimum(jnp.sum((seg_end[None, :] <= (tile_row * tm)[:, None]).astype(jnp.int32), axis=1),
                             N_CLASSES - 1)
    tile_valid = (tile_id < used).astype(jnp.int32)
    slots = np.asarray([[g * EXPERTS_PER_GROUP + s for s in pair]
                        for g in range(N_EXPERT_GROUPS) for pair in PAIR_SLOTS], np.int32)
    tile_onehot = (tile_class[:, None] == class_ids[None, :]).astype(jnp.int32)
    expert_a = jnp.sum(tile_onehot * slots[None, :, 0], axis=1).astype(jnp.int32)
    expert_b = jnp.sum(tile_onehot * slots[None, :, 1], axis=1).astype(jnp.int32)

    def changed(e):
        return jnp.concatenate([jnp.ones((1,), jnp.int32), (e[1:] != e[:-1]).astype(jnp.int32)])

    tables = (expert_a, expert_b, changed(expert_a), changed(expert_b), tile_row, tile_valid)
    x_sorted = moe_dispatch(rows, dest, n_rows)
    y = expert_mlp(x_sorted, tables, layer, w_gate, w_up, w_down)
    return moe_combine(h, y, dest, final_gain)


def _permute_qk_columns(w_qkv):
    w = w_qkv.astype(jnp.bfloat16).reshape(D_MODEL, N_ATTN_GROUPS, 3, N_HEADS, 2, HALF_DIM)
    qk = jnp.transpose(w[:, :, 0:2], (0, 1, 2, 4, 3, 5)).reshape(D_MODEL, N_ATTN_GROUPS, 2, D_MODEL)
    v = w[:, :, 2:3].reshape(D_MODEL, N_ATTN_GROUPS, 1, D_MODEL)
    return jnp.concatenate([qk, v], axis=2).reshape(D_MODEL, QKV_WIDTH)


def kernel(x, positions, norm_mix, norm_ffn, norm_final, conv_w_in, conv_w, conv_w_out,
           attn_w_qkv, attn_w_o, w_coarse, b_coarse, w_fine, b_fine, w_gate, w_up, w_down):
    batch, seq, d = x.shape
    assert d == D_MODEL
    n = batch * seq
    depth = norm_mix.shape[0]
    bf = jnp.bfloat16
    h = x.reshape(n, d)
    cos, sin = rope_tables(positions.reshape(n))
    for i in range(depth):
        j = i // 2
        if i % 2 == 0:
            h = conv_mixer(h, batch, norm_mix[i], conv_w_in[j].astype(bf), conv_w[j], conv_w_out[j].astype(bf))
        else:
            w_qkv = _permute_qk_columns(attn_w_qkv[j])
            outs, lses = [], []
            for g, (window, dil) in enumerate(DILATED_GROUPS):
                assert window // dil == ATT_BLK
                w_group = w_qkv[:, g * 3 * D_MODEL:(g + 1) * 3 * D_MODEL]
                o, lse = dilated_attention_group(
                    qkv_project(h, batch, norm_mix[i], w_group, cos, sin, dil), dil)
                outs.append(o)
                lses.append(lse)
            h = merge_proj_residual(h, batch, outs, lses, attn_w_o[j].astype(bf))
        w_route = jnp.concatenate(
            [w_coarse[i], jnp.transpose(w_fine[i], (1, 0, 2)).reshape(d, N_EXPERTS),
             jnp.zeros((d, LANES - N_EXPERT_GROUPS - N_EXPERTS), jnp.float32)], axis=1)
        b_route = jnp.concatenate(
            [b_coarse[i], b_fine[i].reshape(-1),
             jnp.zeros((LANES - N_EXPERT_GROUPS - N_EXPERTS,), jnp.float32)]).reshape(1, LANES)
        h = hierarchical_moe(h, norm_ffn[i], w_route, b_route,
                             i, w_gate, w_up, w_down,
                             norm_final if i == depth - 1 else None)
    return h.reshape(batch, seq, d)
```

```python
import functools
import math

import numpy as np
import jax
import jax.numpy as jnp
from jax import lax
from jax.experimental import pallas as pl
from jax.experimental.pallas import tpu as pltpu

D_MODEL = 1024
NORM_EPS = 1e-6
CONV_WIDTH = 3
HEAD_DIM = 64
HALF_DIM = HEAD_DIM // 2
N_HEADS = D_MODEL // HEAD_DIM
DILATED_GROUPS = ((128, 1), (512, 4), (2048, 16))
N_ATTN_GROUPS = len(DILATED_GROUPS)
QKV_WIDTH = N_ATTN_GROUPS * 3 * D_MODEL
ROPE_THETA = 10000.0
N_EXPERT_GROUPS = 4
EXPERTS_PER_GROUP = 4
N_EXPERTS = N_EXPERT_GROUPS * EXPERTS_PER_GROUP
EXPERT_FF = D_MODEL // 2
PACKED = D_MODEL // 2

LANES = 128
ATT_BLK = 128
ATT_BLOCKS_PER_STEP = 8
VMEM_LIMIT_BYTES = 56 * 1024 * 1024
ROW_TILE = 1024
PROJ_TILE = 512
CONV_TILE = 512
EXPERT_TILE = 256
DISPATCH_TILE = 1024
COMBINE_TILE = 1024
RANK_BLOCK = 256
DMA_UNROLL = 8
NEG_INF = float("-inf")
LOG2_E = math.log2(math.e)
R_CLASS, R_RANK = range(2)
PAIR_SLOTS = ((0, 1), (0, 2), (0, 3), (1, 3), (1, 2), (3, 2))
N_CLASSES = N_EXPERT_GROUPS * len(PAIR_SLOTS)
ROW_WORDS = PACKED + LANES


def _cparams(*sem):
    return pltpu.CompilerParams(dimension_semantics=sem, vmem_limit_bytes=VMEM_LIMIT_BYTES)


def _rms(x, g):
    return x * lax.rsqrt(jnp.mean(x * x, axis=-1, keepdims=True) + NORM_EPS) * g


def _pack_rows(x):
    lo = pltpu.bitcast(x[:, :PACKED].astype(jnp.bfloat16).astype(jnp.float32), jnp.uint32)
    hi = pltpu.bitcast(x[:, PACKED:].astype(jnp.bfloat16).astype(jnp.float32), jnp.uint32)
    return (lo >> 16) | (hi & jnp.uint32(0xFFFF0000))


def _unpack_rows(u):
    lo = pltpu.bitcast(u << 16, jnp.float32)
    hi = pltpu.bitcast(u & jnp.uint32(0xFFFF0000), jnp.float32)
    return jnp.concatenate([lo, hi], axis=1)


def _rope_kernel(pos_ref, freq_ref, cos_ref, sin_ref):
    ang = pos_ref[...].astype(jnp.float32) * freq_ref[...]
    cos_ref[...] = jnp.cos(ang)
    sin_ref[...] = jnp.sin(ang)


def rope_tables(positions):
    n = positions.shape[0]
    inv_freq = (ROPE_THETA ** (-np.arange(0, HEAD_DIM, 2, dtype=np.float32) / HEAD_DIM)).astype(np.float32)
    freq = jnp.asarray(np.tile(inv_freq, LANES // HALF_DIM)[None, :])
    tr = ROW_TILE
    return pl.pallas_call(
        _rope_kernel,
        grid=(n // tr,),
        in_specs=[pl.BlockSpec((tr, 1), lambda i: (i, 0)),
                  pl.BlockSpec((1, LANES), lambda i: (0, 0))],
        out_specs=[pl.BlockSpec((tr, LANES), lambda i: (i, 0)),
                   pl.BlockSpec((tr, LANES), lambda i: (i, 0))],
        out_shape=[jax.ShapeDtypeStruct((n, LANES), jnp.float32)] * 2,
        compiler_params=_cparams("parallel"),
        name="rope_tables",
    )(positions.reshape(n, 1), freq)


def _conv_mixer_kernel(h_ref, g_ref, win_ref, cw_ref, wout_ref, o_ref, hn_buf, u_buf, v_buf):
    t = h_ref.shape[0]

    @pl.when(pl.program_id(1) == 0)
    def _():
        u_buf[0:8, :] = jnp.zeros((8, D_MODEL), jnp.float32)

    x = h_ref[...]
    hn_buf[...] = _rms(x, g_ref[...]).astype(jnp.bfloat16)
    cw = cw_ref[...]
    chunk = 512
    for c in range(0, D_MODEL, chunk):
        hn = hn_buf[...]
        gate_c = jnp.dot(hn, win_ref[:, D_MODEL + c:D_MODEL + c + chunk], preferred_element_type=jnp.float32)
        hh = jnp.dot(hn, win_ref[:, 2 * D_MODEL + c:2 * D_MODEL + c + chunk], preferred_element_type=jnp.float32)
        u_buf[8:8 + t, c:c + chunk] = gate_c * hh
        conv = (cw[0:1, c:c + chunk] * u_buf[6:6 + t, c:c + chunk]
                + cw[1:2, c:c + chunk] * u_buf[7:7 + t, c:c + chunk]
                + cw[2:3, c:c + chunk] * u_buf[8:8 + t, c:c + chunk])
        gate_b = jnp.dot(hn, win_ref[:, c:c + chunk], preferred_element_type=jnp.float32)
        v_buf[:, c:c + chunk] = (gate_b * conv).astype(jnp.bfloat16)
    u_buf[0:8, :] = u_buf[t:t + 8, :]
    o_ref[...] = x + jnp.dot(v_buf[...], wout_ref[...], preferred_element_type=jnp.float32)


def conv_mixer(h, batch, gain, w_in, conv_w, w_out):
    n = h.shape[0]
    seq = n // batch
    t = CONV_TILE
    ns = seq // t
    row = pl.BlockSpec((t, D_MODEL), lambda b, s: (b * ns + s, 0))
    return pl.pallas_call(
        _conv_mixer_kernel,
        grid=(batch, ns),
        in_specs=[row,
                  pl.BlockSpec((1, D_MODEL), lambda b, s: (0, 0)),
                  pl.BlockSpec((D_MODEL, 3 * D_MODEL), lambda b, s: (0, 0)),
                  pl.BlockSpec((CONV_WIDTH, D_MODEL), lambda b, s: (0, 0)),
                  pl.BlockSpec((D_MODEL, D_MODEL), lambda b, s: (0, 0))],
        out_specs=row,
        out_shape=jax.ShapeDtypeStruct((n, D_MODEL), jnp.float32),
        scratch_shapes=[pltpu.VMEM((t, D_MODEL), jnp.bfloat16),
                        pltpu.VMEM((t + 8, D_MODEL), jnp.float32),
                        pltpu.VMEM((t, D_MODEL), jnp.bfloat16)],
        compiler_params=_cparams("arbitrary", "arbitrary"),
        name="conv_mixer",
    )(h, gain.reshape(1, D_MODEL), w_in, conv_w, w_out)


def _qkv_kernel(h_ref, g_ref, w_ref, cos_ref, sin_ref, o_ref, z_buf, *, dil):
    t = h_ref.shape[0]
    half = D_MODEL // 2
    reps = half // LANES
    rows = t // dil

    def residue_major(x):
        if dil == 1:
            return x
        planes = x.shape[1] // LANES
        for c in range(planes):
            z_buf[c] = x[:, c * LANES:(c + 1) * LANES]
        return jnp.concatenate(
            [jnp.concatenate([z_buf[c, pl.ds(r, rows, stride=dil), :] for r in range(dil)], axis=0)
             for c in range(planes)], axis=1)

    cos128 = residue_major(cos_ref[...])
    sin128 = residue_major(sin_ref[...])
    cos = jnp.concatenate([cos128] * reps, axis=1)
    sin = jnp.concatenate([sin128] * reps, axis=1)
    hn = residue_major(_rms(h_ref[...], g_ref[...])).astype(jnp.bfloat16)
    for part in range(3):
        z = jnp.dot(hn, w_ref[:, part * D_MODEL:(part + 1) * D_MODEL], preferred_element_type=jnp.float32)
        if part < 2:
            z1 = z[:, :half]
            z2 = z[:, half:]
            z = jnp.concatenate([z1 * cos - z2 * sin, z2 * cos + z1 * sin], axis=1)
        if part == 0:
            z = z * (HEAD_DIM ** -0.5 * LOG2_E)
        z = z.astype(o_ref.dtype)
        for r in range(dil):
            o_ref[0, r, :, part * D_MODEL:(part + 1) * D_MODEL] = z[r * rows:(r + 1) * rows]


def qkv_project(h, batch, gain, w_group, cos, sin, dil):
    n = h.shape[0]
    seq = n // batch
    t = PROJ_TILE
    ns = seq // t
    row = lambda width: pl.BlockSpec((t, width), lambda b, s: (b * ns + s, 0))
    return pl.pallas_call(
        functools.partial(_qkv_kernel, dil=dil),
        grid=(batch, ns),
        in_specs=[row(D_MODEL),
                  pl.BlockSpec((1, D_MODEL), lambda b, s: (0, 0)),
                  pl.BlockSpec((D_MODEL, 3 * D_MODEL), lambda b, s: (0, 0)),
                  row(LANES), row(LANES)],
        out_specs=pl.BlockSpec((1, dil, t // dil, 3 * D_MODEL), lambda b, s: (b, 0, s, 0)),
        out_shape=jax.ShapeDtypeStruct((batch, dil, seq // dil, 3 * D_MODEL), jnp.bfloat16),
        scratch_shapes=[pltpu.VMEM((D_MODEL // LANES, t, LANES), jnp.float32)],
        compiler_params=_cparams("parallel", "parallel"),
        name=f"qkv_project_d{dil}",
    )(h, gain.reshape(1, D_MODEL), w_group, cos, sin)


def _attention_kernel(q_ref, k_ref, v_ref, o_ref, m_ref, d_ref, k_win, v_win, bias_buf):
    blk = ATT_BLK
    n = pl.program_id(2)
    n_sub = q_ref.shape[2] // blk
    heads_per_slab = LANES // HALF_DIM
    n_slabs = N_HEADS // heads_per_slab
    rows = heads_per_slab * blk

    @pl.when(n == 0)
    def _():
        k_win[0:blk, :] = jnp.zeros((blk, D_MODEL), k_win.dtype)
        v_win[0:blk, :] = jnp.zeros((blk, D_MODEL), v_win.dtype)

    k_win[blk:(n_sub + 1) * blk, :] = k_ref[0, 0]
    v_win[blk:(n_sub + 1) * blk, :] = v_ref[0, 0]

    @pl.when((pl.program_id(0) == 0) & (pl.program_id(1) == 0) & (n == 0))
    def _():
        qi = lax.broadcasted_iota(jnp.int32, (rows, 2 * blk), 0) % blk
        ki = lax.broadcasted_iota(jnp.int32, (rows, 2 * blk), 1)
        own_ok = (ki >= blk) & (ki - blk <= qi)
        back_ok = (ki < blk) & (ki >= qi)
        bias_buf[0] = jnp.where(own_ok, 0.0, NEG_INF).astype(jnp.float32)
        bias_buf[1] = jnp.where(own_ok | back_ok, 0.0, NEG_INF).astype(jnp.float32)

    first_plane = jnp.minimum(n, 1)

    half = D_MODEL // 2
    lane2 = lax.broadcasted_iota(jnp.int32, (blk, 2 * LANES), 1)
    lane1 = lax.broadcasted_iota(jnp.int32, (blk, LANES), 1)

    def scores(unit):
        sub, m = unit
        q_rows = slice(sub * blk, (sub + 1) * blk)
        k_rows = slice(sub * blk, (sub + 2) * blk)
        q_slab = jnp.concatenate([q_ref[0, 0, q_rows, m * LANES:(m + 1) * LANES],
                                  q_ref[0, 0, q_rows, half + m * LANES:half + (m + 1) * LANES]], axis=1)
        k_slab = jnp.concatenate([k_win[k_rows, m * LANES:(m + 1) * LANES],
                                  k_win[k_rows, half + m * LANES:half + (m + 1) * LANES]], axis=1)
        zero = jnp.zeros_like(q_slab)
        q_heads = jnp.concatenate(
            [jnp.where(((lane2 % LANES) // HALF_DIM) == a, q_slab, zero) for a in range(heads_per_slab)], axis=0)
        bias = bias_buf[first_plane] if sub == 0 else bias_buf[1]
        return lax.dot_general(q_heads, k_slab, (((1,), (1,)), ((), ())),
                               preferred_element_type=jnp.float32) + bias

    units = [(sub, m) for sub in range(n_sub) for m in range(n_slabs)]
    m_ref[0, 0] = jnp.zeros(m_ref.shape[2:], jnp.float32)
    d_ref[0, 0] = jnp.ones(d_ref.shape[2:], jnp.float32)
    s_next = scores(units[0])
    for idx, (sub, m) in enumerate(units):
        s = s_next
        if idx + 1 < len(units):
            s_next = scores(units[idx + 1])
        mx = jnp.max(s, axis=-1, keepdims=True)
        p = jnp.exp2(s - mx)
        den = jnp.sum(p, axis=-1, keepdims=True)
        pb = p.astype(jnp.bfloat16)
        q_rows = slice(sub * blk, (sub + 1) * blk)
        k_rows = slice(sub * blk, (sub + 2) * blk)
        for a in range(heads_per_slab):
            hd = m * heads_per_slab + a
            m_ref[0, 0, q_rows, hd:hd + 1] = mx[a * blk:(a + 1) * blk]
            d_ref[0, 0, q_rows, hd:hd + 1] = den[a * blk:(a + 1) * blk]
        for pair in range(heads_per_slab // 2):
            col = (m * heads_per_slab + pair * 2) * HEAD_DIM
            r0 = pair * 2 * blk
            o2 = jnp.dot(pb[r0:r0 + 2 * blk], v_win[k_rows, col:col + LANES],
                         preferred_element_type=jnp.float32)
            o_ref[0, 0, q_rows, col:col + LANES] = jnp.where(
                lane1 < HEAD_DIM, o2[:blk], o2[blk:]).astype(o_ref.dtype)
    k_win[0:blk, :] = k_win[n_sub * blk:(n_sub + 1) * blk, :]
    v_win[0:blk, :] = v_win[n_sub * blk:(n_sub + 1) * blk, :]


def dilated_attention_group(qkv_g, dil):
    batch, _, length, _ = qkv_g.shape
    blk = ATT_BLK
    rows = min(ATT_BLOCKS_PER_STEP * blk, length)
    nstep = length // rows

    def part_spec(part):
        return pl.BlockSpec((1, 1, rows, D_MODEL), lambda b, r, i: (b, r, i, part))

    return pl.pallas_call(
        _attention_kernel,
        grid=(batch, dil, nstep),
        in_specs=[part_spec(0), part_spec(1), part_spec(2)],
        out_specs=[pl.BlockSpec((1, 1, rows, D_MODEL), lambda b, r, i: (b, r, i, 0)),
                   pl.BlockSpec((1, 1, rows, LANES), lambda b, r, i: (b, r, i, 0)),
                   pl.BlockSpec((1, 1, rows, LANES), lambda b, r, i: (b, r, i, 0))],
        out_shape=[jax.ShapeDtypeStruct((batch, dil, length, D_MODEL), jnp.bfloat16),
                   jax.ShapeDtypeStruct((batch, dil, length, LANES), jnp.float32),
                   jax.ShapeDtypeStruct((batch, dil, length, LANES), jnp.float32)],
        scratch_shapes=[pltpu.VMEM((rows + blk, D_MODEL), jnp.bfloat16),
                        pltpu.VMEM((rows + blk, D_MODEL), jnp.bfloat16),
                        pltpu.VMEM((2, (LANES // HALF_DIM) * blk, 2 * blk), jnp.float32)],
        compiler_params=_cparams("arbitrary", "arbitrary", "arbitrary"),
        name=f"dilated_attention_d{dil}",
    )(qkv_g, qkv_g, qkv_g)


def _merge_proj_kernel(*refs, dils):
    ng = len(dils)
    h_ref = refs[0]
    o_refs = refs[1:1 + ng]
    m_refs = refs[1 + ng:1 + 2 * ng]
    d_refs = refs[1 + 2 * ng:1 + 3 * ng]
    e_ref, w_ref, out_ref, o_buf, l_buf = refs[1 + 3 * ng:]
    t = h_ref.shape[0]

    def token_major(ref, dil):
        if dil == 1:
            return ref[0, 0]
        for r in range(dil):
            l_buf[pl.ds(r, t // dil, stride=dil), :] = ref[0, r]
        return l_buf[...]

    maxima = [token_major(m_refs[g], dil) for g, dil in enumerate(dils)]
    lses = [maxima[g] + jnp.log2(token_major(d_refs[g], dil)) for g, dil in enumerate(dils)]
    top = functools.reduce(jnp.maximum, lses)
    inv = 1.0 / functools.reduce(jnp.add, [jnp.exp2(l - top) for l in lses])
    merged = jnp.zeros((t, D_MODEL), jnp.float32)
    for g, dil in enumerate(dils):
        wg = jnp.exp2(maxima[g] - top) * inv
        hi = wg.astype(jnp.bfloat16)
        lo = (wg - hi.astype(jnp.float32)).astype(jnp.bfloat16)
        wexp = jnp.dot(jnp.concatenate([hi, lo], axis=1), e_ref[...], preferred_element_type=jnp.float32)
        if dil == 1:
            og = o_refs[g][0, 0].astype(jnp.float32)
        else:
            for r in range(dil):
                part = o_refs[g][0, r].astype(jnp.float32)
                for c in range(D_MODEL // LANES):
                    o_buf[c, pl.ds(r, t // dil, stride=dil), :] = part[:, c * LANES:(c + 1) * LANES]
            og = jnp.concatenate([o_buf[c] for c in range(D_MODEL // LANES)], axis=1)
        merged = merged + wexp * og
    out_ref[...] = h_ref[...] + jnp.dot(merged.astype(jnp.bfloat16), w_ref[...],
                                        preferred_element_type=jnp.float32)


def merge_proj_residual(h, batch, outs, maxima, dens, w_o):
    n = h.shape[0]
    seq = n // batch
    t = PROJ_TILE
    ns = seq // t
    dils = tuple(o.shape[1] for o in outs)
    row = pl.BlockSpec((t, D_MODEL), lambda b, s: (b * ns + s, 0))
    expand = np.zeros((2 * LANES, D_MODEL), np.float32)
    for hd in range(N_HEADS):
        expand[hd, hd * HEAD_DIM:(hd + 1) * HEAD_DIM] = 1.0
        expand[LANES + hd, hd * HEAD_DIM:(hd + 1) * HEAD_DIM] = 1.0
    in_specs = [row]
    in_specs += [pl.BlockSpec((1, d, t // d, D_MODEL), lambda b, s: (b, 0, s, 0)) for d in dils]
    in_specs += [pl.BlockSpec((1, d, t // d, LANES), lambda b, s: (b, 0, s, 0)) for d in dils] * 2
    in_specs += [pl.BlockSpec((2 * LANES, D_MODEL), lambda b, s: (0, 0)),
                 pl.BlockSpec((D_MODEL, D_MODEL), lambda b, s: (0, 0))]
    return pl.pallas_call(
        functools.partial(_merge_proj_kernel, dils=dils),
        grid=(batch, ns), in_specs=in_specs, out_specs=row,
        out_shape=jax.ShapeDtypeStruct((n, D_MODEL), jnp.float32),
        scratch_shapes=[pltpu.VMEM((D_MODEL // LANES, t, LANES), jnp.float32),
                        pltpu.VMEM((t, LANES), jnp.float32)],
        compiler_params=_cparams("parallel", "parallel"), name="attn_merge_out_proj",
    )(h, *outs, *maxima, *dens, jnp.asarray(expand, jnp.bfloat16), w_o)


def _router_kernel(h_ref, g_ref, w_ref, b_ref, tri_ref, hn_ref, route_ref, count_ref, run_ref):
    @pl.when(pl.program_id(0) == 0)
    def _():
        run_ref[...] = jnp.zeros_like(run_ref)

    hn = _rms(h_ref[...], g_ref[...])
    hn_ref[:, :PACKED] = _pack_rows(hn)
    hn_hi = hn.astype(jnp.bfloat16)
    hn_lo = (hn - hn_hi.astype(jnp.float32)).astype(jnp.bfloat16)
    both = jnp.dot(hn_hi, w_ref[...], preferred_element_type=jnp.float32)
    logits = (both[:, :LANES] + both[:, LANES:]
              + jnp.dot(hn_lo, w_ref[:, :LANES], preferred_element_type=jnp.float32) + b_ref[...])
    lane = lax.broadcasted_iota(jnp.int32, logits.shape, 1)
    big = jnp.int32(LANES)

    def first_argmax(vals, vmax):
        return jnp.min(jnp.where(vals == vmax, lane, big), axis=-1, keepdims=True)

    coarse = jnp.where(lane < N_EXPERT_GROUPS, logits, NEG_INF)
    cmax = jnp.max(coarse, axis=-1, keepdims=True)
    p_top = 1.0 / jnp.sum(jnp.exp(coarse - cmax), axis=-1, keepdims=True)
    g_top = first_argmax(coarse, cmax)
    lo = N_EXPERT_GROUPS + g_top * EXPERTS_PER_GROUP
    fine = jnp.where((lane >= lo) & (lane < lo + EXPERTS_PER_GROUP), logits, NEG_INF)
    v1 = jnp.max(fine, axis=-1, keepdims=True)
    i1 = first_argmax(fine, v1)
    fine2 = jnp.where(lane == i1, NEG_INF, fine)
    v2 = jnp.max(fine2, axis=-1, keepdims=True)
    i2 = first_argmax(fine2, v2)
    e2 = jnp.exp(v2 - v1)
    w1 = 1.0 / (1.0 + e2)
    w2 = e2 / (1.0 + e2)
    first_lo = i1 < i2
    a = jnp.where(first_lo, i1, i2) - lo
    b = jnp.where(first_lo, i2, i1) - lo
    pair = jnp.where(a == 0, b - 1, jnp.where(a == 1, jnp.where(b == 3, 3, 4), 5))
    cls = g_top * len(PAIR_SLOTS) + pair
    gate_lo = p_top * jnp.where(first_lo, w1, w2)
    gate_hi = p_top * jnp.where(first_lo, w2, w1)
    swapped = pair == 5
    gate_a = jnp.where(swapped, gate_hi, gate_lo)
    gate_b = jnp.where(swapped, gate_lo, gate_hi)
    gates = jnp.where(lane == 0, gate_a, jnp.where(lane == 1, gate_b, 0.0))
    hn_ref[:, PACKED:] = pltpu.bitcast(gates, jnp.uint32)

    sel = lane == cls
    onehot = jnp.where(sel, 1.0, 0.0)
    sub = tri_ref.shape[0]
    run = run_ref[...]
    befores = []
    for r0 in range(0, onehot.shape[0], sub):
        oh = onehot[r0:r0 + sub]
        befores.append(jnp.dot(tri_ref[...], oh.astype(jnp.bfloat16), preferred_element_type=jnp.float32) + run)
        run = run + jnp.sum(oh, axis=0, keepdims=True)
    before = jnp.concatenate(befores, axis=0)
    rank = jnp.sum(jnp.where(sel, before, 0.0), axis=-1, keepdims=True)
    run_ref[...] = run
    count_ref[...] = run

    route = jnp.where(lane == R_CLASS, cls.astype(jnp.float32), 0.0)
    route = jnp.where(lane == R_RANK, rank, route)
    route_ref[...] = route


def moe_router(h, gain, w_route, b_route):
    n = h.shape[0]
    tr = ROW_TILE
    sub = RANK_BLOCK
    tri = jnp.asarray(np.tril(np.ones((sub, sub), np.float32), -1), jnp.bfloat16)
    w_hi = w_route.astype(jnp.bfloat16)
    w_lo = (w_route - w_hi.astype(jnp.float32)).astype(jnp.bfloat16)
    w_split = jnp.concatenate([w_hi, w_lo], axis=1)
    return pl.pallas_call(
        _router_kernel, grid=(n // tr,),
        in_specs=[pl.BlockSpec((tr, D_MODEL), lambda i: (i, 0)),
                  pl.BlockSpec((1, D_MODEL), lambda i: (0, 0)),
                  pl.BlockSpec((D_MODEL, 2 * LANES), lambda i: (0, 0)),
                  pl.BlockSpec((1, LANES), lambda i: (0, 0)),
                  pl.BlockSpec((sub, sub), lambda i: (0, 0))],
        out_specs=[pl.BlockSpec((tr, ROW_WORDS), lambda i: (i, 0)),
                   pl.BlockSpec((tr, LANES), lambda i: (i, 0)),
                   pl.BlockSpec((1, LANES), lambda i: (0, 0))],
        out_shape=[jax.ShapeDtypeStruct((n, ROW_WORDS), jnp.uint32),
                   jax.ShapeDtypeStruct((n, LANES), jnp.float32),
                   jax.ShapeDtypeStruct((1, LANES), jnp.float32)],
        scratch_shapes=[pltpu.VMEM((1, LANES), jnp.float32)],
        compiler_params=_cparams("arbitrary"), name="moe_router",
    )(h, gain.reshape(1, D_MODEL), w_split, b_route, tri)


def _dispatch_kernel(dest_ref, x_ref, init_ref, o_ref, sem):
    del init_ref
    t = x_ref.shape[0]

    def row_copy(j):
        return pltpu.make_async_copy(x_ref.at[pl.ds(j, 1)], o_ref.at[pl.ds(dest_ref[0, 0, j], 1)], sem)

    for j in range(t):
        row_copy(j).start(priority=j % 2)

    def wait(j, c):
        row_copy(0).wait()
        return c

    lax.fori_loop(0, t, wait, 0, unroll=DMA_UNROLL)


def moe_dispatch(rows, dest, n_rows):
    n = rows.shape[0]
    t = DISPATCH_TILE
    return pl.pallas_call(
        _dispatch_kernel, grid=(n // t,),
        in_specs=[pl.BlockSpec((1, 1, t), lambda i: (i, 0, 0), memory_space=pltpu.SMEM),
                  pl.BlockSpec((t, ROW_WORDS), lambda i: (i, 0)),
                  pl.BlockSpec(memory_space=pl.ANY)],
        out_specs=pl.BlockSpec(memory_space=pl.ANY),
        out_shape=jax.ShapeDtypeStruct((n_rows, ROW_WORDS), jnp.uint32),
        scratch_shapes=[pltpu.SemaphoreType.DMA(())],
        input_output_aliases={2: 0},
        compiler_params=_cparams("arbitrary"), name="moe_dispatch",
    )(dest.reshape(n // t, 1, t), rows, jnp.zeros((n_rows, ROW_WORDS), jnp.uint32))


def _expert_kernel(ea_ref, eb_ref, na_ref, nb_ref, tr_ref, tv_ref, x_ref,
                   wga_ref, wua_ref, wda_ref, wgb_ref, wub_ref, wdb_ref, o_ref,
                   wga_bf, wua_bf, wda_bf, wgb_bf, wub_bf, wdb_bf):
    del ea_ref, eb_ref, tr_ref
    i = pl.program_id(0)

    @pl.when(na_ref[i] > 0)
    def _():
        wga_bf[...] = wga_ref[0, 0].astype(jnp.bfloat16)
        wua_bf[...] = wua_ref[0, 0].astype(jnp.bfloat16)
        wda_bf[...] = wda_ref[0, 0].astype(jnp.bfloat16)

    @pl.when(nb_ref[i] > 0)
    def _():
        wgb_bf[...] = wgb_ref[0, 0].astype(jnp.bfloat16)
        wub_bf[...] = wub_ref[0, 0].astype(jnp.bfloat16)
        wdb_bf[...] = wdb_ref[0, 0].astype(jnp.bfloat16)

    @pl.when(tv_ref[i] > 0)
    def _():
        x = _unpack_rows(x_ref[:, :PACKED]).astype(jnp.bfloat16)
        gates = pltpu.bitcast(x_ref[:, PACKED:], jnp.float32)

        def mlp(wg, wu, wd):
            g = jnp.dot(x, wg[...], preferred_element_type=jnp.float32)
            u = jnp.dot(x, wu[...], preferred_element_type=jnp.float32)
            hmid = (g * jax.nn.sigmoid(g) * u).astype(jnp.bfloat16)
            return jnp.dot(hmid, wd[...], preferred_element_type=jnp.float32)

        o_ref[...] = _pack_rows(gates[:, 0:1] * mlp(wga_bf, wua_bf, wda_bf)
                                + gates[:, 1:2] * mlp(wgb_bf, wub_bf, wdb_bf))

    @pl.when(tv_ref[i] == 0)
    def _():
        o_ref[...] = jnp.zeros_like(o_ref)


def expert_mlp(x_sorted, tables, layer, w_gate, w_up, w_down):
    p = x_sorted.shape[0]
    tm = EXPERT_TILE

    def weight_spec(shape, slot):
        return pl.BlockSpec((1, 1) + shape, lambda i, ea, eb, na, nb, tr, tv: (layer, (ea, eb)[slot][i], 0, 0))

    up_shape, down_shape = (D_MODEL, EXPERT_FF), (EXPERT_FF, D_MODEL)
    grid_spec = pltpu.PrefetchScalarGridSpec(
        num_scalar_prefetch=6,
        grid=(p // tm,),
        in_specs=[pl.BlockSpec((tm, ROW_WORDS), lambda i, ea, eb, na, nb, tr, tv: (tr[i], 0)),
                  weight_spec(up_shape, 0), weight_spec(up_shape, 0), weight_spec(down_shape, 0),
                  weight_spec(up_shape, 1), weight_spec(up_shape, 1), weight_spec(down_shape, 1)],
        out_specs=pl.BlockSpec((tm, PACKED), lambda i, ea, eb, na, nb, tr, tv: (i, 0)),
        scratch_shapes=[pltpu.VMEM(up_shape, jnp.bfloat16), pltpu.VMEM(up_shape, jnp.bfloat16),
                        pltpu.VMEM(down_shape, jnp.bfloat16)] * 2,
    )
    return pl.pallas_call(
        _expert_kernel, grid_spec=grid_spec,
        out_shape=jax.ShapeDtypeStruct((p, PACKED), jnp.uint32),
        compiler_params=_cparams("arbitrary"), name="expert_mlp",
    )(*tables, x_sorted, w_gate, w_up, w_down, w_gate, w_up, w_down)


def _combine_kernel(dcur_ref, dnext_ref, h_ref, gain_ref, y_ref, o_ref, y_buf, sem, *, final_norm):
    i = pl.program_id(0)
    last = pl.num_programs(0) - 1
    t = h_ref.shape[0]
    slot = i % 2

    def row_copy(dest_ref, s, j):
        return pltpu.make_async_copy(y_ref.at[pl.ds(dest_ref[0, 0, j], 1)],
                                     y_buf.at[s, pl.ds(j, 1)], sem.at[s])

    def start_all(dest_ref, s):
        for j in range(t):
            row_copy(dest_ref, s, j).start(priority=j % 2)

    @pl.when(i == 0)
    def _():
        start_all(dcur_ref, 0)

    for s in range(2):
        @pl.when((i < last) & (slot == 1 - s))
        def _():
            start_all(dnext_ref, s)

    lax.fori_loop(0, t, lambda j, c: (row_copy(dcur_ref, slot, 0).wait(), c)[1], 0, unroll=DMA_UNROLL)
    out = h_ref[...] + _unpack_rows(y_buf[slot])
    if final_norm:
        out = _rms(out, gain_ref[...])
    o_ref[...] = out


def moe_combine(h, y, dest, final_gain):
    n = h.shape[0]
    t = COMBINE_TILE
    nt = n // t
    dest3 = dest.reshape(nt, 1, t)
    final_norm = final_gain is not None
    gain = final_gain if final_norm else jnp.ones((D_MODEL,), jnp.float32)
    return pl.pallas_call(
        functools.partial(_combine_kernel, final_norm=final_norm), grid=(nt,),
        in_specs=[pl.BlockSpec((1, 1, t), lambda i: (i, 0, 0), memory_space=pltpu.SMEM),
                  pl.BlockSpec((1, 1, t), lambda i: (jnp.minimum(i + 1, nt - 1), 0, 0),
                               memory_space=pltpu.SMEM),
                  pl.BlockSpec((t, D_MODEL), lambda i: (i, 0)),
                  pl.BlockSpec((1, D_MODEL), lambda i: (0, 0)),
                  pl.BlockSpec(memory_space=pl.ANY)],
        out_specs=pl.BlockSpec((t, D_MODEL), lambda i: (i, 0)),
        out_shape=jax.ShapeDtypeStruct((n, D_MODEL), jnp.float32),
        scratch_shapes=[pltpu.VMEM((2, t, PACKED), jnp.uint32), pltpu.SemaphoreType.DMA((2,))],
        compiler_params=_cparams("arbitrary"), name="moe_combine",
    )(dest3, dest3, h, gain.reshape(1, D_MODEL), y)


def hierarchical_moe(h, gain, w_route, b_route, layer, w_gate, w_up, w_down, final_gain=None):
    n = h.shape[0]
    tm = EXPERT_TILE
    rows, route, counts = moe_router(h, gain, w_route, b_route)
    counts = counts[0, :N_CLASSES].astype(jnp.int32)
    padded = ((counts + tm - 1) // tm) * tm
    seg_end = jnp.cumsum(padded)
    seg_start = seg_end - padded
    cls = route[:, R_CLASS].astype(jnp.int32)
    rank = route[:, R_RANK].astype(jnp.int32)
    class_ids = jnp.arange(N_CLASSES, dtype=jnp.int32)
    dest = jnp.sum(jnp.where(cls[:, None] == class_ids[None, :], seg_start[None, :], 0), axis=-1) + rank
    n_rows = n + N_CLASSES * tm
    n_tiles = n_rows // tm
    used = seg_end[-1] // tm
    tile_id = jnp.arange(n_tiles, dtype=jnp.int32)
    tile_row = jnp.minimum(tile_id, used - 1).astype(jnp.int32)
    tile_class = jnp.minimum(jnp.sum((seg_end[None, :] <= (tile_row * tm)[:, None]).astype(jnp.int32), axis=1),
                             N_CLASSES - 1)
    tile_valid = (tile_id < used).astype(jnp.int32)
    slots = np.asarray([[g * EXPERTS_PER_GROUP + s for s in pair]
                        for g in range(N_EXPERT_GROUPS) for pair in PAIR_SLOTS], np.int32)
    tile_onehot = (tile_class[:, None] == class_ids[None, :]).astype(jnp.int32)
    expert_a = jnp.sum(tile_onehot * slots[None, :, 0], axis=1).astype(jnp.int32)
    expert_b = jnp.sum(tile_onehot * slots[None, :, 1], axis=1).astype(jnp.int32)

    def changed(e):
        return jnp.concatenate([jnp.ones((1,), jnp.int32), (e[1:] != e[:-1]).astype(jnp.int32)])

    tables = (expert_a, expert_b, changed(expert_a), changed(expert_b), tile_row, tile_valid)
    x_sorted = moe_dispatch(rows, dest, n_rows)
    y = expert_mlp(x_sorted, tables, layer, w_gate, w_up, w_down)
    return moe_combine(h, y, dest, final_gain)


def _permute_qk_columns(w_qkv):
    w = w_qkv.astype(jnp.bfloat16).reshape(D_MODEL, N_ATTN_GROUPS, 3, N_HEADS, 2, HALF_DIM)
    qk = jnp.transpose(w[:, :, 0:2], (0, 1, 2, 4, 3, 5)).reshape(D_MODEL, N_ATTN_GROUPS, 2, D_MODEL)
    v = w[:, :, 2:3].reshape(D_MODEL, N_ATTN_GROUPS, 1, D_MODEL)
    return jnp.concatenate([qk, v], axis=2).reshape(D_MODEL, QKV_WIDTH)


def kernel(x, positions, norm_mix, norm_ffn, norm_final, conv_w_in, conv_w, conv_w_out,
           attn_w_qkv, attn_w_o, w_coarse, b_coarse, w_fine, b_fine, w_gate, w_up, w_down):
    batch, seq, d = x.shape
    assert d == D_MODEL
    n = batch * seq
    depth = norm_mix.shape[0]
    bf = jnp.bfloat16
    h = x.reshape(n, d)
    cos, sin = rope_tables(positions.reshape(n))
    for i in range(depth):
        j = i // 2
        if i % 2 == 0:
            h = conv_mixer(h, batch, norm_mix[i], conv_w_in[j].astype(bf), conv_w[j], conv_w_out[j].astype(bf))
        else:
            w_qkv = _permute_qk_columns(attn_w_qkv[j])
            outs, maxima, dens = [], [], []
            for g, (window, dil) in enumerate(DILATED_GROUPS):
                assert window // dil == ATT_BLK
                w_group = w_qkv[:, g * 3 * D_MODEL:(g + 1) * 3 * D_MODEL]
                o, mx, den = dilated_attention_group(
                    qkv_project(h, batch, norm_mix[i], w_group, cos, sin, dil), dil)
                outs.append(o)
                maxima.append(mx)
                dens.append(den)
            h = merge_proj_residual(h, batch, outs, maxima, dens, attn_w_o[j].astype(bf))
        w_route = jnp.concatenate(
            [w_coarse[i], jnp.transpose(w_fine[i], (1, 0, 2)).reshape(d, N_EXPERTS),
             jnp.zeros((d, LANES - N_EXPERT_GROUPS - N_EXPERTS), jnp.float32)], axis=1)
        b_route = jnp.concatenate(
            [b_coarse[i], b_fine[i].reshape(-1),
             jnp.zeros((LANES - N_EXPERT_GROUPS - N_EXPERTS,), jnp.float32)]).reshape(1, LANES)
        h = hierarchical_moe(h, norm_ffn[i], w_route, b_route,
                             i, w_gate, w_up, w_down,
                             norm_final if i == depth - 1 else None)
    return h.reshape(batch, seq, d)
```

```python
import functools
import math

import numpy as np
import jax
import jax.numpy as jnp
from jax import lax
from jax.experimental import pallas as pl
from jax.experimental.pallas import tpu as pltpu

D_MODEL = 1024
NORM_EPS = 1e-6
CONV_WIDTH = 3
HEAD_DIM = 64
HALF_DIM = HEAD_DIM // 2
N_HEADS = D_MODEL // HEAD_DIM
DILATED_GROUPS = ((128, 1), (512, 4), (2048, 16))
N_ATTN_GROUPS = len(DILATED_GROUPS)
QKV_WIDTH = N_ATTN_GROUPS * 3 * D_MODEL
ROPE_THETA = 10000.0
N_EXPERT_GROUPS = 4
EXPERTS_PER_GROUP = 4
N_EXPERTS = N_EXPERT_GROUPS * EXPERTS_PER_GROUP
EXPERT_FF = D_MODEL // 2
PACKED = D_MODEL // 2

LANES = 128
ATT_BLK = 128
ATT_BLOCKS_PER_STEP = 8
VMEM_LIMIT_BYTES = 56 * 1024 * 1024
ROW_TILE = 1024
PROJ_TILE = 512
CONV_TILE = 512
EXPERT_TILE = 256
DISPATCH_TILE = 1024
COMBINE_TILE = 512
RANK_BLOCK = 256
DMA_UNROLL = 8
NEG_INF = float("-inf")
LOG2_E = math.log2(math.e)
R_CLASS, R_RANK = range(2)
PAIR_SLOTS = ((0, 1), (0, 2), (0, 3), (1, 3), (1, 2), (3, 2))
N_CLASSES = N_EXPERT_GROUPS * len(PAIR_SLOTS)
ROW_WORDS = PACKED + LANES


def _cparams(*sem):
    return pltpu.CompilerParams(dimension_semantics=sem, vmem_limit_bytes=VMEM_LIMIT_BYTES)


def _rms(x, g):
    return x * lax.rsqrt(jnp.mean(x * x, axis=-1, keepdims=True) + NORM_EPS) * g


def _pack_rows(x):
    lo = pltpu.bitcast(x[:, :PACKED].astype(jnp.bfloat16).astype(jnp.float32), jnp.uint32)
    hi = pltpu.bitcast(x[:, PACKED:].astype(jnp.bfloat16).astype(jnp.float32), jnp.uint32)
    return (lo >> 16) | (hi & jnp.uint32(0xFFFF0000))


def _unpack_rows(u):
    lo = pltpu.bitcast(u << 16, jnp.float32)
    hi = pltpu.bitcast(u & jnp.uint32(0xFFFF0000), jnp.float32)
    return jnp.concatenate([lo, hi], axis=1)


def _rope_kernel(pos_ref, freq_ref, cos_ref, sin_ref):
    ang = pos_ref[...].astype(jnp.float32) * freq_ref[...]
    cos_ref[...] = jnp.cos(ang)
    sin_ref[...] = jnp.sin(ang)


def rope_tables(positions):
    n = positions.shape[0]
    inv_freq = (ROPE_THETA ** (-np.arange(0, HEAD_DIM, 2, dtype=np.float32) / HEAD_DIM)).astype(np.float32)
    freq = jnp.asarray(np.tile(inv_freq, LANES // HALF_DIM)[None, :])
    tr = ROW_TILE
    return pl.pallas_call(
        _rope_kernel,
        grid=(n // tr,),
        in_specs=[pl.BlockSpec((tr, 1), lambda i: (i, 0)),
                  pl.BlockSpec((1, LANES), lambda i: (0, 0))],
        out_specs=[pl.BlockSpec((tr, LANES), lambda i: (i, 0)),
                   pl.BlockSpec((tr, LANES), lambda i: (i, 0))],
        out_shape=[jax.ShapeDtypeStruct((n, LANES), jnp.float32)] * 2,
        compiler_params=_cparams("parallel"),
        name="rope_tables",
    )(positions.reshape(n, 1), freq)


def _conv_mixer_kernel(h_ref, g_ref, win_ref, cw_ref, wout_ref, o_ref, hn_buf, u_buf, v_buf):
    t = h_ref.shape[0]

    @pl.when(pl.program_id(1) == 0)
    def _():
        u_buf[0:8, :] = jnp.zeros((8, D_MODEL), jnp.float32)

    x = h_ref[...]
    hn_buf[...] = _rms(x, g_ref[...]).astype(jnp.bfloat16)
    cw = cw_ref[...]
    chunk = 512
    for c in range(0, D_MODEL, chunk):
        hn = hn_buf[...]
        gate_c = jnp.dot(hn, win_ref[:, D_MODEL + c:D_MODEL + c + chunk], preferred_element_type=jnp.float32)
        hh = jnp.dot(hn, win_ref[:, 2 * D_MODEL + c:2 * D_MODEL + c + chunk], preferred_element_type=jnp.float32)
        u_buf[8:8 + t, c:c + chunk] = gate_c * hh
        conv = (cw[0:1, c:c + chunk] * u_buf[6:6 + t, c:c + chunk]
                + cw[1:2, c:c + chunk] * u_buf[7:7 + t, c:c + chunk]
                + cw[2:3, c:c + chunk] * u_buf[8:8 + t, c:c + chunk])
        gate_b = jnp.dot(hn, win_ref[:, c:c + chunk], preferred_element_type=jnp.float32)
        v_buf[:, c:c + chunk] = (gate_b * conv).astype(jnp.bfloat16)
    u_buf[0:8, :] = u_buf[t:t + 8, :]
    o_ref[...] = x + jnp.dot(v_buf[...], wout_ref[...], preferred_element_type=jnp.float32)


def conv_mixer(h, batch, gain, w_in, conv_w, w_out):
    n = h.shape[0]
    seq = n // batch
    t = CONV_TILE
    ns = seq // t
    row = pl.BlockSpec((t, D_MODEL), lambda b, s: (b * ns + s, 0))
    return pl.pallas_call(
        _conv_mixer_kernel,
        grid=(batch, ns),
        in_specs=[row,
                  pl.BlockSpec((1, D_MODEL), lambda b, s: (0, 0)),
                  pl.BlockSpec((D_MODEL, 3 * D_MODEL), lambda b, s: (0, 0)),
                  pl.BlockSpec((CONV_WIDTH, D_MODEL), lambda b, s: (0, 0)),
                  pl.BlockSpec((D_MODEL, D_MODEL), lambda b, s: (0, 0))],
        out_specs=row,
        out_shape=jax.ShapeDtypeStruct((n, D_MODEL), jnp.float32),
        scratch_shapes=[pltpu.VMEM((t, D_MODEL), jnp.bfloat16),
                        pltpu.VMEM((t + 8, D_MODEL), jnp.float32),
                        pltpu.VMEM((t, D_MODEL), jnp.bfloat16)],
        compiler_params=_cparams("arbitrary", "arbitrary"),
        name="conv_mixer",
    )(h, gain.reshape(1, D_MODEL), w_in, conv_w, w_out)


def _qkv_kernel(h_ref, g_ref, w_ref, cos_ref, sin_ref, o_ref, z_buf, *, dil):
    t = h_ref.shape[0]
    half = D_MODEL // 2
    reps = half // LANES
    rows = t // dil

    def residue_major(x):
        if dil == 1:
            return x
        planes = x.shape[1] // LANES
        for c in range(planes):
            z_buf[c] = x[:, c * LANES:(c + 1) * LANES]
        return jnp.concatenate(
            [jnp.concatenate([z_buf[c, pl.ds(r, rows, stride=dil), :] for r in range(dil)], axis=0)
             for c in range(planes)], axis=1)

    cos128 = residue_major(cos_ref[...])
    sin128 = residue_major(sin_ref[...])
    cos = jnp.concatenate([cos128] * reps, axis=1)
    sin = jnp.concatenate([sin128] * reps, axis=1)
    hn = residue_major(_rms(h_ref[...], g_ref[...])).astype(jnp.bfloat16)
    for part in range(3):
        z = jnp.dot(hn, w_ref[:, part * D_MODEL:(part + 1) * D_MODEL], preferred_element_type=jnp.float32)
        if part < 2:
            z1 = z[:, :half]
            z2 = z[:, half:]
            z = jnp.concatenate([z1 * cos - z2 * sin, z2 * cos + z1 * sin], axis=1)
        if part == 0:
            z = z * (HEAD_DIM ** -0.5 * LOG2_E)
        z = z.astype(o_ref.dtype)
        for r in range(dil):
            o_ref[0, r, :, part * D_MODEL:(part + 1) * D_MODEL] = z[r * rows:(r + 1) * rows]


def qkv_project(h, batch, gain, w_group, cos, sin, dil):
    n = h.shape[0]
    seq = n // batch
    t = PROJ_TILE
    ns = seq // t
    row = lambda width: pl.BlockSpec((t, width), lambda b, s: (b * ns + s, 0))
    return pl.pallas_call(
        functools.partial(_qkv_kernel, dil=dil),
        grid=(batch, ns),
        in_specs=[row(D_MODEL),
                  pl.BlockSpec((1, D_MODEL), lambda b, s: (0, 0)),
                  pl.BlockSpec((D_MODEL, 3 * D_MODEL), lambda b, s: (0, 0)),
                  row(LANES), row(LANES)],
        out_specs=pl.BlockSpec((1, dil, t // dil, 3 * D_MODEL), lambda b, s: (b, 0, s, 0)),
        out_shape=jax.ShapeDtypeStruct((batch, dil, seq // dil, 3 * D_MODEL), jnp.bfloat16),
        scratch_shapes=[pltpu.VMEM((D_MODEL // LANES, t, LANES), jnp.float32)],
        compiler_params=_cparams("parallel", "parallel"),
        name=f"qkv_project_d{dil}",
    )(h, gain.reshape(1, D_MODEL), w_group, cos, sin)


def _attention_kernel(q_ref, k_ref, v_ref, o_ref, m_ref, d_ref, k_win, v_win, bias_buf):
    blk = ATT_BLK
    n = pl.program_id(2)
    n_res = q_ref.shape[1]
    n_sub = q_ref.shape[2] // blk
    heads_per_slab = LANES // HALF_DIM
    n_slabs = N_HEADS // heads_per_slab
    rows = heads_per_slab * blk

    @pl.when(n == 0)
    def _():
        for res in range(n_res):
            k_win[res, 0:blk, :] = jnp.zeros((blk, D_MODEL), k_win.dtype)
            v_win[res, 0:blk, :] = jnp.zeros((blk, D_MODEL), v_win.dtype)

    for res in range(n_res):
        k_win[res, blk:(n_sub + 1) * blk, :] = k_ref[0, res]
        v_win[res, blk:(n_sub + 1) * blk, :] = v_ref[0, res]

    @pl.when((pl.program_id(0) == 0) & (pl.program_id(1) == 0) & (n == 0))
    def _():
        qi = lax.broadcasted_iota(jnp.int32, (rows, 2 * blk), 0) % blk
        ki = lax.broadcasted_iota(jnp.int32, (rows, 2 * blk), 1)
        own_ok = (ki >= blk) & (ki - blk <= qi)
        back_ok = (ki < blk) & (ki >= qi)
        bias_buf[0] = jnp.where(own_ok, 0.0, NEG_INF).astype(jnp.float32)
        bias_buf[1] = jnp.where(own_ok | back_ok, 0.0, NEG_INF).astype(jnp.float32)

    first_plane = jnp.minimum(n, 1)

    half = D_MODEL // 2
    lane2 = lax.broadcasted_iota(jnp.int32, (blk, 2 * LANES), 1)
    lane1 = lax.broadcasted_iota(jnp.int32, (blk, LANES), 1)

    def scores(unit):
        res, sub, m = unit
        q_rows = slice(sub * blk, (sub + 1) * blk)
        k_rows = slice(sub * blk, (sub + 2) * blk)
        q_slab = jnp.concatenate([q_ref[0, res, q_rows, m * LANES:(m + 1) * LANES],
                                  q_ref[0, res, q_rows, half + m * LANES:half + (m + 1) * LANES]], axis=1)
        k_slab = jnp.concatenate([k_win[res, k_rows, m * LANES:(m + 1) * LANES],
                                  k_win[res, k_rows, half + m * LANES:half + (m + 1) * LANES]], axis=1)
        zero = jnp.zeros_like(q_slab)
        q_heads = jnp.concatenate(
            [jnp.where(((lane2 % LANES) // HALF_DIM) == a, q_slab, zero) for a in range(heads_per_slab)], axis=0)
        bias = bias_buf[first_plane] if sub == 0 else bias_buf[1]
        return lax.dot_general(q_heads, k_slab, (((1,), (1,)), ((), ())),
                               preferred_element_type=jnp.float32) + bias

    units = [(res, sub, m) for res in range(n_res) for sub in range(n_sub) for m in range(n_slabs)]
    m_ref[0] = jnp.zeros(m_ref.shape[1:], jnp.float32)
    d_ref[0] = jnp.ones(d_ref.shape[1:], jnp.float32)
    s_next = scores(units[0])
    for idx, (res, sub, m) in enumerate(units):
        s = s_next
        if idx + 1 < len(units):
            s_next = scores(units[idx + 1])
        mx = jnp.max(s, axis=-1, keepdims=True)
        p = jnp.exp2(s - mx)
        den = jnp.sum(p, axis=-1, keepdims=True)
        pb = p.astype(jnp.bfloat16)
        q_rows = slice(sub * blk, (sub + 1) * blk)
        k_rows = slice(sub * blk, (sub + 2) * blk)
        for a in range(heads_per_slab):
            hd = m * heads_per_slab + a
            m_ref[0, res, q_rows, hd:hd + 1] = mx[a * blk:(a + 1) * blk]
            d_ref[0, res, q_rows, hd:hd + 1] = den[a * blk:(a + 1) * blk]
        for pair in range(heads_per_slab // 2):
            col = (m * heads_per_slab + pair * 2) * HEAD_DIM
            r0 = pair * 2 * blk
            o2 = jnp.dot(pb[r0:r0 + 2 * blk], v_win[res, k_rows, col:col + LANES],
                         preferred_element_type=jnp.float32)
            o_ref[0, res, q_rows, col:col + LANES] = jnp.where(
                lane1 < HEAD_DIM, o2[:blk], o2[blk:]).astype(o_ref.dtype)
    for res in range(n_res):
        k_win[res, 0:blk, :] = k_win[res, n_sub * blk:(n_sub + 1) * blk, :]
        v_win[res, 0:blk, :] = v_win[res, n_sub * blk:(n_sub + 1) * blk, :]


def dilated_attention_group(qkv_g, dil):
    batch, _, length, _ = qkv_g.shape
    blk = ATT_BLK
    rows = min(ATT_BLOCKS_PER_STEP * blk, length)
    nstep = length // rows
    n_res = min(dil, max(1, ATT_BLOCKS_PER_STEP * blk // length))
    assert dil % n_res == 0

    def part_spec(part):
        return pl.BlockSpec((1, n_res, rows, D_MODEL), lambda b, r, i: (b, r, i, part))

    return pl.pallas_call(
        _attention_kernel,
        grid=(batch, dil // n_res, nstep),
        in_specs=[part_spec(0), part_spec(1), part_spec(2)],
        out_specs=[pl.BlockSpec((1, n_res, rows, D_MODEL), lambda b, r, i: (b, r, i, 0)),
                   pl.BlockSpec((1, n_res, rows, LANES), lambda b, r, i: (b, r, i, 0)),
                   pl.BlockSpec((1, n_res, rows, LANES), lambda b, r, i: (b, r, i, 0))],
        out_shape=[jax.ShapeDtypeStruct((batch, dil, length, D_MODEL), jnp.bfloat16),
                   jax.ShapeDtypeStruct((batch, dil, length, LANES), jnp.float32),
                   jax.ShapeDtypeStruct((batch, dil, length, LANES), jnp.float32)],
        scratch_shapes=[pltpu.VMEM((n_res, rows + blk, D_MODEL), jnp.bfloat16),
                        pltpu.VMEM((n_res, rows + blk, D_MODEL), jnp.bfloat16),
                        pltpu.VMEM((2, (LANES // HALF_DIM) * blk, 2 * blk), jnp.float32)],
        compiler_params=_cparams("arbitrary", "arbitrary", "arbitrary"),
        name=f"dilated_attention_d{dil}",
    )(qkv_g, qkv_g, qkv_g)


def _merge_proj_kernel(*refs, dils):
    ng = len(dils)
    h_ref = refs[0]
    o_refs = refs[1:1 + ng]
    m_refs = refs[1 + ng:1 + 2 * ng]
    d_refs = refs[1 + 2 * ng:1 + 3 * ng]
    e_ref, w_ref, out_ref, o_buf, l_buf = refs[1 + 3 * ng:]
    t = h_ref.shape[0]

    def token_major(ref, dil):
        if dil == 1:
            return ref[0, 0]
        for r in range(dil):
            l_buf[pl.ds(r, t // dil, stride=dil), :] = ref[0, r]
        return l_buf[...]

    maxima = [token_major(m_refs[g], dil) for g, dil in enumerate(dils)]
    lses = [maxima[g] + jnp.log2(token_major(d_refs[g], dil)) for g, dil in enumerate(dils)]
    top = functools.reduce(jnp.maximum, lses)
    inv = 1.0 / functools.reduce(jnp.add, [jnp.exp2(l - top) for l in lses])
    merged = jnp.zeros((t, D_MODEL), jnp.float32)
    for g, dil in enumerate(dils):
        wg = jnp.exp2(maxima[g] - top) * inv
        hi = wg.astype(jnp.bfloat16)
        lo = (wg - hi.astype(jnp.float32)).astype(jnp.bfloat16)
        wexp = jnp.dot(jnp.concatenate([hi, lo], axis=1), e_ref[...], preferred_element_type=jnp.float32)
        if dil == 1:
            og = o_refs[g][0, 0].astype(jnp.float32)
        else:
            for r in range(dil):
                part = o_refs[g][0, r].astype(jnp.float32)
                for c in range(D_MODEL // LANES):
                    o_buf[c, pl.ds(r, t // dil, stride=dil), :] = part[:, c * LANES:(c + 1) * LANES]
            og = jnp.concatenate([o_buf[c] for c in range(D_MODEL // LANES)], axis=1)
        merged = merged + wexp * og
    out_ref[...] = h_ref[...] + jnp.dot(merged.astype(jnp.bfloat16), w_ref[...],
                                        preferred_element_type=jnp.float32)


def merge_proj_residual(h, batch, outs, maxima, dens, w_o):
    n = h.shape[0]
    seq = n // batch
    t = PROJ_TILE
    ns = seq // t
    dils = tuple(o.shape[1] for o in outs)
    row = pl.BlockSpec((t, D_MODEL), lambda b, s: (b * ns + s, 0))
    expand = np.zeros((2 * LANES, D_MODEL), np.float32)
    for hd in range(N_HEADS):
        expand[hd, hd * HEAD_DIM:(hd + 1) * HEAD_DIM] = 1.0
        expand[LANES + hd, hd * HEAD_DIM:(hd + 1) * HEAD_DIM] = 1.0
    in_specs = [row]
    in_specs += [pl.BlockSpec((1, d, t // d, D_MODEL), lambda b, s: (b, 0, s, 0)) for d in dils]
    in_specs += [pl.BlockSpec((1, d, t // d, LANES), lambda b, s: (b, 0, s, 0)) for d in dils] * 2
    in_specs += [pl.BlockSpec((2 * LANES, D_MODEL), lambda b, s: (0, 0)),
                 pl.BlockSpec((D_MODEL, D_MODEL), lambda b, s: (0, 0))]
    return pl.pallas_call(
        functools.partial(_merge_proj_kernel, dils=dils),
        grid=(batch, ns), in_specs=in_specs, out_specs=row,
        out_shape=jax.ShapeDtypeStruct((n, D_MODEL), jnp.float32),
        scratch_shapes=[pltpu.VMEM((D_MODEL // LANES, t, LANES), jnp.float32),
                        pltpu.VMEM((t, LANES), jnp.float32)],
        compiler_params=_cparams("parallel", "parallel"), name="attn_merge_out_proj",
    )(h, *outs, *maxima, *dens, jnp.asarray(expand, jnp.bfloat16), w_o)


def _router_kernel(h_ref, g_ref, w_ref, b_ref, tri_ref, hn_ref, route_ref, count_ref, run_ref):
    @pl.when(pl.program_id(0) == 0)
    def _():
        run_ref[...] = jnp.zeros_like(run_ref)

    hn = _rms(h_ref[...], g_ref[...])
    hn_ref[:, :PACKED] = _pack_rows(hn)
    hn_hi = hn.astype(jnp.bfloat16)
    hn_lo = (hn - hn_hi.astype(jnp.float32)).astype(jnp.bfloat16)
    both = jnp.dot(hn_hi, w_ref[...], preferred_element_type=jnp.float32)
    logits = (both[:, :LANES] + both[:, LANES:]
              + jnp.dot(hn_lo, w_ref[:, :LANES], preferred_element_type=jnp.float32) + b_ref[...])
    lane = lax.broadcasted_iota(jnp.int32, logits.shape, 1)
    big = jnp.int32(LANES)

    def first_argmax(vals, vmax):
        return jnp.min(jnp.where(vals == vmax, lane, big), axis=-1, keepdims=True)

    coarse = jnp.where(lane < N_EXPERT_GROUPS, logits, NEG_INF)
    cmax = jnp.max(coarse, axis=-1, keepdims=True)
    p_top = 1.0 / jnp.sum(jnp.exp(coarse - cmax), axis=-1, keepdims=True)
    g_top = first_argmax(coarse, cmax)
    lo = N_EXPERT_GROUPS + g_top * EXPERTS_PER_GROUP
    fine = jnp.where((lane >= lo) & (lane < lo + EXPERTS_PER_GROUP), logits, NEG_INF)
    v1 = jnp.max(fine, axis=-1, keepdims=True)
    i1 = first_argmax(fine, v1)
    fine2 = jnp.where(lane == i1, NEG_INF, fine)
    v2 = jnp.max(fine2, axis=-1, keepdims=True)
    i2 = first_argmax(fine2, v2)
    e2 = jnp.exp(v2 - v1)
    w1 = 1.0 / (1.0 + e2)
    w2 = e2 / (1.0 + e2)
    first_lo = i1 < i2
    a = jnp.where(first_lo, i1, i2) - lo
    b = jnp.where(first_lo, i2, i1) - lo
    pair = jnp.where(a == 0, b - 1, jnp.where(a == 1, jnp.where(b == 3, 3, 4), 5))
    cls = g_top * len(PAIR_SLOTS) + pair
    gate_lo = p_top * jnp.where(first_lo, w1, w2)
    gate_hi = p_top * jnp.where(first_lo, w2, w1)
    swapped = pair == 5
    gate_a = jnp.where(swapped, gate_hi, gate_lo)
    gate_b = jnp.where(swapped, gate_lo, gate_hi)
    gates = jnp.where(lane == 0, gate_a, jnp.where(lane == 1, gate_b, 0.0))
    hn_ref[:, PACKED:] = pltpu.bitcast(gates, jnp.uint32)

    sel = lane == cls
    onehot = jnp.where(sel, 1.0, 0.0)
    sub = tri_ref.shape[0]
    run = run_ref[...]
    befores = []
    for r0 in range(0, onehot.shape[0], sub):
        oh = onehot[r0:r0 + sub]
        befores.append(jnp.dot(tri_ref[...], oh.astype(jnp.bfloat16), preferred_element_type=jnp.float32) + run)
        run = run + jnp.sum(oh, axis=0, keepdims=True)
    before = jnp.concatenate(befores, axis=0)
    rank = jnp.sum(jnp.where(sel, before, 0.0), axis=-1, keepdims=True)
    run_ref[...] = run
    count_ref[...] = run

    route = jnp.where(lane == R_CLASS, cls.astype(jnp.float32), 0.0)
    route = jnp.where(lane == R_RANK, rank, route)
    route_ref[...] = route


def moe_router(h, gain, w_route, b_route):
    n = h.shape[0]
    tr = ROW_TILE
    sub = RANK_BLOCK
    tri = jnp.asarray(np.tril(np.ones((sub, sub), np.float32), -1), jnp.bfloat16)
    w_hi = w_route.astype(jnp.bfloat16)
    w_lo = (w_route - w_hi.astype(jnp.float32)).astype(jnp.bfloat16)
    w_split = jnp.concatenate([w_hi, w_lo], axis=1)
    return pl.pallas_call(
        _router_kernel, grid=(n // tr,),
        in_specs=[pl.BlockSpec((tr, D_MODEL), lambda i: (i, 0)),
                  pl.BlockSpec((1, D_MODEL), lambda i: (0, 0)),
                  pl.BlockSpec((D_MODEL, 2 * LANES), lambda i: (0, 0)),
                  pl.BlockSpec((1, LANES), lambda i: (0, 0)),
                  pl.BlockSpec((sub, sub), lambda i: (0, 0))],
        out_specs=[pl.BlockSpec((tr, ROW_WORDS), lambda i: (i, 0)),
                   pl.BlockSpec((tr, LANES), lambda i: (i, 0)),
                   pl.BlockSpec((1, LANES), lambda i: (0, 0))],
        out_shape=[jax.ShapeDtypeStruct((n, ROW_WORDS), jnp.uint32),
                   jax.ShapeDtypeStruct((n, LANES), jnp.float32),
                   jax.ShapeDtypeStruct((1, LANES), jnp.float32)],
        scratch_shapes=[pltpu.VMEM((1, LANES), jnp.float32)],
        compiler_params=_cparams("arbitrary"), name="moe_router",
    )(h, gain.reshape(1, D_MODEL), w_split, b_route, tri)


def _dispatch_kernel(dest_ref, x_ref, init_ref, o_ref, sem):
    del init_ref
    t = x_ref.shape[0]

    def row_copy(j):
        return pltpu.make_async_copy(x_ref.at[pl.ds(j, 1)], o_ref.at[pl.ds(dest_ref[0, 0, j], 1)], sem)

    for j in range(t):
        row_copy(j).start(priority=j % 2)

    def wait(j, c):
        row_copy(0).wait()
        return c

    lax.fori_loop(0, t, wait, 0, unroll=DMA_UNROLL)


def moe_dispatch(rows, dest, n_rows):
    n = rows.shape[0]
    t = DISPATCH_TILE
    return pl.pallas_call(
        _dispatch_kernel, grid=(n // t,),
        in_specs=[pl.BlockSpec((1, 1, t), lambda i: (i, 0, 0), memory_space=pltpu.SMEM),
                  pl.BlockSpec((t, ROW_WORDS), lambda i: (i, 0)),
                  pl.BlockSpec(memory_space=pl.ANY)],
        out_specs=pl.BlockSpec(memory_space=pl.ANY),
        out_shape=jax.ShapeDtypeStruct((n_rows, ROW_WORDS), jnp.uint32),
        scratch_shapes=[pltpu.SemaphoreType.DMA(())],
        input_output_aliases={2: 0},
        compiler_params=_cparams("arbitrary"), name="moe_dispatch",
    )(dest.reshape(n // t, 1, t), rows, jnp.zeros((n_rows, ROW_WORDS), jnp.uint32))


def _expert_kernel(ea_ref, eb_ref, na_ref, nb_ref, tr_ref, tv_ref, x_ref,
                   wga_ref, wua_ref, wda_ref, wgb_ref, wub_ref, wdb_ref, o_ref,
                   wga_bf, wua_bf, wda_bf, wgb_bf, wub_bf, wdb_bf):
    del ea_ref, eb_ref, tr_ref
    i = pl.program_id(0)

    @pl.when(na_ref[i] > 0)
    def _():
        wga_bf[...] = wga_ref[0, 0].astype(jnp.bfloat16)
        wua_bf[...] = wua_ref[0, 0].astype(jnp.bfloat16)
        wda_bf[...] = wda_ref[0, 0].astype(jnp.bfloat16)

    @pl.when(nb_ref[i] > 0)
    def _():
        wgb_bf[...] = wgb_ref[0, 0].astype(jnp.bfloat16)
        wub_bf[...] = wub_ref[0, 0].astype(jnp.bfloat16)
        wdb_bf[...] = wdb_ref[0, 0].astype(jnp.bfloat16)

    @pl.when(tv_ref[i] > 0)
    def _():
        x = _unpack_rows(x_ref[:, :PACKED]).astype(jnp.bfloat16)
        gates = pltpu.bitcast(x_ref[:, PACKED:], jnp.float32)

        def mlp(wg, wu, wd):
            g = jnp.dot(x, wg[...], preferred_element_type=jnp.float32)
            u = jnp.dot(x, wu[...], preferred_element_type=jnp.float32)
            hmid = (g * jax.nn.sigmoid(g) * u).astype(jnp.bfloat16)
            return jnp.dot(hmid, wd[...], preferred_element_type=jnp.float32)

        o_ref[...] = _pack_rows(gates[:, 0:1] * mlp(wga_bf, wua_bf, wda_bf)
                                + gates[:, 1:2] * mlp(wgb_bf, wub_bf, wdb_bf))

    @pl.when(tv_ref[i] == 0)
    def _():
        o_ref[...] = jnp.zeros_like(o_ref)


def expert_mlp(x_sorted, tables, layer, w_gate, w_up, w_down):
    p = x_sorted.shape[0]
    tm = EXPERT_TILE

    def weight_spec(shape, slot):
        return pl.BlockSpec((1, 1) + shape, lambda i, ea, eb, na, nb, tr, tv: (layer, (ea, eb)[slot][i], 0, 0))

    up_shape, down_shape = (D_MODEL, EXPERT_FF), (EXPERT_FF, D_MODEL)
    grid_spec = pltpu.PrefetchScalarGridSpec(
        num_scalar_prefetch=6,
        grid=(p // tm,),
        in_specs=[pl.BlockSpec((tm, ROW_WORDS), lambda i, ea, eb, na, nb, tr, tv: (tr[i], 0)),
                  weight_spec(up_shape, 0), weight_spec(up_shape, 0), weight_spec(down_shape, 0),
                  weight_spec(up_shape, 1), weight_spec(up_shape, 1), weight_spec(down_shape, 1)],
        out_specs=pl.BlockSpec((tm, PACKED), lambda i, ea, eb, na, nb, tr, tv: (i, 0)),
        scratch_shapes=[pltpu.VMEM(up_shape, jnp.bfloat16), pltpu.VMEM(up_shape, jnp.bfloat16),
                        pltpu.VMEM(down_shape, jnp.bfloat16)] * 2,
    )
    return pl.pallas_call(
        _expert_kernel, grid_spec=grid_spec,
        out_shape=jax.ShapeDtypeStruct((p, PACKED), jnp.uint32),
        compiler_params=_cparams("arbitrary"), name="expert_mlp",
    )(*tables, x_sorted, w_gate, w_up, w_down, w_gate, w_up, w_down)


def _combine_kernel(dcur_ref, dnext_ref, h_ref, gain_ref, y_ref, o_ref, y_buf, sem, *, final_norm):
    i = pl.program_id(0)
    last = pl.num_programs(0) - 1
    t = h_ref.shape[0]
    slot = i % 2

    def row_copy(dest_ref, s, j):
        return pltpu.make_async_copy(y_ref.at[pl.ds(dest_ref[0, 0, j], 1)],
                                     y_buf.at[s, pl.ds(j, 1)], sem.at[s])

    def start_all(dest_ref, s):
        for j in range(t):
            row_copy(dest_ref, s, j).start(priority=j % 2)

    @pl.when(i == 0)
    def _():
        start_all(dcur_ref, 0)

    for s in range(2):
        @pl.when((i < last) & (slot == 1 - s))
        def _():
            start_all(dnext_ref, s)

    lax.fori_loop(0, t, lambda j, c: (row_copy(dcur_ref, slot, 0).wait(), c)[1], 0, unroll=DMA_UNROLL)
    out = h_ref[...] + _unpack_rows(y_buf[slot])
    if final_norm:
        out = _rms(out, gain_ref[...])
    o_ref[...] = out


def moe_combine(h, y, dest, final_gain):
    n = h.shape[0]
    t = COMBINE_TILE
    nt = n // t
    dest3 = dest.reshape(nt, 1, t)
    final_norm = final_gain is not None
    gain = final_gain if final_norm else jnp.ones((D_MODEL,), jnp.float32)
    return pl.pallas_call(
        functools.partial(_combine_kernel, final_norm=final_norm), grid=(nt,),
        in_specs=[pl.BlockSpec((1, 1, t), lambda i: (i, 0, 0), memory_space=pltpu.SMEM),
                  pl.BlockSpec((1, 1, t), lambda i: (jnp.minimum(i + 1, nt - 1), 0, 0),
                               memory_space=pltpu.SMEM),
                  pl.BlockSpec((t, D_MODEL), lambda i: (i, 0)),
                  pl.BlockSpec((1, D_MODEL), lambda i: (0, 0)),
                  pl.BlockSpec(memory_space=pl.ANY)],
        out_specs=pl.BlockSpec((t, D_MODEL), lambda i: (i, 0)),
        out_shape=jax.ShapeDtypeStruct((n, D_MODEL), jnp.float32),
        scratch_shapes=[pltpu.VMEM((2, t, PACKED), jnp.uint32), pltpu.SemaphoreType.DMA((2,))],
        compiler_params=_cparams("arbitrary"), name="moe_combine",
    )(dest3, dest3, h, gain.reshape(1, D_MODEL), y)


def hierarchical_moe(h, gain, w_route, b_route, layer, w_gate, w_up, w_down, final_gain=None):
    n = h.shape[0]
    tm = EXPERT_TILE
    rows, route, counts = moe_router(h, gain, w_route, b_route)
    counts = counts[0, :N_CLASSES].astype(jnp.int32)
    padded = ((counts + tm - 1) // tm) * tm
    seg_end = jnp.cumsum(padded)
    seg_start = seg_end - padded
    cls = route[:, R_CLASS].astype(jnp.int32)
    rank = route[:, R_RANK].astype(jnp.int32)
    class_ids = jnp.arange(N_CLASSES, dtype=jnp.int32)
    dest = jnp.sum(jnp.where(cls[:, None] == class_ids[None, :], seg_start[None, :], 0), axis=-1) + rank
    n_rows = n + N_CLASSES * tm
    n_tiles = n_rows // tm
    used = seg_end[-1] // tm
    tile_id = jnp.arange(n_tiles, dtype=jnp.int32)
    tile_row = jnp.minimum(tile_id, used - 1).astype(jnp.int32)
    tile_class = jnp.minimum(jnp.sum((seg_end[None, :] <= (tile_row * tm)[:, None]).astype(jnp.int32), axis=1),
                             N_CLASSES - 1)
    tile_valid = (tile_id < used).astype(jnp.int32)
    slots = np.asarray([[g * EXPERTS_PER_GROUP + s for s in pair]
                        for g in range(N_EXPERT_GROUPS) for pair in PAIR_SLOTS], np.int32)
    tile_onehot = (tile_class[:, None] == class_ids[None, :]).astype(jnp.int32)
    expert_a = jnp.sum(tile_onehot * slots[None, :, 0], axis=1).astype(jnp.int32)
    expert_b = jnp.sum(tile_onehot * slots[None, :, 1], axis=1).astype(jnp.int32)

    def changed(e):
        return jnp.concatenate([jnp.ones((1,), jnp.int32), (e[1:] != e[:-1]).astype(jnp.int32)])

    tables = (expert_a, expert_b, changed(expert_a), changed(expert_b), tile_row, tile_valid)
    x_sorted = moe_dispatch(rows, dest, n_rows)
    y = expert_mlp(x_sorted, tables, layer, w_gate, w_up, w_down)
    return moe_combine(h, y, dest, final_gain)


def _permute_qk_columns(w_qkv):
    w = w_qkv.astype(jnp.bfloat16).reshape(D_MODEL, N_ATTN_GROUPS, 3, N_HEADS, 2, HALF_DIM)
    qk = jnp.transpose(w[:, :, 0:2], (0, 1, 2, 4, 3, 5)).reshape(D_MODEL, N_ATTN_GROUPS, 2, D_MODEL)
    v = w[:, :, 2:3].reshape(D_MODEL, N_ATTN_GROUPS, 1, D_MODEL)
    return jnp.concatenate([qk, v], axis=2).reshape(D_MODEL, QKV_WIDTH)


def kernel(x, positions, norm_mix, norm_ffn, norm_final, conv_w_in, conv_w, conv_w_out,
           attn_w_qkv, attn_w_o, w_coarse, b_coarse, w_fine, b_fine, w_gate, w_up, w_down):
    batch, seq, d = x.shape
    assert d == D_MODEL
    n = batch * seq
    depth = norm_mix.shape[0]
    bf = jnp.bfloat16
    h = x.reshape(n, d)
    cos, sin = rope_tables(positions.reshape(n))
    for i in range(depth):
        j = i // 2
        if i % 2 == 0:
            h = conv_mixer(h, batch, norm_mix[i], conv_w_in[j].astype(bf), conv_w[j], conv_w_out[j].astype(bf))
        else:
            w_qkv = _permute_qk_columns(attn_w_qkv[j])
            outs, maxima, dens = [], [], []
            for g, (window, dil) in enumerate(DILATED_GROUPS):
                assert window // dil == ATT_BLK
                w_group = w_qkv[:, g * 3 * D_MODEL:(g + 1) * 3 * D_MODEL]
                o, mx, den = dilated_attention_group(
                    qkv_project(h, batch, norm_mix[i], w_group, cos, sin, dil), dil)
                outs.append(o)
                maxima.append(mx)
                dens.append(den)
            h = merge_proj_residual(h, batch, outs, maxima, dens, attn_w_o[j].astype(bf))
        w_route = jnp.concatenate(
            [w_coarse[i], jnp.transpose(w_fine[i], (1, 0, 2)).reshape(d, N_EXPERTS),
             jnp.zeros((d, LANES - N_EXPERT_GROUPS - N_EXPERTS), jnp.float32)], axis=1)
        b_route = jnp.concatenate(
            [b_coarse[i], b_fine[i].reshape(-1),
             jnp.zeros((LANES - N_EXPERT_GROUPS - N_EXPERTS,), jnp.float32)]).reshape(1, LANES)
        h = hierarchical_moe(h, norm_ffn[i], w_route, b_route,
                             i, w_gate, w_up, w_down,
                             norm_final if i == depth - 1 else None)
    return h.reshape(batch, seq, d)
```

```python
import functools
import math

import numpy as np
import jax
import jax.numpy as jnp
from jax import lax
from jax.experimental import pallas as pl
from jax.experimental.pallas import tpu as pltpu

D_MODEL = 1024
NORM_EPS = 1e-6
CONV_WIDTH = 3
HEAD_DIM = 64
HALF_DIM = HEAD_DIM // 2
N_HEADS = D_MODEL // HEAD_DIM
DILATED_GROUPS = ((128, 1), (512, 4), (2048, 16))
N_ATTN_GROUPS = len(DILATED_GROUPS)
QKV_WIDTH = N_ATTN_GROUPS * 3 * D_MODEL
ROPE_THETA = 10000.0
N_EXPERT_GROUPS = 4
EXPERTS_PER_GROUP = 4
N_EXPERTS = N_EXPERT_GROUPS * EXPERTS_PER_GROUP
EXPERT_FF = D_MODEL // 2
PACKED = D_MODEL // 2

LANES = 128
ATT_BLK = 128
ATT_BLOCKS_PER_STEP = 8
VMEM_LIMIT_BYTES = 56 * 1024 * 1024
ROW_TILE = 1024
QKV_TILE = 1024
PROJ_TILE = 512
CONV_TILE = 512
EXPERT_TILE = 512
DISPATCH_TILE = 1024
COMBINE_TILE = 512
RANK_BLOCK = 256
DMA_UNROLL = 8
NEG_INF = float("-inf")
LOG2_E = math.log2(math.e)
R_CLASS, R_RANK = range(2)
PAIR_SLOTS = ((0, 1), (0, 2), (0, 3), (1, 3), (1, 2), (3, 2))
N_CLASSES = N_EXPERT_GROUPS * len(PAIR_SLOTS)
ROW_WORDS = PACKED + LANES


def _cparams(*sem):
    return pltpu.CompilerParams(dimension_semantics=sem, vmem_limit_bytes=VMEM_LIMIT_BYTES)


def _rms(x, g):
    return x * lax.rsqrt(jnp.mean(x * x, axis=-1, keepdims=True) + NORM_EPS) * g


def _pack_rows(x):
    lo = pltpu.bitcast(x[:, :PACKED].astype(jnp.bfloat16).astype(jnp.float32), jnp.uint32)
    hi = pltpu.bitcast(x[:, PACKED:].astype(jnp.bfloat16).astype(jnp.float32), jnp.uint32)
    return (lo >> 16) | (hi & jnp.uint32(0xFFFF0000))


def _unpack_rows(u):
    lo = pltpu.bitcast(u << 16, jnp.float32)
    hi = pltpu.bitcast(u & jnp.uint32(0xFFFF0000), jnp.float32)
    return jnp.concatenate([lo, hi], axis=1)


def _rope_kernel(pos_ref, freq_ref, cos_ref, sin_ref):
    ang = pos_ref[...].astype(jnp.float32) * freq_ref[...]
    cos_ref[...] = jnp.cos(ang)
    sin_ref[...] = jnp.sin(ang)


def rope_tables(positions):
    n = positions.shape[0]
    inv_freq = (ROPE_THETA ** (-np.arange(0, HEAD_DIM, 2, dtype=np.float32) / HEAD_DIM)).astype(np.float32)
    freq = jnp.asarray(np.tile(inv_freq, LANES // HALF_DIM)[None, :])
    tr = ROW_TILE
    return pl.pallas_call(
        _rope_kernel,
        grid=(n // tr,),
        in_specs=[pl.BlockSpec((tr, 1), lambda i: (i, 0)),
                  pl.BlockSpec((1, LANES), lambda i: (0, 0))],
        out_specs=[pl.BlockSpec((tr, LANES), lambda i: (i, 0)),
                   pl.BlockSpec((tr, LANES), lambda i: (i, 0))],
        out_shape=[jax.ShapeDtypeStruct((n, LANES), jnp.float32)] * 2,
        compiler_params=_cparams("parallel"),
        name="rope_tables",
    )(positions.reshape(n, 1), freq)


def _conv_mixer_kernel(h_ref, g_ref, win_ref, cw_ref, wout_ref, o_ref, hn_buf, u_buf, v_buf):
    t = h_ref.shape[0]

    @pl.when(pl.program_id(1) == 0)
    def _():
        u_buf[0:8, :] = jnp.zeros((8, D_MODEL), jnp.float32)

    x = h_ref[...]
    hn_buf[...] = _rms(x, g_ref[...]).astype(jnp.bfloat16)
    cw = cw_ref[...]
    chunk = 512
    for c in range(0, D_MODEL, chunk):
        hn = hn_buf[...]
        gate_c = jnp.dot(hn, win_ref[:, D_MODEL + c:D_MODEL + c + chunk], preferred_element_type=jnp.float32)
        hh = jnp.dot(hn, win_ref[:, 2 * D_MODEL + c:2 * D_MODEL + c + chunk], preferred_element_type=jnp.float32)
        u_buf[8:8 + t, c:c + chunk] = gate_c * hh
        conv = (cw[0:1, c:c + chunk] * u_buf[6:6 + t, c:c + chunk]
                + cw[1:2, c:c + chunk] * u_buf[7:7 + t, c:c + chunk]
                + cw[2:3, c:c + chunk] * u_buf[8:8 + t, c:c + chunk])
        gate_b = jnp.dot(hn, win_ref[:, c:c + chunk], preferred_element_type=jnp.float32)
        v_buf[:, c:c + chunk] = (gate_b * conv).astype(jnp.bfloat16)
    u_buf[0:8, :] = u_buf[t:t + 8, :]
    o_ref[...] = x + jnp.dot(v_buf[...], wout_ref[...], preferred_element_type=jnp.float32)


def conv_mixer(h, batch, gain, w_in, conv_w, w_out):
    n = h.shape[0]
    seq = n // batch
    t = CONV_TILE
    ns = seq // t
    row = pl.BlockSpec((t, D_MODEL), lambda b, s: (b * ns + s, 0))
    return pl.pallas_call(
        _conv_mixer_kernel,
        grid=(batch, ns),
        in_specs=[row,
                  pl.BlockSpec((1, D_MODEL), lambda b, s: (0, 0)),
                  pl.BlockSpec((D_MODEL, 3 * D_MODEL), lambda b, s: (0, 0)),
                  pl.BlockSpec((CONV_WIDTH, D_MODEL), lambda b, s: (0, 0)),
                  pl.BlockSpec((D_MODEL, D_MODEL), lambda b, s: (0, 0))],
        out_specs=row,
        out_shape=jax.ShapeDtypeStruct((n, D_MODEL), jnp.float32),
        scratch_shapes=[pltpu.VMEM((t, D_MODEL), jnp.bfloat16),
                        pltpu.VMEM((t + 8, D_MODEL), jnp.float32),
                        pltpu.VMEM((t, D_MODEL), jnp.bfloat16)],
        compiler_params=_cparams("arbitrary", "arbitrary"),
        name="conv_mixer",
    )(h, gain.reshape(1, D_MODEL), w_in, conv_w, w_out)


def _qkv_kernel(h_ref, g_ref, w_ref, cos_ref, sin_ref, o_ref, z_buf, *, dil):
    t = h_ref.shape[0]
    half = D_MODEL // 2
    reps = half // LANES
    rows = t // dil

    def residue_major(x):
        if dil == 1:
            return x
        planes = x.shape[1] // LANES
        for c in range(planes):
            z_buf[c] = x[:, c * LANES:(c + 1) * LANES]
        return jnp.concatenate(
            [jnp.concatenate([z_buf[c, pl.ds(r, rows, stride=dil), :] for r in range(dil)], axis=0)
             for c in range(planes)], axis=1)

    cos128 = residue_major(cos_ref[...])
    sin128 = residue_major(sin_ref[...])
    cos = jnp.concatenate([cos128] * reps, axis=1)
    sin = jnp.concatenate([sin128] * reps, axis=1)
    hn = residue_major(_rms(h_ref[...], g_ref[...])).astype(jnp.bfloat16)
    for part in range(3):
        z = jnp.dot(hn, w_ref[:, part * D_MODEL:(part + 1) * D_MODEL], preferred_element_type=jnp.float32)
        if part < 2:
            z1 = z[:, :half]
            z2 = z[:, half:]
            z = jnp.concatenate([z1 * cos - z2 * sin, z2 * cos + z1 * sin], axis=1)
        if part == 0:
            z = z * (HEAD_DIM ** -0.5 * LOG2_E)
        z = z.astype(o_ref.dtype)
        for r in range(dil):
            o_ref[0, r, :, part * D_MODEL:(part + 1) * D_MODEL] = z[r * rows:(r + 1) * rows]


def qkv_project(h, batch, gain, w_group, cos, sin, dil):
    n = h.shape[0]
    seq = n // batch
    t = QKV_TILE
    ns = seq // t
    row = lambda width: pl.BlockSpec((t, width), lambda b, s: (b * ns + s, 0))
    return pl.pallas_call(
        functools.partial(_qkv_kernel, dil=dil),
        grid=(batch, ns),
        in_specs=[row(D_MODEL),
                  pl.BlockSpec((1, D_MODEL), lambda b, s: (0, 0)),
                  pl.BlockSpec((D_MODEL, 3 * D_MODEL), lambda b, s: (0, 0)),
                  row(LANES), row(LANES)],
        out_specs=pl.BlockSpec((1, dil, t // dil, 3 * D_MODEL), lambda b, s: (b, 0, s, 0)),
        out_shape=jax.ShapeDtypeStruct((batch, dil, seq // dil, 3 * D_MODEL), jnp.bfloat16),
        scratch_shapes=[pltpu.VMEM((D_MODEL // LANES, t, LANES), jnp.float32)],
        compiler_params=_cparams("parallel", "parallel"),
        name=f"qkv_project_d{dil}",
    )(h, gain.reshape(1, D_MODEL), w_group, cos, sin)


def _attention_kernel(q_ref, k_ref, v_ref, o_ref, m_ref, d_ref, k_win, v_win, bias_buf):
    blk = ATT_BLK
    n = pl.program_id(2)
    n_res = q_ref.shape[1]
    n_sub = q_ref.shape[2] // blk
    heads_per_slab = LANES // HALF_DIM
    n_slabs = N_HEADS // heads_per_slab
    rows = heads_per_slab * blk

    @pl.when(n == 0)
    def _():
        for res in range(n_res):
            k_win[res, 0:blk, :] = jnp.zeros((blk, D_MODEL), k_win.dtype)
            v_win[res, 0:blk, :] = jnp.zeros((blk, D_MODEL), v_win.dtype)

    for res in range(n_res):
        k_win[res, blk:(n_sub + 1) * blk, :] = k_ref[0, res]
        v_win[res, blk:(n_sub + 1) * blk, :] = v_ref[0, res]

    @pl.when((pl.program_id(0) == 0) & (pl.program_id(1) == 0) & (n == 0))
    def _():
        qi = lax.broadcasted_iota(jnp.int32, (rows, 2 * blk), 0) % blk
        ki = lax.broadcasted_iota(jnp.int32, (rows, 2 * blk), 1)
        own_ok = (ki >= blk) & (ki - blk <= qi)
        back_ok = (ki < blk) & (ki >= qi)
        bias_buf[0] = jnp.where(own_ok, 0.0, NEG_INF).astype(jnp.float32)
        bias_buf[1] = jnp.where(own_ok | back_ok, 0.0, NEG_INF).astype(jnp.float32)

    first_plane = jnp.minimum(n, 1)

    half = D_MODEL // 2
    lane2 = lax.broadcasted_iota(jnp.int32, (blk, 2 * LANES), 1)
    lane1 = lax.broadcasted_iota(jnp.int32, (blk, LANES), 1)

    def scores(unit):
        res, sub, m = unit
        q_rows = slice(sub * blk, (sub + 1) * blk)
        k_rows = slice(sub * blk, (sub + 2) * blk)
        q_slab = jnp.concatenate([q_ref[0, res, q_rows, m * LANES:(m + 1) * LANES],
                                  q_ref[0, res, q_rows, half + m * LANES:half + (m + 1) * LANES]], axis=1)
        k_slab = jnp.concatenate([k_win[res, k_rows, m * LANES:(m + 1) * LANES],
                                  k_win[res, k_rows, half + m * LANES:half + (m + 1) * LANES]], axis=1)
        zero = jnp.zeros_like(q_slab)
        q_heads = jnp.concatenate(
            [jnp.where(((lane2 % LANES) // HALF_DIM) == a, q_slab, zero) for a in range(heads_per_slab)], axis=0)
        bias = bias_buf[first_plane] if sub == 0 else bias_buf[1]
        return lax.dot_general(q_heads, k_slab, (((1,), (1,)), ((), ())),
                               preferred_element_type=jnp.float32) + bias

    units = [(res, sub, m) for res in range(n_res) for sub in range(n_sub) for m in range(n_slabs)]
    m_ref[0] = jnp.zeros(m_ref.shape[1:], jnp.float32)
    d_ref[0] = jnp.ones(d_ref.shape[1:], jnp.float32)
    s_next = scores(units[0])
    for idx, (res, sub, m) in enumerate(units):
        s = s_next
        if idx + 1 < len(units):
            s_next = scores(units[idx + 1])
        mx = jnp.max(s, axis=-1, keepdims=True)
        p = jnp.exp2(s - mx)
        den = jnp.sum(p, axis=-1, keepdims=True)
        pb = p.astype(jnp.bfloat16)
        q_rows = slice(sub * blk, (sub + 1) * blk)
        k_rows = slice(sub * blk, (sub + 2) * blk)
        for a in range(heads_per_slab):
            hd = m * heads_per_slab + a
            m_ref[0, res, q_rows, hd:hd + 1] = mx[a * blk:(a + 1) * blk]
            d_ref[0, res, q_rows, hd:hd + 1] = den[a * blk:(a + 1) * blk]
        for pair in range(heads_per_slab // 2):
            col = (m * heads_per_slab + pair * 2) * HEAD_DIM
            r0 = pair * 2 * blk
            o2 = jnp.dot(pb[r0:r0 + 2 * blk], v_win[res, k_rows, col:col + LANES],
                         preferred_element_type=jnp.float32)
            o_ref[0, res, q_rows, col:col + LANES] = jnp.where(
                lane1 < HEAD_DIM, o2[:blk], o2[blk:]).astype(o_ref.dtype)
    for res in range(n_res):
        k_win[res, 0:blk, :] = k_win[res, n_sub * blk:(n_sub + 1) * blk, :]
        v_win[res, 0:blk, :] = v_win[res, n_sub * blk:(n_sub + 1) * blk, :]


def dilated_attention_group(qkv_g, dil):
    batch, _, length, _ = qkv_g.shape
    blk = ATT_BLK
    rows = min(ATT_BLOCKS_PER_STEP * blk, length)
    nstep = length // rows
    n_res = min(dil, max(1, ATT_BLOCKS_PER_STEP * blk // length))
    assert dil % n_res == 0

    def part_spec(part):
        return pl.BlockSpec((1, n_res, rows, D_MODEL), lambda b, r, i: (b, r, i, part))

    return pl.pallas_call(
        _attention_kernel,
        grid=(batch, dil // n_res, nstep),
        in_specs=[part_spec(0), part_spec(1), part_spec(2)],
        out_specs=[pl.BlockSpec((1, n_res, rows, D_MODEL), lambda b, r, i: (b, r, i, 0)),
                   pl.BlockSpec((1, n_res, rows, LANES), lambda b, r, i: (b, r, i, 0)),
                   pl.BlockSpec((1, n_res, rows, LANES), lambda b, r, i: (b, r, i, 0))],
        out_shape=[jax.ShapeDtypeStruct((batch, dil, length, D_MODEL), jnp.bfloat16),
                   jax.ShapeDtypeStruct((batch, dil, length, LANES), jnp.float32),
                   jax.ShapeDtypeStruct((batch, dil, length, LANES), jnp.float32)],
        scratch_shapes=[pltpu.VMEM((n_res, rows + blk, D_MODEL), jnp.bfloat16),
                        pltpu.VMEM((n_res, rows + blk, D_MODEL), jnp.bfloat16),
                        pltpu.VMEM((2, (LANES // HALF_DIM) * blk, 2 * blk), jnp.float32)],
        compiler_params=_cparams("arbitrary", "arbitrary", "arbitrary"),
        name=f"dilated_attention_d{dil}",
    )(qkv_g, qkv_g, qkv_g)


def _merge_proj_kernel(*refs, dils):
    ng = len(dils)
    h_ref = refs[0]
    o_refs = refs[1:1 + ng]
    m_refs = refs[1 + ng:1 + 2 * ng]
    d_refs = refs[1 + 2 * ng:1 + 3 * ng]
    e_ref, w_ref, out_ref, o_buf, l_buf = refs[1 + 3 * ng:]
    t = h_ref.shape[0]

    def token_major(ref, dil):
        if dil == 1:
            return ref[0, 0]
        for r in range(dil):
            l_buf[pl.ds(r, t // dil, stride=dil), :] = ref[0, r]
        return l_buf[...]

    maxima = [token_major(m_refs[g], dil) for g, dil in enumerate(dils)]
    lses = [maxima[g] + jnp.log2(token_major(d_refs[g], dil)) for g, dil in enumerate(dils)]
    top = functools.reduce(jnp.maximum, lses)
    inv = 1.0 / functools.reduce(jnp.add, [jnp.exp2(l - top) for l in lses])
    merged = jnp.zeros((t, D_MODEL), jnp.float32)
    for g, dil in enumerate(dils):
        wg = jnp.exp2(maxima[g] - top) * inv
        hi = wg.astype(jnp.bfloat16)
        lo = (wg - hi.astype(jnp.float32)).astype(jnp.bfloat16)
        wexp = jnp.dot(jnp.concatenate([hi, lo], axis=1), e_ref[...], preferred_element_type=jnp.float32)
        if dil == 1:
            og = o_refs[g][0, 0].astype(jnp.float32)
        else:
            for r in range(dil):
                part = o_refs[g][0, r].astype(jnp.float32)
                for c in range(D_MODEL // LANES):
                    o_buf[c, pl.ds(r, t // dil, stride=dil), :] = part[:, c * LANES:(c + 1) * LANES]
            og = jnp.concatenate([o_buf[c] for c in range(D_MODEL // LANES)], axis=1)
        merged = merged + wexp * og
    out_ref[...] = h_ref[...] + jnp.dot(merged.astype(jnp.bfloat16), w_ref[...],
                                        preferred_element_type=jnp.float32)


def merge_proj_residual(h, batch, outs, maxima, dens, w_o):
    n = h.shape[0]
    seq = n // batch
    t = PROJ_TILE
    ns = seq // t
    dils = tuple(o.shape[1] for o in outs)
    row = pl.BlockSpec((t, D_MODEL), lambda b, s: (b * ns + s, 0))
    expand = np.zeros((2 * LANES, D_MODEL), np.float32)
    for hd in range(N_HEADS):
        expand[hd, hd * HEAD_DIM:(hd + 1) * HEAD_DIM] = 1.0
        expand[LANES + hd, hd * HEAD_DIM:(hd + 1) * HEAD_DIM] = 1.0
    in_specs = [row]
    in_specs += [pl.BlockSpec((1, d, t // d, D_MODEL), lambda b, s: (b, 0, s, 0)) for d in dils]
    in_specs += [pl.BlockSpec((1, d, t // d, LANES), lambda b, s: (b, 0, s, 0)) for d in dils] * 2
    in_specs += [pl.BlockSpec((2 * LANES, D_MODEL), lambda b, s: (0, 0)),
                 pl.BlockSpec((D_MODEL, D_MODEL), lambda b, s: (0, 0))]
    return pl.pallas_call(
        functools.partial(_merge_proj_kernel, dils=dils),
        grid=(batch, ns), in_specs=in_specs, out_specs=row,
        out_shape=jax.ShapeDtypeStruct((n, D_MODEL), jnp.float32),
        scratch_shapes=[pltpu.VMEM((D_MODEL // LANES, t, LANES), jnp.float32),
                        pltpu.VMEM((t, LANES), jnp.float32)],
        compiler_params=_cparams("parallel", "parallel"), name="attn_merge_out_proj",
    )(h, *outs, *maxima, *dens, jnp.asarray(expand, jnp.bfloat16), w_o)


def _router_kernel(h_ref, g_ref, w_ref, b_ref, tri_ref, hn_ref, route_ref, count_ref, run_ref):
    @pl.when(pl.program_id(0) == 0)
    def _():
        run_ref[...] = jnp.zeros_like(run_ref)

    hn = _rms(h_ref[...], g_ref[...])
    hn_ref[:, :PACKED] = _pack_rows(hn)
    hn_hi = hn.astype(jnp.bfloat16)
    hn_lo = (hn - hn_hi.astype(jnp.float32)).astype(jnp.bfloat16)
    both = jnp.dot(hn_hi, w_ref[...], preferred_element_type=jnp.float32)
    logits = (both[:, :LANES] + both[:, LANES:]
              + jnp.dot(hn_lo, w_ref[:, :LANES], preferred_element_type=jnp.float32) + b_ref[...])
    lane = lax.broadcasted_iota(jnp.int32, logits.shape, 1)
    big = jnp.int32(LANES)

    def first_argmax(vals, vmax):
        return jnp.min(jnp.where(vals == vmax, lane, big), axis=-1, keepdims=True)

    coarse = jnp.where(lane < N_EXPERT_GROUPS, logits, NEG_INF)
    cmax = jnp.max(coarse, axis=-1, keepdims=True)
    p_top = 1.0 / jnp.sum(jnp.exp(coarse - cmax), axis=-1, keepdims=True)
    g_top = first_argmax(coarse, cmax)
    lo = N_EXPERT_GROUPS + g_top * EXPERTS_PER_GROUP
    fine = jnp.where((lane >= lo) & (lane < lo + EXPERTS_PER_GROUP), logits, NEG_INF)
    v1 = jnp.max(fine, axis=-1, keepdims=True)
    i1 = first_argmax(fine, v1)
    fine2 = jnp.where(lane == i1, NEG_INF, fine)
    v2 = jnp.max(fine2, axis=-1, keepdims=True)
    i2 = first_argmax(fine2, v2)
    e2 = jnp.exp(v2 - v1)
    w1 = 1.0 / (1.0 + e2)
    w2 = e2 / (1.0 + e2)
    first_lo = i1 < i2
    a = jnp.where(first_lo, i1, i2) - lo
    b = jnp.where(first_lo, i2, i1) - lo
    pair = jnp.where(a == 0, b - 1, jnp.where(a == 1, jnp.where(b == 3, 3, 4), 5))
    cls = g_top * len(PAIR_SLOTS) + pair
    gate_lo = p_top * jnp.where(first_lo, w1, w2)
    gate_hi = p_top * jnp.where(first_lo, w2, w1)
    swapped = pair == 5
    gate_a = jnp.where(swapped, gate_hi, gate_lo)
    gate_b = jnp.where(swapped, gate_lo, gate_hi)
    gates = jnp.where(lane == 0, gate_a, jnp.where(lane == 1, gate_b, 0.0))
    hn_ref[:, PACKED:] = pltpu.bitcast(gates, jnp.uint32)

    sel = lane == cls
    onehot = jnp.where(sel, 1.0, 0.0)
    sub = tri_ref.shape[0]
    run = run_ref[...]
    befores = []
    for r0 in range(0, onehot.shape[0], sub):
        oh = onehot[r0:r0 + sub]
        befores.append(jnp.dot(tri_ref[...], oh.astype(jnp.bfloat16), preferred_element_type=jnp.float32) + run)
        run = run + jnp.sum(oh, axis=0, keepdims=True)
    before = jnp.concatenate(befores, axis=0)
    rank = jnp.sum(jnp.where(sel, before, 0.0), axis=-1, keepdims=True)
    run_ref[...] = run
    count_ref[...] = run

    route = jnp.where(lane == R_CLASS, cls.astype(jnp.float32), 0.0)
    route = jnp.where(lane == R_RANK, rank, route)
    route_ref[...] = route


def moe_router(h, gain, w_route, b_route):
    n = h.shape[0]
    tr = ROW_TILE
    sub = RANK_BLOCK
    tri = jnp.asarray(np.tril(np.ones((sub, sub), np.float32), -1), jnp.bfloat16)
    w_hi = w_route.astype(jnp.bfloat16)
    w_lo = (w_route - w_hi.astype(jnp.float32)).astype(jnp.bfloat16)
    w_split = jnp.concatenate([w_hi, w_lo], axis=1)
    return pl.pallas_call(
        _router_kernel, grid=(n // tr,),
        in_specs=[pl.BlockSpec((tr, D_MODEL), lambda i: (i, 0)),
                  pl.BlockSpec((1, D_MODEL), lambda i: (0, 0)),
                  pl.BlockSpec((D_MODEL, 2 * LANES), lambda i: (0, 0)),
                  pl.BlockSpec((1, LANES), lambda i: (0, 0)),
                  pl.BlockSpec((sub, sub), lambda i: (0, 0))],
        out_specs=[pl.BlockSpec((tr, ROW_WORDS), lambda i: (i, 0)),
                   pl.BlockSpec((tr, LANES), lambda i: (i, 0)),
                   pl.BlockSpec((1, LANES), lambda i: (0, 0))],
        out_shape=[jax.ShapeDtypeStruct((n, ROW_WORDS), jnp.uint32),
                   jax.ShapeDtypeStruct((n, LANES), jnp.float32),
                   jax.ShapeDtypeStruct((1, LANES), jnp.float32)],
        scratch_shapes=[pltpu.VMEM((1, LANES), jnp.float32)],
        compiler_params=_cparams("arbitrary"), name="moe_router",
    )(h, gain.reshape(1, D_MODEL), w_split, b_route, tri)


def _dispatch_kernel(dest_ref, x_ref, init_ref, o_ref, sem):
    del init_ref
    t = x_ref.shape[0]

    def row_copy(j):
        return pltpu.make_async_copy(x_ref.at[pl.ds(j, 1)], o_ref.at[pl.ds(dest_ref[0, 0, j], 1)], sem)

    for j in range(t):
        row_copy(j).start(priority=j % 2)

    def wait(j, c):
        row_copy(0).wait()
        return c

    lax.fori_loop(0, t, wait, 0, unroll=DMA_UNROLL)


def moe_dispatch(rows, dest, n_rows):
    n = rows.shape[0]
    t = DISPATCH_TILE
    return pl.pallas_call(
        _dispatch_kernel, grid=(n // t,),
        in_specs=[pl.BlockSpec((1, 1, t), lambda i: (i, 0, 0), memory_space=pltpu.SMEM),
                  pl.BlockSpec((t, ROW_WORDS), lambda i: (i, 0)),
                  pl.BlockSpec(memory_space=pl.ANY)],
        out_specs=pl.BlockSpec(memory_space=pl.ANY),
        out_shape=jax.ShapeDtypeStruct((n_rows, ROW_WORDS), jnp.uint32),
        scratch_shapes=[pltpu.SemaphoreType.DMA(())],
        input_output_aliases={2: 0},
        compiler_params=_cparams("arbitrary"), name="moe_dispatch",
    )(dest.reshape(n // t, 1, t), rows, jnp.zeros((n_rows, ROW_WORDS), jnp.uint32))


def _expert_kernel(ea_ref, eb_ref, na_ref, nb_ref, tr_ref, tv_ref, x_ref,
                   wga_ref, wua_ref, wda_ref, wgb_ref, wub_ref, wdb_ref, o_ref,
                   wga_bf, wua_bf, wda_bf, wgb_bf, wub_bf, wdb_bf):
    del ea_ref, eb_ref, tr_ref
    i = pl.program_id(0)

    @pl.when(na_ref[i] > 0)
    def _():
        wga_bf[...] = wga_ref[0, 0].astype(jnp.bfloat16)
        wua_bf[...] = wua_ref[0, 0].astype(jnp.bfloat16)
        wda_bf[...] = wda_ref[0, 0].astype(jnp.bfloat16)

    @pl.when(nb_ref[i] > 0)
    def _():
        wgb_bf[...] = wgb_ref[0, 0].astype(jnp.bfloat16)
        wub_bf[...] = wub_ref[0, 0].astype(jnp.bfloat16)
        wdb_bf[...] = wdb_ref[0, 0].astype(jnp.bfloat16)

    @pl.when(tv_ref[i] > 0)
    def _():
        x = _unpack_rows(x_ref[:, :PACKED]).astype(jnp.bfloat16)
        gates = pltpu.bitcast(x_ref[:, PACKED:], jnp.float32)

        def mlp(wg, wu, wd):
            g = jnp.dot(x, wg[...], preferred_element_type=jnp.float32)
            u = jnp.dot(x, wu[...], preferred_element_type=jnp.float32)
            hmid = (g * jax.nn.sigmoid(g) * u).astype(jnp.bfloat16)
            return jnp.dot(hmid, wd[...], preferred_element_type=jnp.float32)

        o_ref[...] = _pack_rows(gates[:, 0:1] * mlp(wga_bf, wua_bf, wda_bf)
                                + gates[:, 1:2] * mlp(wgb_bf, wub_bf, wdb_bf))

    @pl.when(tv_ref[i] == 0)
    def _():
        o_ref[...] = jnp.zeros_like(o_ref)


def expert_mlp(x_sorted, tables, layer, w_gate, w_up, w_down):
    p = x_sorted.shape[0]
    tm = EXPERT_TILE

    def weight_spec(shape, slot):
        return pl.BlockSpec((1, 1) + shape, lambda i, ea, eb, na, nb, tr, tv: (layer, (ea, eb)[slot][i], 0, 0))

    up_shape, down_shape = (D_MODEL, EXPERT_FF), (EXPERT_FF, D_MODEL)
    grid_spec = pltpu.PrefetchScalarGridSpec(
        num_scalar_prefetch=6,
        grid=(p // tm,),
        in_specs=[pl.BlockSpec((tm, ROW_WORDS), lambda i, ea, eb, na, nb, tr, tv: (tr[i], 0)),
                  weight_spec(up_shape, 0), weight_spec(up_shape, 0), weight_spec(down_shape, 0),
                  weight_spec(up_shape, 1), weight_spec(up_shape, 1), weight_spec(down_shape, 1)],
        out_specs=pl.BlockSpec((tm, PACKED), lambda i, ea, eb, na, nb, tr, tv: (i, 0)),
        scratch_shapes=[pltpu.VMEM(up_shape, jnp.bfloat16), pltpu.VMEM(up_shape, jnp.bfloat16),
                        pltpu.VMEM(down_shape, jnp.bfloat16)] * 2,
    )
    return pl.pallas_call(
        _expert_kernel, grid_spec=grid_spec,
        out_shape=jax.ShapeDtypeStruct((p, PACKED), jnp.uint32),
        compiler_params=_cparams("arbitrary"), name="expert_mlp",
    )(*tables, x_sorted, w_gate, w_up, w_down, w_gate, w_up, w_down)


def _combine_kernel(dcur_ref, dnext_ref, h_ref, gain_ref, y_ref, o_ref, y_buf, sem, *, final_norm):
    i = pl.program_id(0)
    last = pl.num_programs(0) - 1
    t = h_ref.shape[0]
    slot = i % 2

    def row_copy(dest_ref, s, j):
        return pltpu.make_async_copy(y_ref.at[pl.ds(dest_ref[0, 0, j], 1)],
                                     y_buf.at[s, pl.ds(j, 1)], sem.at[s])

    def start_all(dest_ref, s):
        for j in range(t):
            row_copy(dest_ref, s, j).start(priority=j % 2)

    @pl.when(i == 0)
    def _():
        start_all(dcur_ref, 0)

    for s in range(2):
        @pl.when((i < last) & (slot == 1 - s))
        def _():
            start_all(dnext_ref, s)

    lax.fori_loop(0, t, lambda j, c: (row_copy(dcur_ref, slot, 0).wait(), c)[1], 0, unroll=DMA_UNROLL)
    out = h_ref[...] + _unpack_rows(y_buf[slot])
    if final_norm:
        out = _rms(out, gain_ref[...])
    o_ref[...] = out


def moe_combine(h, y, dest, final_gain):
    n = h.shape[0]
    t = COMBINE_TILE
    nt = n // t
    dest3 = dest.reshape(nt, 1, t)
    final_norm = final_gain is not None
    gain = final_gain if final_norm else jnp.ones((D_MODEL,), jnp.float32)
    return pl.pallas_call(
        functools.partial(_combine_kernel, final_norm=final_norm), grid=(nt,),
        in_specs=[pl.BlockSpec((1, 1, t), lambda i: (i, 0, 0), memory_space=pltpu.SMEM),
                  pl.BlockSpec((1, 1, t), lambda i: (jnp.minimum(i + 1, nt - 1), 0, 0),
                               memory_space=pltpu.SMEM),
                  pl.BlockSpec((t, D_MODEL), lambda i: (i, 0)),
                  pl.BlockSpec((1, D_MODEL), lambda i: (0, 0)),
                  pl.BlockSpec(memory_space=pl.ANY)],
        out_specs=pl.BlockSpec((t, D_MODEL), lambda i: (i, 0)),
        out_shape=jax.ShapeDtypeStruct((n, D_MODEL), jnp.float32),
        scratch_shapes=[pltpu.VMEM((2, t, PACKED), jnp.uint32), pltpu.SemaphoreType.DMA((2,))],
        compiler_params=_cparams("arbitrary"), name="moe_combine",
    )(dest3, dest3, h, gain.reshape(1, D_MODEL), y)


def hierarchical_moe(h, gain, w_route, b_route, layer, w_gate, w_up, w_down, final_gain=None):
    n = h.shape[0]
    tm = EXPERT_TILE
    rows, route, counts = moe_router(h, gain, w_route, b_route)
    counts = counts[0, :N_CLASSES].astype(jnp.int32)
    padded = ((counts + tm - 1) // tm) * tm
    seg_end = jnp.cumsum(padded)
    seg_start = seg_end - padded
    cls = route[:, R_CLASS].astype(jnp.int32)
    rank = route[:, R_RANK].astype(jnp.int32)
    class_ids = jnp.arange(N_CLASSES, dtype=jnp.int32)
    dest = jnp.sum(jnp.where(cls[:, None] == class_ids[None, :], seg_start[None, :], 0), axis=-1) + rank
    n_rows = n + N_CLASSES * tm
    n_tiles = n_rows // tm
    used = seg_end[-1] // tm
    tile_id = jnp.arange(n_tiles, dtype=jnp.int32)
    tile_row = jnp.minimum(tile_id, used - 1).astype(jnp.int32)
    tile_class = jnp.minimum(jnp.sum((seg_end[None, :] <= (tile_row * tm)[:, None]).astype(jnp.int32), axis=1),
                             N_CLASSES - 1)
    tile_valid = (tile_id < used).astype(jnp.int32)
    slots = np.asarray([[g * EXPERTS_PER_GROUP + s for s in pair]
                        for g in range(N_EXPERT_GROUPS) for pair in PAIR_SLOTS], np.int32)
    tile_onehot = (tile_class[:, None] == class_ids[None, :]).astype(jnp.int32)
    expert_a = jnp.sum(tile_onehot * slots[None, :, 0], axis=1).astype(jnp.int32)
    expert_b = jnp.sum(tile_onehot * slots[None, :, 1], axis=1).astype(jnp.int32)

    def changed(e):
        return jnp.concatenate([jnp.ones((1,), jnp.int32), (e[1:] != e[:-1]).astype(jnp.int32)])

    tables = (expert_a, expert_b, changed(expert_a), changed(expert_b), tile_row, tile_valid)
    x_sorted = moe_dispatch(rows, dest, n_rows)
    y = expert_mlp(x_sorted, tables, layer, w_gate, w_up, w_down)
    return moe_combine(h, y, dest, final_gain)


def _permute_qk_columns(w_qkv):
    w = w_qkv.astype(jnp.bfloat16).reshape(D_MODEL, N_ATTN_GROUPS, 3, N_HEADS, 2, HALF_DIM)
    qk = jnp.transpose(w[:, :, 0:2], (0, 1, 2, 4, 3, 5)).reshape(D_MODEL, N_ATTN_GROUPS, 2, D_MODEL)
    v = w[:, :, 2:3].reshape(D_MODEL, N_ATTN_GROUPS, 1, D_MODEL)
    return jnp.concatenate([qk, v], axis=2).reshape(D_MODEL, QKV_WIDTH)


def kernel(x, positions, norm_mix, norm_ffn, norm_final, conv_w_in, conv_w, conv_w_out,
           attn_w_qkv, attn_w_o, w_coarse, b_coarse, w_fine, b_fine, w_gate, w_up, w_down):
    batch, seq, d = x.shape
    assert d == D_MODEL
    n = batch * seq
    depth = norm_mix.shape[0]
    bf = jnp.bfloat16
    h = x.reshape(n, d)
    cos, sin = rope_tables(positions.reshape(n))
    for i in range(depth):
        j = i // 2
        if i % 2 == 0:
            h = conv_mixer(h, batch, norm_mix[i], conv_w_in[j].astype(bf), conv_w[j], conv_w_out[j].astype(bf))
        else:
            w_qkv = _permute_qk_columns(attn_w_qkv[j])
            outs, maxima, dens = [], [], []
            for g, (window, dil) in enumerate(DILATED_GROUPS):
                assert window // dil == ATT_BLK
                w_group = w_qkv[:, g * 3 * D_MODEL:(g + 1) * 3 * D_MODEL]
                o, mx, den = dilated_attention_group(
                    qkv_project(h, batch, norm_mix[i], w_group, cos, sin, dil), dil)
                outs.append(o)
                maxima.append(mx)
                dens.append(den)
            h = merge_proj_residual(h, batch, outs, maxima, dens, attn_w_o[j].astype(bf))
        w_route = jnp.concatenate(
            [w_coarse[i], jnp.transpose(w_fine[i], (1, 0, 2)).reshape(d, N_EXPERTS),
             jnp.zeros((d, LANES - N_EXPERT_GROUPS - N_EXPERTS), jnp.float32)], axis=1)
        b_route = jnp.concatenate(
            [b_coarse[i], b_fine[i].reshape(-1),
             jnp.zeros((LANES - N_EXPERT_GROUPS - N_EXPERTS,), jnp.float32)]).reshape(1, LANES)
        h = hierarchical_moe(h, norm_ffn[i], w_route, b_route,
                             i, w_gate, w_up, w_down,
                             norm_final if i == depth - 1 else None)
    return h.reshape(batch, seq, d)
```

```python
import functools
import math

import numpy as np
import jax
import jax.numpy as jnp
from jax import lax
from jax.experimental import pallas as pl
from jax.experimental.pallas import tpu as pltpu

D_MODEL = 1024
NORM_EPS = 1e-6
CONV_WIDTH = 3
HEAD_DIM = 64
HALF_DIM = HEAD_DIM // 2
N_HEADS = D_MODEL // HEAD_DIM
DILATED_GROUPS = ((128, 1), (512, 4), (2048, 16))
N_ATTN_GROUPS = len(DILATED_GROUPS)
QKV_WIDTH = N_ATTN_GROUPS * 3 * D_MODEL
ROPE_THETA = 10000.0
N_EXPERT_GROUPS = 4
EXPERTS_PER_GROUP = 4
N_EXPERTS = N_EXPERT_GROUPS * EXPERTS_PER_GROUP
EXPERT_FF = D_MODEL // 2
PACKED = D_MODEL // 2

LANES = 128
ATT_BLK = 128
ATT_BLOCKS_PER_STEP = 8
VMEM_LIMIT_BYTES = 56 * 1024 * 1024
ROW_TILE = 1024
QKV_TILE = 1024
PROJ_TILE = 1024
CONV_TILE = 1024
EXPERT_TILE = 512
DISPATCH_TILE = 1024
COMBINE_TILE = 512
RANK_BLOCK = 256
DMA_UNROLL = 8
NEG_INF = float("-inf")
LOG2_E = math.log2(math.e)
R_CLASS, R_RANK = range(2)
PAIR_SLOTS = ((0, 1), (0, 2), (0, 3), (1, 3), (1, 2), (3, 2))
N_CLASSES = N_EXPERT_GROUPS * len(PAIR_SLOTS)
ROW_WORDS = PACKED + LANES


def _cparams(*sem):
    return pltpu.CompilerParams(dimension_semantics=sem, vmem_limit_bytes=VMEM_LIMIT_BYTES)


def _rms(x, g):
    return x * lax.rsqrt(jnp.mean(x * x, axis=-1, keepdims=True) + NORM_EPS) * g


def _pack_rows(x):
    lo = pltpu.bitcast(x[:, :PACKED].astype(jnp.bfloat16).astype(jnp.float32), jnp.uint32)
    hi = pltpu.bitcast(x[:, PACKED:].astype(jnp.bfloat16).astype(jnp.float32), jnp.uint32)
    return (lo >> 16) | (hi & jnp.uint32(0xFFFF0000))


def _unpack_rows(u):
    lo = pltpu.bitcast(u << 16, jnp.float32)
    hi = pltpu.bitcast(u & jnp.uint32(0xFFFF0000), jnp.float32)
    return jnp.concatenate([lo, hi], axis=1)


def _rope_kernel(pos_ref, freq_ref, cos_ref, sin_ref):
    ang = pos_ref[...].astype(jnp.float32) * freq_ref[...]
    cos_ref[...] = jnp.cos(ang)
    sin_ref[...] = jnp.sin(ang)


def rope_tables(positions):
    n = positions.shape[0]
    inv_freq = (ROPE_THETA ** (-np.arange(0, HEAD_DIM, 2, dtype=np.float32) / HEAD_DIM)).astype(np.float32)
    freq = jnp.asarray(np.tile(inv_freq, LANES // HALF_DIM)[None, :])
    tr = ROW_TILE
    return pl.pallas_call(
        _rope_kernel,
        grid=(n // tr,),
        in_specs=[pl.BlockSpec((tr, 1), lambda i: (i, 0)),
                  pl.BlockSpec((1, LANES), lambda i: (0, 0))],
        out_specs=[pl.BlockSpec((tr, LANES), lambda i: (i, 0)),
                   pl.BlockSpec((tr, LANES), lambda i: (i, 0))],
        out_shape=[jax.ShapeDtypeStruct((n, LANES), jnp.float32)] * 2,
        compiler_params=_cparams("parallel"),
        name="rope_tables",
    )(positions.reshape(n, 1), freq)


def _conv_mixer_kernel(h_ref, g_ref, win_ref, cw_ref, wout_ref, o_ref, hn_buf, u_buf, v_buf):
    t = h_ref.shape[0]

    @pl.when(pl.program_id(1) == 0)
    def _():
        u_buf[0:8, :] = jnp.zeros((8, D_MODEL), jnp.float32)

    x = h_ref[...]
    hn_buf[...] = _rms(x, g_ref[...]).astype(jnp.bfloat16)
    cw = cw_ref[...]
    chunk = 512
    for c in range(0, D_MODEL, chunk):
        hn = hn_buf[...]
        gate_c = jnp.dot(hn, win_ref[:, D_MODEL + c:D_MODEL + c + chunk], preferred_element_type=jnp.float32)
        hh = jnp.dot(hn, win_ref[:, 2 * D_MODEL + c:2 * D_MODEL + c + chunk], preferred_element_type=jnp.float32)
        u_buf[8:8 + t, c:c + chunk] = gate_c * hh
        conv = (cw[0:1, c:c + chunk] * u_buf[6:6 + t, c:c + chunk]
                + cw[1:2, c:c + chunk] * u_buf[7:7 + t, c:c + chunk]
                + cw[2:3, c:c + chunk] * u_buf[8:8 + t, c:c + chunk])
        gate_b = jnp.dot(hn, win_ref[:, c:c + chunk], preferred_element_type=jnp.float32)
        v_buf[:, c:c + chunk] = (gate_b * conv).astype(jnp.bfloat16)
    u_buf[0:8, :] = u_buf[t:t + 8, :]
    o_ref[...] = x + jnp.dot(v_buf[...], wout_ref[...], preferred_element_type=jnp.float32)


def conv_mixer(h, batch, gain, w_in, conv_w, w_out):
    n = h.shape[0]
    seq = n // batch
    t = CONV_TILE
    ns = seq // t
    row = pl.BlockSpec((t, D_MODEL), lambda b, s: (b * ns + s, 0))
    return pl.pallas_call(
        _conv_mixer_kernel,
        grid=(batch, ns),
        in_specs=[row,
                  pl.BlockSpec((1, D_MODEL), lambda b, s: (0, 0)),
                  pl.BlockSpec((D_MODEL, 3 * D_MODEL), lambda b, s: (0, 0)),
                  pl.BlockSpec((CONV_WIDTH, D_MODEL), lambda b, s: (0, 0)),
                  pl.BlockSpec((D_MODEL, D_MODEL), lambda b, s: (0, 0))],
        out_specs=row,
        out_shape=jax.ShapeDtypeStruct((n, D_MODEL), jnp.float32),
        scratch_shapes=[pltpu.VMEM((t, D_MODEL), jnp.bfloat16),
                        pltpu.VMEM((t + 8, D_MODEL), jnp.float32),
                        pltpu.VMEM((t, D_MODEL), jnp.bfloat16)],
        compiler_params=_cparams("arbitrary", "arbitrary"),
        name="conv_mixer",
    )(h, gain.reshape(1, D_MODEL), w_in, conv_w, w_out)


def _qkv_kernel(h_ref, g_ref, w_ref, cos_ref, sin_ref, o_ref, z_buf, *, dil):
    t = h_ref.shape[0]
    half = D_MODEL // 2
    reps = half // LANES
    rows = t // dil

    def residue_major(x):
        if dil == 1:
            return x
        planes = x.shape[1] // LANES
        for c in range(planes):
            z_buf[c] = x[:, c * LANES:(c + 1) * LANES]
        return jnp.concatenate(
            [jnp.concatenate([z_buf[c, pl.ds(r, rows, stride=dil), :] for r in range(dil)], axis=0)
             for c in range(planes)], axis=1)

    cos128 = residue_major(cos_ref[...])
    sin128 = residue_major(sin_ref[...])
    cos = jnp.concatenate([cos128] * reps, axis=1)
    sin = jnp.concatenate([sin128] * reps, axis=1)
    hn = residue_major(_rms(h_ref[...], g_ref[...])).astype(jnp.bfloat16)
    for part in range(3):
        z = jnp.dot(hn, w_ref[:, part * D_MODEL:(part + 1) * D_MODEL], preferred_element_type=jnp.float32)
        if part < 2:
            z1 = z[:, :half]
            z2 = z[:, half:]
            z = jnp.concatenate([z1 * cos - z2 * sin, z2 * cos + z1 * sin], axis=1)
        if part == 0:
            z = z * (HEAD_DIM ** -0.5 * LOG2_E)
        z = z.astype(o_ref.dtype)
        for r in range(dil):
            o_ref[0, r, :, part * D_MODEL:(part + 1) * D_MODEL] = z[r * rows:(r + 1) * rows]


def qkv_project(h, batch, gain, w_group, cos, sin, dil):
    n = h.shape[0]
    seq = n // batch
    t = QKV_TILE
    ns = seq // t
    row = lambda width: pl.BlockSpec((t, width), lambda b, s: (b * ns + s, 0))
    return pl.pallas_call(
        functools.partial(_qkv_kernel, dil=dil),
        grid=(batch, ns),
        in_specs=[row(D_MODEL),
                  pl.BlockSpec((1, D_MODEL), lambda b, s: (0, 0)),
                  pl.BlockSpec((D_MODEL, 3 * D_MODEL), lambda b, s: (0, 0)),
                  row(LANES), row(LANES)],
        out_specs=pl.BlockSpec((1, dil, t // dil, 3 * D_MODEL), lambda b, s: (b, 0, s, 0)),
        out_shape=jax.ShapeDtypeStruct((batch, dil, seq // dil, 3 * D_MODEL), jnp.bfloat16),
        scratch_shapes=[pltpu.VMEM((D_MODEL // LANES, t, LANES), jnp.float32)],
        compiler_params=_cparams("parallel", "parallel"),
        name=f"qkv_project_d{dil}",
    )(h, gain.reshape(1, D_MODEL), w_group, cos, sin)


def _attention_kernel(q_ref, k_ref, v_ref, o_ref, m_ref, d_ref, k_win, v_win, bias_buf):
    blk = ATT_BLK
    n = pl.program_id(2)
    n_res = q_ref.shape[1]
    n_sub = q_ref.shape[2] // blk
    heads_per_slab = LANES // HALF_DIM
    n_slabs = N_HEADS // heads_per_slab
    rows = heads_per_slab * blk

    @pl.when(n == 0)
    def _():
        for res in range(n_res):
            k_win[res, 0:blk, :] = jnp.zeros((blk, D_MODEL), k_win.dtype)
            v_win[res, 0:blk, :] = jnp.zeros((blk, D_MODEL), v_win.dtype)

    for res in range(n_res):
        k_win[res, blk:(n_sub + 1) * blk, :] = k_ref[0, res]
        v_win[res, blk:(n_sub + 1) * blk, :] = v_ref[0, res]

    @pl.when((pl.program_id(0) == 0) & (pl.program_id(1) == 0) & (n == 0))
    def _():
        qi = lax.broadcasted_iota(jnp.int32, (rows, 2 * blk), 0) % blk
        ki = lax.broadcasted_iota(jnp.int32, (rows, 2 * blk), 1)
        own_ok = (ki >= blk) & (ki - blk <= qi)
        back_ok = (ki < blk) & (ki >= qi)
        bias_buf[0] = jnp.where(own_ok, 0.0, NEG_INF).astype(jnp.float32)
        bias_buf[1] = jnp.where(own_ok | back_ok, 0.0, NEG_INF).astype(jnp.float32)

    first_plane = jnp.minimum(n, 1)

    half = D_MODEL // 2
    lane2 = lax.broadcasted_iota(jnp.int32, (blk, 2 * LANES), 1)
    lane1 = lax.broadcasted_iota(jnp.int32, (blk, LANES), 1)

    def scores(unit):
        res, sub, m = unit
        q_rows = slice(sub * blk, (sub + 1) * blk)
        k_rows = slice(sub * blk, (sub + 2) * blk)
        q_slab = jnp.concatenate([q_ref[0, res, q_rows, m * LANES:(m + 1) * LANES],
                                  q_ref[0, res, q_rows, half + m * LANES:half + (m + 1) * LANES]], axis=1)
        k_slab = jnp.concatenate([k_win[res, k_rows, m * LANES:(m + 1) * LANES],
                                  k_win[res, k_rows, half + m * LANES:half + (m + 1) * LANES]], axis=1)
        zero = jnp.zeros_like(q_slab)
        q_heads = jnp.concatenate(
            [jnp.where(((lane2 % LANES) // HALF_DIM) == a, q_slab, zero) for a in range(heads_per_slab)], axis=0)
        bias = bias_buf[first_plane] if sub == 0 else bias_buf[1]
        return lax.dot_general(q_heads, k_slab, (((1,), (1,)), ((), ())),
                               preferred_element_type=jnp.float32) + bias

    units = [(res, sub, m) for res in range(n_res) for sub in range(n_sub) for m in range(n_slabs)]
    m_ref[0] = jnp.zeros(m_ref.shape[1:], jnp.float32)
    d_ref[0] = jnp.ones(d_ref.shape[1:], jnp.float32)
    s_next = scores(units[0])
    for idx, (res, sub, m) in enumerate(units):
        s = s_next
        if idx + 1 < len(units):
            s_next = scores(units[idx + 1])
        mx = jnp.max(s, axis=-1, keepdims=True)
        p = jnp.exp2(s - mx)
        den = jnp.sum(p, axis=-1, keepdims=True)
        pb = p.astype(jnp.bfloat16)
        q_rows = slice(sub * blk, (sub + 1) * blk)
        k_rows = slice(sub * blk, (sub + 2) * blk)
        for a in range(heads_per_slab):
            hd = m * heads_per_slab + a
            m_ref[0, res, q_rows, hd:hd + 1] = mx[a * blk:(a + 1) * blk]
            d_ref[0, res, q_rows, hd:hd + 1] = den[a * blk:(a + 1) * blk]
        for pair in range(heads_per_slab // 2):
            col = (m * heads_per_slab + pair * 2) * HEAD_DIM
            r0 = pair * 2 * blk
            o2 = jnp.dot(pb[r0:r0 + 2 * blk], v_win[res, k_rows, col:col + LANES],
                         preferred_element_type=jnp.float32)
            o_ref[0, res, q_rows, col:col + LANES] = jnp.where(
                lane1 < HEAD_DIM, o2[:blk], o2[blk:]).astype(o_ref.dtype)
    for res in range(n_res):
        k_win[res, 0:blk, :] = k_win[res, n_sub * blk:(n_sub + 1) * blk, :]
        v_win[res, 0:blk, :] = v_win[res, n_sub * blk:(n_sub + 1) * blk, :]


def dilated_attention_group(qkv_g, dil):
    batch, _, length, _ = qkv_g.shape
    blk = ATT_BLK
    rows = min(ATT_BLOCKS_PER_STEP * blk, length)
    nstep = length // rows
    n_res = min(dil, max(1, ATT_BLOCKS_PER_STEP * blk // length))
    assert dil % n_res == 0

    def part_spec(part):
        return pl.BlockSpec((1, n_res, rows, D_MODEL), lambda b, r, i: (b, r, i, part))

    return pl.pallas_call(
        _attention_kernel,
        grid=(batch, dil // n_res, nstep),
        in_specs=[part_spec(0), part_spec(1), part_spec(2)],
        out_specs=[pl.BlockSpec((1, n_res, rows, D_MODEL), lambda b, r, i: (b, r, i, 0)),
                   pl.BlockSpec((1, n_res, rows, LANES), lambda b, r, i: (b, r, i, 0)),
                   pl.BlockSpec((1, n_res, rows, LANES), lambda b, r, i: (b, r, i, 0))],
        out_shape=[jax.ShapeDtypeStruct((batch, dil, length, D_MODEL), jnp.bfloat16),
                   jax.ShapeDtypeStruct((batch, dil, length, LANES), jnp.float32),
                   jax.ShapeDtypeStruct((batch, dil, length, LANES), jnp.float32)],
        scratch_shapes=[pltpu.VMEM((n_res, rows + blk, D_MODEL), jnp.bfloat16),
                        pltpu.VMEM((n_res, rows + blk, D_MODEL), jnp.bfloat16),
                        pltpu.VMEM((2, (LANES // HALF_DIM) * blk, 2 * blk), jnp.float32)],
        compiler_params=_cparams("arbitrary", "arbitrary", "arbitrary"),
        name=f"dilated_attention_d{dil}",
    )(qkv_g, qkv_g, qkv_g)


def _merge_proj_kernel(*refs, dils):
    ng = len(dils)
    h_ref = refs[0]
    o_refs = refs[1:1 + ng]
    m_refs = refs[1 + ng:1 + 2 * ng]
    d_refs = refs[1 + 2 * ng:1 + 3 * ng]
    e_ref, w_ref, out_ref, o_buf, l_buf = refs[1 + 3 * ng:]
    t = h_ref.shape[0]

    def token_major(ref, dil):
        if dil == 1:
            return ref[0, 0]
        for r in range(dil):
            l_buf[pl.ds(r, t // dil, stride=dil), :] = ref[0, r]
        return l_buf[...]

    maxima = [token_major(m_refs[g], dil) for g, dil in enumerate(dils)]
    lses = [maxima[g] + jnp.log2(token_major(d_refs[g], dil)) for g, dil in enumerate(dils)]
    top = functools.reduce(jnp.maximum, lses)
    inv = 1.0 / functools.reduce(jnp.add, [jnp.exp2(l - top) for l in lses])
    merged = jnp.zeros((t, D_MODEL), jnp.float32)
    for g, dil in enumerate(dils):
        wg = jnp.exp2(maxima[g] - top) * inv
        hi = wg.astype(jnp.bfloat16)
        lo = (wg - hi.astype(jnp.float32)).astype(jnp.bfloat16)
        wexp = jnp.dot(jnp.concatenate([hi, lo], axis=1), e_ref[...], preferred_element_type=jnp.float32)
        if dil == 1:
            og = o_refs[g][0, 0].astype(jnp.float32)
        else:
            for r in range(dil):
                part = o_refs[g][0, r].astype(jnp.float32)
                for c in range(D_MODEL // LANES):
                    o_buf[c, pl.ds(r, t // dil, stride=dil), :] = part[:, c * LANES:(c + 1) * LANES]
            og = jnp.concatenate([o_buf[c] for c in range(D_MODEL // LANES)], axis=1)
        merged = merged + wexp * og
    out_ref[...] = h_ref[...] + jnp.dot(merged.astype(jnp.bfloat16), w_ref[...],
                                        preferred_element_type=jnp.float32)


def merge_proj_residual(h, batch, outs, maxima, dens, w_o):
    n = h.shape[0]
    seq = n // batch
    t = PROJ_TILE
    ns = seq // t
    dils = tuple(o.shape[1] for o in outs)
    row = pl.BlockSpec((t, D_MODEL), lambda b, s: (b * ns + s, 0))
    expand = np.zeros((2 * LANES, D_MODEL), np.float32)
    for hd in range(N_HEADS):
        expand[hd, hd * HEAD_DIM:(hd + 1) * HEAD_DIM] = 1.0
        expand[LANES + hd, hd * HEAD_DIM:(hd + 1) * HEAD_DIM] = 1.0
    in_specs = [row]
    in_specs += [pl.BlockSpec((1, d, t // d, D_MODEL), lambda b, s: (b, 0, s, 0)) for d in dils]
    in_specs += [pl.BlockSpec((1, d, t // d, LANES), lambda b, s: (b, 0, s, 0)) for d in dils] * 2
    in_specs += [pl.BlockSpec((2 * LANES, D_MODEL), lambda b, s: (0, 0)),
                 pl.BlockSpec((D_MODEL, D_MODEL), lambda b, s: (0, 0))]
    return pl.pallas_call(
        functools.partial(_merge_proj_kernel, dils=dils),
        grid=(batch, ns), in_specs=in_specs, out_specs=row,
        out_shape=jax.ShapeDtypeStruct((n, D_MODEL), jnp.float32),
        scratch_shapes=[pltpu.VMEM((D_MODEL // LANES, t, LANES), jnp.float32),
                        pltpu.VMEM((t, LANES), jnp.float32)],
        compiler_params=_cparams("parallel", "parallel"), name="attn_merge_out_proj",
    )(h, *outs, *maxima, *dens, jnp.asarray(expand, jnp.bfloat16), w_o)


def _router_kernel(h_ref, g_ref, w_ref, b_ref, tri_ref, hn_ref, route_ref, count_ref, run_ref):
    @pl.when(pl.program_id(0) == 0)
    def _():
        run_ref[...] = jnp.zeros_like(run_ref)

    hn = _rms(h_ref[...], g_ref[...])
    hn_ref[:, :PACKED] = _pack_rows(hn)
    hn_hi = hn.astype(jnp.bfloat16)
    hn_lo = (hn - hn_hi.astype(jnp.float32)).astype(jnp.bfloat16)
    both = jnp.dot(hn_hi, w_ref[...], preferred_element_type=jnp.float32)
    logits = (both[:, :LANES] + both[:, LANES:]
              + jnp.dot(hn_lo, w_ref[:, :LANES], preferred_element_type=jnp.float32) + b_ref[...])
    lane = lax.broadcasted_iota(jnp.int32, logits.shape, 1)
    big = jnp.int32(LANES)

    def first_argmax(vals, vmax):
        return jnp.min(jnp.where(vals == vmax, lane, big), axis=-1, keepdims=True)

    coarse = jnp.where(lane < N_EXPERT_GROUPS, logits, NEG_INF)
    cmax = jnp.max(coarse, axis=-1, keepdims=True)
    p_top = 1.0 / jnp.sum(jnp.exp(coarse - cmax), axis=-1, keepdims=True)
    g_top = first_argmax(coarse, cmax)
    lo = N_EXPERT_GROUPS + g_top * EXPERTS_PER_GROUP
    fine = jnp.where((lane >= lo) & (lane < lo + EXPERTS_PER_GROUP), logits, NEG_INF)
    v1 = jnp.max(fine, axis=-1, keepdims=True)
    i1 = first_argmax(fine, v1)
    fine2 = jnp.where(lane == i1, NEG_INF, fine)
    v2 = jnp.max(fine2, axis=-1, keepdims=True)
    i2 = first_argmax(fine2, v2)
    e2 = jnp.exp(v2 - v1)
    w1 = 1.0 / (1.0 + e2)
    w2 = e2 / (1.0 + e2)
    first_lo = i1 < i2
    a = jnp.where(first_lo, i1, i2) - lo
    b = jnp.where(first_lo, i2, i1) - lo
    pair = jnp.where(a == 0, b - 1, jnp.where(a == 1, jnp.where(b == 3, 3, 4), 5))
    cls = g_top * len(PAIR_SLOTS) + pair
    gate_lo = p_top * jnp.where(first_lo, w1, w2)
    gate_hi = p_top * jnp.where(first_lo, w2, w1)
    swapped = pair == 5
    gate_a = jnp.where(swapped, gate_hi, gate_lo)
    gate_b = jnp.where(swapped, gate_lo, gate_hi)
    gates = jnp.where(lane == 0, gate_a, jnp.where(lane == 1, gate_b, 0.0))
    hn_ref[:, PACKED:] = pltpu.bitcast(gates, jnp.uint32)

    sel = lane == cls
    onehot = jnp.where(sel, 1.0, 0.0)
    sub = tri_ref.shape[0]
    run = run_ref[...]
    befores = []
    for r0 in range(0, onehot.shape[0], sub):
        oh = onehot[r0:r0 + sub]
        befores.append(jnp.dot(tri_ref[...], oh.astype(jnp.bfloat16), preferred_element_type=jnp.float32) + run)
        run = run + jnp.sum(oh, axis=0, keepdims=True)
    before = jnp.concatenate(befores, axis=0)
    rank = jnp.sum(jnp.where(sel, before, 0.0), axis=-1, keepdims=True)
    run_ref[...] = run
    count_ref[...] = run

    route = jnp.where(lane == R_CLASS, cls.astype(jnp.float32), 0.0)
    route = jnp.where(lane == R_RANK, rank, route)
    route_ref[...] = route


def moe_router(h, gain, w_route, b_route):
    n = h.shape[0]
    tr = ROW_TILE
    sub = RANK_BLOCK
    tri = jnp.asarray(np.tril(np.ones((sub, sub), np.float32), -1), jnp.bfloat16)
    w_hi = w_route.astype(jnp.bfloat16)
    w_lo = (w_route - w_hi.astype(jnp.float32)).astype(jnp.bfloat16)
    w_split = jnp.concatenate([w_hi, w_lo], axis=1)
    return pl.pallas_call(
        _router_kernel, grid=(n // tr,),
        in_specs=[pl.BlockSpec((tr, D_MODEL), lambda i: (i, 0)),
                  pl.BlockSpec((1, D_MODEL), lambda i: (0, 0)),
                  pl.BlockSpec((D_MODEL, 2 * LANES), lambda i: (0, 0)),
                  pl.BlockSpec((1, LANES), lambda i: (0, 0)),
                  pl.BlockSpec((sub, sub), lambda i: (0, 0))],
        out_specs=[pl.BlockSpec((tr, ROW_WORDS), lambda i: (i, 0)),
                   pl.BlockSpec((tr, LANES), lambda i: (i, 0)),
                   pl.BlockSpec((1, LANES), lambda i: (0, 0))],
        out_shape=[jax.ShapeDtypeStruct((n, ROW_WORDS), jnp.uint32),
                   jax.ShapeDtypeStruct((n, LANES), jnp.float32),
                   jax.ShapeDtypeStruct((1, LANES), jnp.float32)],
        scratch_shapes=[pltpu.VMEM((1, LANES), jnp.float32)],
        compiler_params=_cparams("arbitrary"), name="moe_router",
    )(h, gain.reshape(1, D_MODEL), w_split, b_route, tri)


def _dispatch_kernel(dest_ref, x_ref, init_ref, o_ref, sem):
    del init_ref
    t = x_ref.shape[0]

    def row_copy(j):
        return pltpu.make_async_copy(x_ref.at[pl.ds(j, 1)], o_ref.at[pl.ds(dest_ref[0, 0, j], 1)], sem)

    for j in range(t):
        row_copy(j).start(priority=j % 2)

    def wait(j, c):
        row_copy(0).wait()
        return c

    lax.fori_loop(0, t, wait, 0, unroll=DMA_UNROLL)


def moe_dispatch(rows, dest, n_rows):
    n = rows.shape[0]
    t = DISPATCH_TILE
    return pl.pallas_call(
        _dispatch_kernel, grid=(n // t,),
        in_specs=[pl.BlockSpec((1, 1, t), lambda i: (i, 0, 0), memory_space=pltpu.SMEM),
                  pl.BlockSpec((t, ROW_WORDS), lambda i: (i, 0)),
                  pl.BlockSpec(memory_space=pl.ANY)],
        out_specs=pl.BlockSpec(memory_space=pl.ANY),
        out_shape=jax.ShapeDtypeStruct((n_rows, ROW_WORDS), jnp.uint32),
        scratch_shapes=[pltpu.SemaphoreType.DMA(())],
        input_output_aliases={2: 0},
        compiler_params=_cparams("arbitrary"), name="moe_dispatch",
    )(dest.reshape(n // t, 1, t), rows, jnp.zeros((n_rows, ROW_WORDS), jnp.uint32))


def _expert_kernel(ea_ref, eb_ref, na_ref, nb_ref, tr_ref, tv_ref, x_ref,
                   wga_ref, wua_ref, wda_ref, wgb_ref, wub_ref, wdb_ref, o_ref,
                   wga_bf, wua_bf, wda_bf, wgb_bf, wub_bf, wdb_bf):
    del ea_ref, eb_ref, tr_ref
    i = pl.program_id(0)

    @pl.when(na_ref[i] > 0)
    def _():
        wga_bf[...] = wga_ref[0, 0].astype(jnp.bfloat16)
        wua_bf[...] = wua_ref[0, 0].astype(jnp.bfloat16)
        wda_bf[...] = wda_ref[0, 0].astype(jnp.bfloat16)

    @pl.when(nb_ref[i] > 0)
    def _():
        wgb_bf[...] = wgb_ref[0, 0].astype(jnp.bfloat16)
        wub_bf[...] = wub_ref[0, 0].astype(jnp.bfloat16)
        wdb_bf[...] = wdb_ref[0, 0].astype(jnp.bfloat16)

    @pl.when(tv_ref[i] > 0)
    def _():
        x = _unpack_rows(x_ref[:, :PACKED]).astype(jnp.bfloat16)
        gates = pltpu.bitcast(x_ref[:, PACKED:], jnp.float32)

        def mlp(wg, wu, wd):
            g = jnp.dot(x, wg[...], preferred_element_type=jnp.float32)
            u = jnp.dot(x, wu[...], preferred_element_type=jnp.float32)
            hmid = (g * jax.nn.sigmoid(g) * u).astype(jnp.bfloat16)
            return jnp.dot(hmid, wd[...], preferred_element_type=jnp.float32)

        o_ref[...] = _pack_rows(gates[:, 0:1] * mlp(wga_bf, wua_bf, wda_bf)
                                + gates[:, 1:2] * mlp(wgb_bf, wub_bf, wdb_bf))

    @pl.when(tv_ref[i] == 0)
    def _():
        o_ref[...] = jnp.zeros_like(o_ref)


def expert_mlp(x_sorted, tables, layer, w_gate, w_up, w_down):
    p = x_sorted.shape[0]
    tm = EXPERT_TILE

    def weight_spec(shape, slot):
        return pl.BlockSpec((1, 1) + shape, lambda i, ea, eb, na, nb, tr, tv: (layer, (ea, eb)[slot][i], 0, 0))

    up_shape, down_shape = (D_MODEL, EXPERT_FF), (EXPERT_FF, D_MODEL)
    grid_spec = pltpu.PrefetchScalarGridSpec(
        num_scalar_prefetch=6,
        grid=(p // tm,),
        in_specs=[pl.BlockSpec((tm, ROW_WORDS), lambda i, ea, eb, na, nb, tr, tv: (tr[i], 0)),
                  weight_spec(up_shape, 0), weight_spec(up_shape, 0), weight_spec(down_shape, 0),
                  weight_spec(up_shape, 1), weight_spec(up_shape, 1), weight_spec(down_shape, 1)],
        out_specs=pl.BlockSpec((tm, PACKED), lambda i, ea, eb, na, nb, tr, tv: (i, 0)),
        scratch_shapes=[pltpu.VMEM(up_shape, jnp.bfloat16), pltpu.VMEM(up_shape, jnp.bfloat16),
                        pltpu.VMEM(down_shape, jnp.bfloat16)] * 2,
    )
    return pl.pallas_call(
        _expert_kernel, grid_spec=grid_spec,
        out_shape=jax.ShapeDtypeStruct((p, PACKED), jnp.uint32),
        compiler_params=_cparams("arbitrary"), name="expert_mlp",
    )(*tables, x_sorted, w_gate, w_up, w_down, w_gate, w_up, w_down)


def _combine_kernel(dcur_ref, dnext_ref, h_ref, gain_ref, y_ref, o_ref, y_buf, sem, *, final_norm):
    i = pl.program_id(0)
    last = pl.num_programs(0) - 1
    t = h_ref.shape[0]
    slot = i % 2

    def row_copy(dest_ref, s, j):
        return pltpu.make_async_copy(y_ref.at[pl.ds(dest_ref[0, 0, j], 1)],
                                     y_buf.at[s, pl.ds(j, 1)], sem.at[s])

    def start_all(dest_ref, s):
        for j in range(t):
            row_copy(dest_ref, s, j).start(priority=j % 2)

    @pl.when(i == 0)
    def _():
        start_all(dcur_ref, 0)

    for s in range(2):
        @pl.when((i < last) & (slot == 1 - s))
        def _():
            start_all(dnext_ref, s)

    lax.fori_loop(0, t, lambda j, c: (row_copy(dcur_ref, slot, 0).wait(), c)[1], 0, unroll=DMA_UNROLL)
    out = h_ref[...] + _unpack_rows(y_buf[slot])
    if final_norm:
        out = _rms(out, gain_ref[...])
    o_ref[...] = out


def moe_combine(h, y, dest, final_gain):
    n = h.shape[0]
    t = COMBINE_TILE
    nt = n // t
    dest3 = dest.reshape(nt, 1, t)
    final_norm = final_gain is not None
    gain = final_gain if final_norm else jnp.ones((D_MODEL,), jnp.float32)
    return pl.pallas_call(
        functools.partial(_combine_kernel, final_norm=final_norm), grid=(nt,),
        in_specs=[pl.BlockSpec((1, 1, t), lambda i: (i, 0, 0), memory_space=pltpu.SMEM),
                  pl.BlockSpec((1, 1, t), lambda i: (jnp.minimum(i + 1, nt - 1), 0, 0),
                               memory_space=pltpu.SMEM),
                  pl.BlockSpec((t, D_MODEL), lambda i: (i, 0)),
                  pl.BlockSpec((1, D_MODEL), lambda i: (0, 0)),
                  pl.BlockSpec(memory_space=pl.ANY)],
        out_specs=pl.BlockSpec((t, D_MODEL), lambda i: (i, 0)),
        out_shape=jax.ShapeDtypeStruct((n, D_MODEL), jnp.float32),
        scratch_shapes=[pltpu.VMEM((2, t, PACKED), jnp.uint32), pltpu.SemaphoreType.DMA((2,))],
        compiler_params=_cparams("arbitrary"), name="moe_combine",
    )(dest3, dest3, h, gain.reshape(1, D_MODEL), y)


def hierarchical_moe(h, gain, w_route, b_route, layer, w_gate, w_up, w_down, final_gain=None):
    n = h.shape[0]
    tm = EXPERT_TILE
    rows, route, counts = moe_router(h, gain, w_route, b_route)
    counts = counts[0, :N_CLASSES].astype(jnp.int32)
    padded = ((counts + tm - 1) // tm) * tm
    seg_end = jnp.cumsum(padded)
    seg_start = seg_end - padded
    cls = route[:, R_CLASS].astype(jnp.int32)
    rank = route[:, R_RANK].astype(jnp.int32)
    class_ids = jnp.arange(N_CLASSES, dtype=jnp.int32)
    dest = jnp.sum(jnp.where(cls[:, None] == class_ids[None, :], seg_start[None, :], 0), axis=-1) + rank
    n_rows = n + N_CLASSES * tm
    n_tiles = n_rows // tm
    used = seg_end[-1] // tm
    tile_id = jnp.arange(n_tiles, dtype=jnp.int32)
    tile_row = jnp.minimum(tile_id, used - 1).astype(jnp.int32)
    tile_class = jnp.minimum(jnp.sum((seg_end[None, :] <= (tile_row * tm)[:, None]).astype(jnp.int32), axis=1),
                             N_CLASSES - 1)
    tile_valid = (tile_id < used).astype(jnp.int32)
    slots = np.asarray([[g * EXPERTS_PER_GROUP + s for s in pair]
                        for g in range(N_EXPERT_GROUPS) for pair in PAIR_SLOTS], np.int32)
    tile_onehot = (tile_class[:, None] == class_ids[None, :]).astype(jnp.int32)
    expert_a = jnp.sum(tile_onehot * slots[None, :, 0], axis=1).astype(jnp.int32)
    expert_b = jnp.sum(tile_onehot * slots[None, :, 1], axis=1).astype(jnp.int32)

    def changed(e):
        return jnp.concatenate([jnp.ones((1,), jnp.int32), (e[1:] != e[:-1]).astype(jnp.int32)])

    tables = (expert_a, expert_b, changed(expert_a), changed(expert_b), tile_row, tile_valid)
    x_sorted = moe_dispatch(rows, dest, n_rows)
    y = expert_mlp(x_sorted, tables, layer, w_gate, w_up, w_down)
    return moe_combine(h, y, dest, final_gain)


def _permute_qk_columns(w_qkv):
    w = w_qkv.astype(jnp.bfloat16).reshape(D_MODEL, N_ATTN_GROUPS, 3, N_HEADS, 2, HALF_DIM)
    qk = jnp.transpose(w[:, :, 0:2], (0, 1, 2, 4, 3, 5)).reshape(D_MODEL, N_ATTN_GROUPS, 2, D_MODEL)
    v = w[:, :, 2:3].reshape(D_MODEL, N_ATTN_GROUPS, 1, D_MODEL)
    return jnp.concatenate([qk, v], axis=2).reshape(D_MODEL, QKV_WIDTH)


def kernel(x, positions, norm_mix, norm_ffn, norm_final, conv_w_in, conv_w, conv_w_out,
           attn_w_qkv, attn_w_o, w_coarse, b_coarse, w_fine, b_fine, w_gate, w_up, w_down):
    batch, seq, d = x.shape
    assert d == D_MODEL
    n = batch * seq
    depth = norm_mix.shape[0]
    bf = jnp.bfloat16
    h = x.reshape(n, d)
    cos, sin = rope_tables(positions.reshape(n))
    for i in range(depth):
        j = i // 2
        if i % 2 == 0:
            h = conv_mixer(h, batch, norm_mix[i], conv_w_in[j].astype(bf), conv_w[j], conv_w_out[j].astype(bf))
        else:
            w_qkv = _permute_qk_columns(attn_w_qkv[j])
            outs, maxima, dens = [], [], []
            for g, (window, dil) in enumerate(DILATED_GROUPS):
                assert window // dil == ATT_BLK
                w_group = w_qkv[:, g * 3 * D_MODEL:(g + 1) * 3 * D_MODEL]
                o, mx, den = dilated_attention_group(
                    qkv_project(h, batch, norm_mix[i], w_group, cos, sin, dil), dil)
                outs.append(o)
                maxima.append(mx)
                dens.append(den)
            h = merge_proj_residual(h, batch, outs, maxima, dens, attn_w_o[j].astype(bf))
        w_route = jnp.concatenate(
            [w_coarse[i], jnp.transpose(w_fine[i], (1, 0, 2)).reshape(d, N_EXPERTS),
             jnp.zeros((d, LANES - N_EXPERT_GROUPS - N_EXPERTS), jnp.float32)], axis=1)
        b_route = jnp.concatenate(
            [b_coarse[i], b_fine[i].reshape(-1),
             jnp.zeros((LANES - N_EXPERT_GROUPS - N_EXPERTS,), jnp.float32)]).reshape(1, LANES)
        h = hierarchical_moe(h, norm_ffn[i], w_route, b_route,
                             i, w_gate, w_up, w_down,
                             norm_final if i == depth - 1 else None)
    return h.reshape(batch, seq, d)
```

```python
import functools
import math

import numpy as np
import jax
import jax.numpy as jnp
from jax import lax
from jax.experimental import pallas as pl
from jax.experimental.pallas import tpu as pltpu

D_MODEL = 1024
NORM_EPS = 1e-6
CONV_WIDTH = 3
HEAD_DIM = 64
HALF_DIM = HEAD_DIM // 2
N_HEADS = D_MODEL // HEAD_DIM
DILATED_GROUPS = ((128, 1), (512, 4), (2048, 16))
N_ATTN_GROUPS = len(DILATED_GROUPS)
QKV_WIDTH = N_ATTN_GROUPS * 3 * D_MODEL
ROPE_THETA = 10000.0
N_EXPERT_GROUPS = 4
EXPERTS_PER_GROUP = 4
N_EXPERTS = N_EXPERT_GROUPS * EXPERTS_PER_GROUP
EXPERT_FF = D_MODEL // 2
PACKED = D_MODEL // 2

LANES = 128
ATT_BLK = 128
ATT_BLOCKS_PER_STEP = 16
VMEM_LIMIT_BYTES = 56 * 1024 * 1024
ROW_TILE = 1024
QKV_TILE = 1024
PROJ_TILE = 1024
CONV_TILE = 1024
EXPERT_TILE = 512
DISPATCH_TILE = 1024
COMBINE_TILE = 512
RANK_BLOCK = 256
DMA_UNROLL = 8
NEG_INF = float("-inf")
LOG2_E = math.log2(math.e)
R_CLASS, R_RANK = range(2)
PAIR_SLOTS = ((0, 1), (0, 2), (0, 3), (1, 3), (1, 2), (3, 2))
N_CLASSES = N_EXPERT_GROUPS * len(PAIR_SLOTS)
ROW_WORDS = PACKED + LANES


def _cparams(*sem):
    return pltpu.CompilerParams(dimension_semantics=sem, vmem_limit_bytes=VMEM_LIMIT_BYTES)


def _rms(x, g):
    return x * lax.rsqrt(jnp.mean(x * x, axis=-1, keepdims=True) + NORM_EPS) * g


def _pack_rows(x):
    lo = pltpu.bitcast(x[:, :PACKED].astype(jnp.bfloat16).astype(jnp.float32), jnp.uint32)
    hi = pltpu.bitcast(x[:, PACKED:].astype(jnp.bfloat16).astype(jnp.float32), jnp.uint32)
    return (lo >> 16) | (hi & jnp.uint32(0xFFFF0000))


def _unpack_rows(u):
    lo = pltpu.bitcast(u << 16, jnp.float32)
    hi = pltpu.bitcast(u & jnp.uint32(0xFFFF0000), jnp.float32)
    return jnp.concatenate([lo, hi], axis=1)


def _rope_kernel(pos_ref, freq_ref, cos_ref, sin_ref):
    ang = pos_ref[...].astype(jnp.float32) * freq_ref[...]
    cos_ref[...] = jnp.cos(ang)
    sin_ref[...] = jnp.sin(ang)


def rope_tables(positions):
    n = positions.shape[0]
    inv_freq = (ROPE_THETA ** (-np.arange(0, HEAD_DIM, 2, dtype=np.float32) / HEAD_DIM)).astype(np.float32)
    freq = jnp.asarray(np.tile(inv_freq, LANES // HALF_DIM)[None, :])
    tr = ROW_TILE
    return pl.pallas_call(
        _rope_kernel,
        grid=(n // tr,),
        in_specs=[pl.BlockSpec((tr, 1), lambda i: (i, 0)),
                  pl.BlockSpec((1, LANES), lambda i: (0, 0))],
        out_specs=[pl.BlockSpec((tr, LANES), lambda i: (i, 0)),
                   pl.BlockSpec((tr, LANES), lambda i: (i, 0))],
        out_shape=[jax.ShapeDtypeStruct((n, LANES), jnp.float32)] * 2,
        compiler_params=_cparams("parallel"),
        name="rope_tables",
    )(positions.reshape(n, 1), freq)


def _conv_mixer_kernel(h_ref, g_ref, win_ref, cw_ref, wout_ref, o_ref, hn_buf, u_buf, v_buf):
    t = h_ref.shape[0]

    @pl.when(pl.program_id(1) == 0)
    def _():
        u_buf[0:8, :] = jnp.zeros((8, D_MODEL), jnp.float32)

    x = h_ref[...]
    hn_buf[...] = _rms(x, g_ref[...]).astype(jnp.bfloat16)
    cw = cw_ref[...]
    chunk = 512
    for c in range(0, D_MODEL, chunk):
        hn = hn_buf[...]
        gate_c = jnp.dot(hn, win_ref[:, D_MODEL + c:D_MODEL + c + chunk], preferred_element_type=jnp.float32)
        hh = jnp.dot(hn, win_ref[:, 2 * D_MODEL + c:2 * D_MODEL + c + chunk], preferred_element_type=jnp.float32)
        u_buf[8:8 + t, c:c + chunk] = gate_c * hh
        conv = (cw[0:1, c:c + chunk] * u_buf[6:6 + t, c:c + chunk]
                + cw[1:2, c:c + chunk] * u_buf[7:7 + t, c:c + chunk]
                + cw[2:3, c:c + chunk] * u_buf[8:8 + t, c:c + chunk])
        gate_b = jnp.dot(hn, win_ref[:, c:c + chunk], preferred_element_type=jnp.float32)
        v_buf[:, c:c + chunk] = (gate_b * conv).astype(jnp.bfloat16)
    u_buf[0:8, :] = u_buf[t:t + 8, :]
    o_ref[...] = x + jnp.dot(v_buf[...], wout_ref[...], preferred_element_type=jnp.float32)


def conv_mixer(h, batch, gain, w_in, conv_w, w_out):
    n = h.shape[0]
    seq = n // batch
    t = CONV_TILE
    ns = seq // t
    row = pl.BlockSpec((t, D_MODEL), lambda b, s: (b * ns + s, 0))
    return pl.pallas_call(
        _conv_mixer_kernel,
        grid=(batch, ns),
        in_specs=[row,
                  pl.BlockSpec((1, D_MODEL), lambda b, s: (0, 0)),
                  pl.BlockSpec((D_MODEL, 3 * D_MODEL), lambda b, s: (0, 0)),
                  pl.BlockSpec((CONV_WIDTH, D_MODEL), lambda b, s: (0, 0)),
                  pl.BlockSpec((D_MODEL, D_MODEL), lambda b, s: (0, 0))],
        out_specs=row,
        out_shape=jax.ShapeDtypeStruct((n, D_MODEL), jnp.float32),
        scratch_shapes=[pltpu.VMEM((t, D_MODEL), jnp.bfloat16),
                        pltpu.VMEM((t + 8, D_MODEL), jnp.float32),
                        pltpu.VMEM((t, D_MODEL), jnp.bfloat16)],
        compiler_params=_cparams("arbitrary", "arbitrary"),
        name="conv_mixer",
    )(h, gain.reshape(1, D_MODEL), w_in, conv_w, w_out)


def _qkv_kernel(h_ref, g_ref, w_ref, cos_ref, sin_ref, o_ref, z_buf, *, dil):
    t = h_ref.shape[0]
    half = D_MODEL // 2
    reps = half // LANES
    rows = t // dil

    def residue_major(x):
        if dil == 1:
            return x
        planes = x.shape[1] // LANES
        for c in range(planes):
            z_buf[c] = x[:, c * LANES:(c + 1) * LANES]
        return jnp.concatenate(
            [jnp.concatenate([z_buf[c, pl.ds(r, rows, stride=dil), :] for r in range(dil)], axis=0)
             for c in range(planes)], axis=1)

    cos128 = residue_major(cos_ref[...])
    sin128 = residue_major(sin_ref[...])
    cos = jnp.concatenate([cos128] * reps, axis=1)
    sin = jnp.concatenate([sin128] * reps, axis=1)
    hn = residue_major(_rms(h_ref[...], g_ref[...])).astype(jnp.bfloat16)
    for part in range(3):
        z = jnp.dot(hn, w_ref[:, part * D_MODEL:(part + 1) * D_MODEL], preferred_element_type=jnp.float32)
        if part < 2:
            z1 = z[:, :half]
            z2 = z[:, half:]
            z = jnp.concatenate([z1 * cos - z2 * sin, z2 * cos + z1 * sin], axis=1)
        if part == 0:
            z = z * (HEAD_DIM ** -0.5 * LOG2_E)
        z = z.astype(o_ref.dtype)
        for r in range(dil):
            o_ref[0, r, :, part * D_MODEL:(part + 1) * D_MODEL] = z[r * rows:(r + 1) * rows]


def qkv_project(h, batch, gain, w_group, cos, sin, dil):
    n = h.shape[0]
    seq = n // batch
    t = QKV_TILE
    ns = seq // t
    row = lambda width: pl.BlockSpec((t, width), lambda b, s: (b * ns + s, 0))
    return pl.pallas_call(
        functools.partial(_qkv_kernel, dil=dil),
        grid=(batch, ns),
        in_specs=[row(D_MODEL),
                  pl.BlockSpec((1, D_MODEL), lambda b, s: (0, 0)),
                  pl.BlockSpec((D_MODEL, 3 * D_MODEL), lambda b, s: (0, 0)),
                  row(LANES), row(LANES)],
        out_specs=pl.BlockSpec((1, dil, t // dil, 3 * D_MODEL), lambda b, s: (b, 0, s, 0)),
        out_shape=jax.ShapeDtypeStruct((batch, dil, seq // dil, 3 * D_MODEL), jnp.bfloat16),
        scratch_shapes=[pltpu.VMEM((D_MODEL // LANES, t, LANES), jnp.float32)],
        compiler_params=_cparams("parallel", "parallel"),
        name=f"qkv_project_d{dil}",
    )(h, gain.reshape(1, D_MODEL), w_group, cos, sin)


def _attention_kernel(q_ref, k_ref, v_ref, o_ref, m_ref, d_ref, k_win, v_win, bias_buf):
    blk = ATT_BLK
    n = pl.program_id(2)
    n_res = q_ref.shape[1]
    n_sub = q_ref.shape[2] // blk
    heads_per_slab = LANES // HALF_DIM
    n_slabs = N_HEADS // heads_per_slab
    rows = heads_per_slab * blk

    @pl.when(n == 0)
    def _():
        for res in range(n_res):
            k_win[res, 0:blk, :] = jnp.zeros((blk, D_MODEL), k_win.dtype)
            v_win[res, 0:blk, :] = jnp.zeros((blk, D_MODEL), v_win.dtype)

    for res in range(n_res):
        k_win[res, blk:(n_sub + 1) * blk, :] = k_ref[0, res]
        v_win[res, blk:(n_sub + 1) * blk, :] = v_ref[0, res]

    @pl.when((pl.program_id(0) == 0) & (pl.program_id(1) == 0) & (n == 0))
    def _():
        qi = lax.broadcasted_iota(jnp.int32, (rows, 2 * blk), 0) % blk
        ki = lax.broadcasted_iota(jnp.int32, (rows, 2 * blk), 1)
        own_ok = (ki >= blk) & (ki - blk <= qi)
        back_ok = (ki < blk) & (ki >= qi)
        bias_buf[0] = jnp.where(own_ok, 0.0, NEG_INF).astype(jnp.float32)
        bias_buf[1] = jnp.where(own_ok | back_ok, 0.0, NEG_INF).astype(jnp.float32)

    first_plane = jnp.minimum(n, 1)

    half = D_MODEL // 2
    lane2 = lax.broadcasted_iota(jnp.int32, (blk, 2 * LANES), 1)
    lane1 = lax.broadcasted_iota(jnp.int32, (blk, LANES), 1)

    def scores(unit):
        res, sub, m = unit
        q_rows = slice(sub * blk, (sub + 1) * blk)
        k_rows = slice(sub * blk, (sub + 2) * blk)
        q_slab = jnp.concatenate([q_ref[0, res, q_rows, m * LANES:(m + 1) * LANES],
                                  q_ref[0, res, q_rows, half + m * LANES:half + (m + 1) * LANES]], axis=1)
        k_slab = jnp.concatenate([k_win[res, k_rows, m * LANES:(m + 1) * LANES],
                                  k_win[res, k_rows, half + m * LANES:half + (m + 1) * LANES]], axis=1)
        zero = jnp.zeros_like(q_slab)
        q_heads = jnp.concatenate(
            [jnp.where(((lane2 % LANES) // HALF_DIM) == a, q_slab, zero) for a in range(heads_per_slab)], axis=0)
        bias = bias_buf[first_plane] if sub == 0 else bias_buf[1]
        return lax.dot_general(q_heads, k_slab, (((1,), (1,)), ((), ())),
                               preferred_element_type=jnp.float32) + bias

    units = [(res, sub, m) for res in range(n_res) for sub in range(n_sub) for m in range(n_slabs)]
    m_ref[0] = jnp.zeros(m_ref.shape[1:], jnp.float32)
    d_ref[0] = jnp.ones(d_ref.shape[1:], jnp.float32)
    s_next = scores(units[0])
    for idx, (res, sub, m) in enumerate(units):
        s = s_next
        if idx + 1 < len(units):
            s_next = scores(units[idx + 1])
        mx = jnp.max(s, axis=-1, keepdims=True)
        p = jnp.exp2(s - mx)
        den = jnp.sum(p, axis=-1, keepdims=True)
        pb = p.astype(jnp.bfloat16)
        q_rows = slice(sub * blk, (sub + 1) * blk)
        k_rows = slice(sub * blk, (sub + 2) * blk)
        for a in range(heads_per_slab):
            hd = m * heads_per_slab + a
            m_ref[0, res, q_rows, hd:hd + 1] = mx[a * blk:(a + 1) * blk]
            d_ref[0, res, q_rows, hd:hd + 1] = den[a * blk:(a + 1) * blk]
        for pair in range(heads_per_slab // 2):
            col = (m * heads_per_slab + pair * 2) * HEAD_DIM
            r0 = pair * 2 * blk
            o2 = jnp.dot(pb[r0:r0 + 2 * blk], v_win[res, k_rows, col:col + LANES],
                         preferred_element_type=jnp.float32)
            o_ref[0, res, q_rows, col:col + LANES] = jnp.where(
                lane1 < HEAD_DIM, o2[:blk], o2[blk:]).astype(o_ref.dtype)
    for res in range(n_res):
        k_win[res, 0:blk, :] = k_win[res, n_sub * blk:(n_sub + 1) * blk, :]
        v_win[res, 0:blk, :] = v_win[res, n_sub * blk:(n_sub + 1) * blk, :]


def dilated_attention_group(qkv_g, dil):
    batch, _, length, _ = qkv_g.shape
    blk = ATT_BLK
    rows = min(ATT_BLOCKS_PER_STEP * blk, length)
    nstep = length // rows
    n_res = min(dil, max(1, ATT_BLOCKS_PER_STEP * blk // length))
    assert dil % n_res == 0

    def part_spec(part):
        return pl.BlockSpec((1, n_res, rows, D_MODEL), lambda b, r, i: (b, r, i, part))

    return pl.pallas_call(
        _attention_kernel,
        grid=(batch, dil // n_res, nstep),
        in_specs=[part_spec(0), part_spec(1), part_spec(2)],
        out_specs=[pl.BlockSpec((1, n_res, rows, D_MODEL), lambda b, r, i: (b, r, i, 0)),
                   pl.BlockSpec((1, n_res, rows, LANES), lambda b, r, i: (b, r, i, 0)),
                   pl.BlockSpec((1, n_res, rows, LANES), lambda b, r, i: (b, r, i, 0))],
        out_shape=[jax.ShapeDtypeStruct((batch, dil, length, D_MODEL), jnp.bfloat16),
                   jax.ShapeDtypeStruct((batch, dil, length, LANES), jnp.float32),
                   jax.ShapeDtypeStruct((batch, dil, length, LANES), jnp.float32)],
        scratch_shapes=[pltpu.VMEM((n_res, rows + blk, D_MODEL), jnp.bfloat16),
                        pltpu.VMEM((n_res, rows + blk, D_MODEL), jnp.bfloat16),
                        pltpu.VMEM((2, (LANES // HALF_DIM) * blk, 2 * blk), jnp.float32)],
        compiler_params=_cparams("arbitrary", "arbitrary", "arbitrary"),
        name=f"dilated_attention_d{dil}",
    )(qkv_g, qkv_g, qkv_g)


def _merge_proj_kernel(*refs, dils):
    ng = len(dils)
    h_ref = refs[0]
    o_refs = refs[1:1 + ng]
    m_refs = refs[1 + ng:1 + 2 * ng]
    d_refs = refs[1 + 2 * ng:1 + 3 * ng]
    e_ref, w_ref, out_ref, o_buf, l_buf = refs[1 + 3 * ng:]
    t = h_ref.shape[0]

    def token_major(ref, dil):
        if dil == 1:
            return ref[0, 0]
        for r in range(dil):
            l_buf[pl.ds(r, t // dil, stride=dil), :] = ref[0, r]
        return l_buf[...]

    maxima = [token_major(m_refs[g], dil) for g, dil in enumerate(dils)]
    lses = [maxima[g] + jnp.log2(token_major(d_refs[g], dil)) for g, dil in enumerate(dils)]
    top = functools.reduce(jnp.maximum, lses)
    inv = 1.0 / functools.reduce(jnp.add, [jnp.exp2(l - top) for l in lses])
    merged = jnp.zeros((t, D_MODEL), jnp.float32)
    for g, dil in enumerate(dils):
        wg = jnp.exp2(maxima[g] - top) * inv
        hi = wg.astype(jnp.bfloat16)
        lo = (wg - hi.astype(jnp.float32)).astype(jnp.bfloat16)
        wexp = jnp.dot(jnp.concatenate([hi, lo], axis=1), e_ref[...], preferred_element_type=jnp.float32)
        if dil == 1:
            og = o_refs[g][0, 0].astype(jnp.float32)
        else:
            for r in range(dil):
                part = o_refs[g][0, r].astype(jnp.float32)
                for c in range(D_MODEL // LANES):
                    o_buf[c, pl.ds(r, t // dil, stride=dil), :] = part[:, c * LANES:(c + 1) * LANES]
            og = jnp.concatenate([o_buf[c] for c in range(D_MODEL // LANES)], axis=1)
        merged = merged + wexp * og
    out_ref[...] = h_ref[...] + jnp.dot(merged.astype(jnp.bfloat16), w_ref[...],
                                        preferred_element_type=jnp.float32)


def merge_proj_residual(h, batch, outs, maxima, dens, w_o):
    n = h.shape[0]
    seq = n // batch
    t = PROJ_TILE
    ns = seq // t
    dils = tuple(o.shape[1] for o in outs)
    row = pl.BlockSpec((t, D_MODEL), lambda b, s: (b * ns + s, 0))
    expand = np.zeros((2 * LANES, D_MODEL), np.float32)
    for hd in range(N_HEADS):
        expand[hd, hd * HEAD_DIM:(hd + 1) * HEAD_DIM] = 1.0
        expand[LANES + hd, hd * HEAD_DIM:(hd + 1) * HEAD_DIM] = 1.0
    in_specs = [row]
    in_specs += [pl.BlockSpec((1, d, t // d, D_MODEL), lambda b, s: (b, 0, s, 0)) for d in dils]
    in_specs += [pl.BlockSpec((1, d, t // d, LANES), lambda b, s: (b, 0, s, 0)) for d in dils] * 2
    in_specs += [pl.BlockSpec((2 * LANES, D_MODEL), lambda b, s: (0, 0)),
                 pl.BlockSpec((D_MODEL, D_MODEL), lambda b, s: (0, 0))]
    return pl.pallas_call(
        functools.partial(_merge_proj_kernel, dils=dils),
        grid=(batch, ns), in_specs=in_specs, out_specs=row,
        out_shape=jax.ShapeDtypeStruct((n, D_MODEL), jnp.float32),
        scratch_shapes=[pltpu.VMEM((D_MODEL // LANES, t, LANES), jnp.float32),
                        pltpu.VMEM((t, LANES), jnp.float32)],
        compiler_params=_cparams("parallel", "parallel"), name="attn_merge_out_proj",
    )(h, *outs, *maxima, *dens, jnp.asarray(expand, jnp.bfloat16), w_o)


def _router_kernel(h_ref, g_ref, w_ref, b_ref, tri_ref, hn_ref, route_ref, count_ref, run_ref):
    @pl.when(pl.program_id(0) == 0)
    def _():
        run_ref[...] = jnp.zeros_like(run_ref)

    hn = _rms(h_ref[...], g_ref[...])
    hn_ref[:, :PACKED] = _pack_rows(hn)
    hn_hi = hn.astype(jnp.bfloat16)
    hn_lo = (hn - hn_hi.astype(jnp.float32)).astype(jnp.bfloat16)
    both = jnp.dot(hn_hi, w_ref[...], preferred_element_type=jnp.float32)
    logits = (both[:, :LANES] + both[:, LANES:]
              + jnp.dot(hn_lo, w_ref[:, :LANES], preferred_element_type=jnp.float32) + b_ref[...])
    lane = lax.broadcasted_iota(jnp.int32, logits.shape, 1)
    big = jnp.int32(LANES)

    def first_argmax(vals, vmax):
        return jnp.min(jnp.where(vals == vmax, lane, big), axis=-1, keepdims=True)

    coarse = jnp.where(lane < N_EXPERT_GROUPS, logits, NEG_INF)
    cmax = jnp.max(coarse, axis=-1, keepdims=True)
    p_top = 1.0 / jnp.sum(jnp.exp(coarse - cmax), axis=-1, keepdims=True)
    g_top = first_argmax(coarse, cmax)
    lo = N_EXPERT_GROUPS + g_top * EXPERTS_PER_GROUP
    fine = jnp.where((lane >= lo) & (lane < lo + EXPERTS_PER_GROUP), logits, NEG_INF)
    v1 = jnp.max(fine, axis=-1, keepdims=True)
    i1 = first_argmax(fine, v1)
    fine2 = jnp.where(lane == i1, NEG_INF, fine)
    v2 = jnp.max(fine2, axis=-1, keepdims=True)
    i2 = first_argmax(fine2, v2)
    e2 = jnp.exp(v2 - v1)
    w1 = 1.0 / (1.0 + e2)
    w2 = e2 / (1.0 + e2)
    first_lo = i1 < i2
    a = jnp.where(first_lo, i1, i2) - lo
    b = jnp.where(first_lo, i2, i1) - lo
    pair = jnp.where(a == 0, b - 1, jnp.where(a == 1, jnp.where(b == 3, 3, 4), 5))
    cls = g_top * len(PAIR_SLOTS) + pair
    gate_lo = p_top * jnp.where(first_lo, w1, w2)
    gate_hi = p_top * jnp.where(first_lo, w2, w1)
    swapped = pair == 5
    gate_a = jnp.where(swapped, gate_hi, gate_lo)
    gate_b = jnp.where(swapped, gate_lo, gate_hi)
    gates = jnp.where(lane == 0, gate_a, jnp.where(lane == 1, gate_b, 0.0))
    hn_ref[:, PACKED:] = pltpu.bitcast(gates, jnp.uint32)

    sel = lane == cls
    onehot = jnp.where(sel, 1.0, 0.0)
    sub = tri_ref.shape[0]
    run = run_ref[...]
    befores = []
    for r0 in range(0, onehot.shape[0], sub):
        oh = onehot[r0:r0 + sub]
        befores.append(jnp.dot(tri_ref[...], oh.astype(jnp.bfloat16), preferred_element_type=jnp.float32) + run)
        run = run + jnp.sum(oh, axis=0, keepdims=True)
    before = jnp.concatenate(befores, axis=0)
    rank = jnp.sum(jnp.where(sel, before, 0.0), axis=-1, keepdims=True)
    run_ref[...] = run
    count_ref[...] = run

    route = jnp.where(lane == R_CLASS, cls.astype(jnp.float32), 0.0)
    route = jnp.where(lane == R_RANK, rank, route)
    route_ref[...] = route


def moe_router(h, gain, w_route, b_route):
    n = h.shape[0]
    tr = ROW_TILE
    sub = RANK_BLOCK
    tri = jnp.asarray(np.tril(np.ones((sub, sub), np.float32), -1), jnp.bfloat16)
    w_hi = w_route.astype(jnp.bfloat16)
    w_lo = (w_route - w_hi.astype(jnp.float32)).astype(jnp.bfloat16)
    w_split = jnp.concatenate([w_hi, w_lo], axis=1)
    return pl.pallas_call(
        _router_kernel, grid=(n // tr,),
        in_specs=[pl.BlockSpec((tr, D_MODEL), lambda i: (i, 0)),
                  pl.BlockSpec((1, D_MODEL), lambda i: (0, 0)),
                  pl.BlockSpec((D_MODEL, 2 * LANES), lambda i: (0, 0)),
                  pl.BlockSpec((1, LANES), lambda i: (0, 0)),
                  pl.BlockSpec((sub, sub), lambda i: (0, 0))],
        out_specs=[pl.BlockSpec((tr, ROW_WORDS), lambda i: (i, 0)),
                   pl.BlockSpec((tr, LANES), lambda i: (i, 0)),
                   pl.BlockSpec((1, LANES), lambda i: (0, 0))],
        out_shape=[jax.ShapeDtypeStruct((n, ROW_WORDS), jnp.uint32),
                   jax.ShapeDtypeStruct((n, LANES), jnp.float32),
                   jax.ShapeDtypeStruct((1, LANES), jnp.float32)],
        scratch_shapes=[pltpu.VMEM((1, LANES), jnp.float32)],
        compiler_params=_cparams("arbitrary"), name="moe_router",
    )(h, gain.reshape(1, D_MODEL), w_split, b_route, tri)


def _dispatch_kernel(dest_ref, x_ref, init_ref, o_ref, sem):
    del init_ref
    t = x_ref.shape[0]

    def row_copy(j):
        return pltpu.make_async_copy(x_ref.at[pl.ds(j, 1)], o_ref.at[pl.ds(dest_ref[0, 0, j], 1)], sem)

    for j in range(t):
        row_copy(j).start(priority=j % 2)

    def wait(j, c):
        row_copy(0).wait()
        return c

    lax.fori_loop(0, t, wait, 0, unroll=DMA_UNROLL)


def moe_dispatch(rows, dest, n_rows):
    n = rows.shape[0]
    t = DISPATCH_TILE
    return pl.pallas_call(
        _dispatch_kernel, grid=(n // t,),
        in_specs=[pl.BlockSpec((1, 1, t), lambda i: (i, 0, 0), memory_space=pltpu.SMEM),
                  pl.BlockSpec((t, ROW_WORDS), lambda i: (i, 0)),
                  pl.BlockSpec(memory_space=pl.ANY)],
        out_specs=pl.BlockSpec(memory_space=pl.ANY),
        out_shape=jax.ShapeDtypeStruct((n_rows, ROW_WORDS), jnp.uint32),
        scratch_shapes=[pltpu.SemaphoreType.DMA(())],
        input_output_aliases={2: 0},
        compiler_params=_cparams("arbitrary"), name="moe_dispatch",
    )(dest.reshape(n // t, 1, t), rows, jnp.zeros((n_rows, ROW_WORDS), jnp.uint32))


def _expert_kernel(ea_ref, eb_ref, na_ref, nb_ref, tr_ref, tv_ref, x_ref,
                   wga_ref, wua_ref, wda_ref, wgb_ref, wub_ref, wdb_ref, o_ref,
                   wga_bf, wua_bf, wda_bf, wgb_bf, wub_bf, wdb_bf):
    del ea_ref, eb_ref, tr_ref
    i = pl.program_id(0)

    @pl.when(na_ref[i] > 0)
    def _():
        wga_bf[...] = wga_ref[0, 0].astype(jnp.bfloat16)
        wua_bf[...] = wua_ref[0, 0].astype(jnp.bfloat16)
        wda_bf[...] = wda_ref[0, 0].astype(jnp.bfloat16)

    @pl.when(nb_ref[i] > 0)
    def _():
        wgb_bf[...] = wgb_ref[0, 0].astype(jnp.bfloat16)
        wub_bf[...] = wub_ref[0, 0].astype(jnp.bfloat16)
        wdb_bf[...] = wdb_ref[0, 0].astype(jnp.bfloat16)

    @pl.when(tv_ref[i] > 0)
    def _():
        x = _unpack_rows(x_ref[:, :PACKED]).astype(jnp.bfloat16)
        gates = pltpu.bitcast(x_ref[:, PACKED:], jnp.float32)

        def mlp(wg, wu, wd):
            g = jnp.dot(x, wg[...], preferred_element_type=jnp.float32)
            u = jnp.dot(x, wu[...], preferred_element_type=jnp.float32)
            hmid = (g * jax.nn.sigmoid(g) * u).astype(jnp.bfloat16)
            return jnp.dot(hmid, wd[...], preferred_element_type=jnp.float32)

        o_ref[...] = _pack_rows(gates[:, 0:1] * mlp(wga_bf, wua_bf, wda_bf)
                                + gates[:, 1:2] * mlp(wgb_bf, wub_bf, wdb_bf))

    @pl.when(tv_ref[i] == 0)
    def _():
        o_ref[...] = jnp.zeros_like(o_ref)


def expert_mlp(x_sorted, tables, layer, w_gate, w_up, w_down):
    p = x_sorted.shape[0]
    tm = EXPERT_TILE

    def weight_spec(shape, slot):
        return pl.BlockSpec((1, 1) + shape, lambda i, ea, eb, na, nb, tr, tv: (layer, (ea, eb)[slot][i], 0, 0))

    up_shape, down_shape = (D_MODEL, EXPERT_FF), (EXPERT_FF, D_MODEL)
    grid_spec = pltpu.PrefetchScalarGridSpec(
        num_scalar_prefetch=6,
        grid=(p // tm,),
        in_specs=[pl.BlockSpec((tm, ROW_WORDS), lambda i, ea, eb, na, nb, tr, tv: (tr[i], 0)),
                  weight_spec(up_shape, 0), weight_spec(up_shape, 0), weight_spec(down_shape, 0),
                  weight_spec(up_shape, 1), weight_spec(up_shape, 1), weight_spec(down_shape, 1)],
        out_specs=pl.BlockSpec((tm, PACKED), lambda i, ea, eb, na, nb, tr, tv: (i, 0)),
        scratch_shapes=[pltpu.VMEM(up_shape, jnp.bfloat16), pltpu.VMEM(up_shape, jnp.bfloat16),
                        pltpu.VMEM(down_shape, jnp.bfloat16)] * 2,
    )
    return pl.pallas_call(
        _expert_kernel, grid_spec=grid_spec,
        out_shape=jax.ShapeDtypeStruct((p, PACKED), jnp.uint32),
        compiler_params=_cparams("arbitrary"), name="expert_mlp",
    )(*tables, x_sorted, w_gate, w_up, w_down, w_gate, w_up, w_down)


def _combine_kernel(dcur_ref, dnext_ref, h_ref, gain_ref, y_ref, o_ref, y_buf, sem, *, final_norm):
    i = pl.program_id(0)
    last = pl.num_programs(0) - 1
    t = h_ref.shape[0]
    slot = i % 2

    def row_copy(dest_ref, s, j):
        return pltpu.make_async_copy(y_ref.at[pl.ds(dest_ref[0, 0, j], 1)],
                                     y_buf.at[s, pl.ds(j, 1)], sem.at[s])

    def start_all(dest_ref, s):
        for j in range(t):
            row_copy(dest_ref, s, j).start(priority=j % 2)

    @pl.when(i == 0)
    def _():
        start_all(dcur_ref, 0)

    for s in range(2):
        @pl.when((i < last) & (slot == 1 - s))
        def _():
            start_all(dnext_ref, s)

    lax.fori_loop(0, t, lambda j, c: (row_copy(dcur_ref, slot, 0).wait(), c)[1], 0, unroll=DMA_UNROLL)
    out = h_ref[...] + _unpack_rows(y_buf[slot])
    if final_norm:
        out = _rms(out, gain_ref[...])
    o_ref[...] = out


def moe_combine(h, y, dest, final_gain):
    n = h.shape[0]
    t = COMBINE_TILE
    nt = n // t
    dest3 = dest.reshape(nt, 1, t)
    final_norm = final_gain is not None
    gain = final_gain if final_norm else jnp.ones((D_MODEL,), jnp.float32)
    return pl.pallas_call(
        functools.partial(_combine_kernel, final_norm=final_norm), grid=(nt,),
        in_specs=[pl.BlockSpec((1, 1, t), lambda i: (i, 0, 0), memory_space=pltpu.SMEM),
                  pl.BlockSpec((1, 1, t), lambda i: (jnp.minimum(i + 1, nt - 1), 0, 0),
                               memory_space=pltpu.SMEM),
                  pl.BlockSpec((t, D_MODEL), lambda i: (i, 0)),
                  pl.BlockSpec((1, D_MODEL), lambda i: (0, 0)),
                  pl.BlockSpec(memory_space=pl.ANY)],
        out_specs=pl.BlockSpec((t, D_MODEL), lambda i: (i, 0)),
        out_shape=jax.ShapeDtypeStruct((n, D_MODEL), jnp.float32),
        scratch_shapes=[pltpu.VMEM((2, t, PACKED), jnp.uint32), pltpu.SemaphoreType.DMA((2,))],
        compiler_params=_cparams("arbitrary"), name="moe_combine",
    )(dest3, dest3, h, gain.reshape(1, D_MODEL), y)


def hierarchical_moe(h, gain, w_route, b_route, layer, w_gate, w_up, w_down, final_gain=None):
    n = h.shape[0]
    tm = EXPERT_TILE
    rows, route, counts = moe_router(h, gain, w_route, b_route)
    counts = counts[0, :N_CLASSES].astype(jnp.int32)
    padded = ((counts + tm - 1) // tm) * tm
    seg_end = jnp.cumsum(padded)
    seg_start = seg_end - padded
    cls = route[:, R_CLASS].astype(jnp.int32)
    rank = route[:, R_RANK].astype(jnp.int32)
    class_ids = jnp.arange(N_CLASSES, dtype=jnp.int32)
    dest = jnp.sum(jnp.where(cls[:, None] == class_ids[None, :], seg_start[None, :], 0), axis=-1) + rank
    n_rows = n + N_CLASSES * tm
    n_tiles = n_rows // tm
    used = seg_end[-1] // tm
    tile_id = jnp.arange(n_tiles, dtype=jnp.int32)
    tile_row = jnp.minimum(tile_id, used - 1).astype(jnp.int32)
    tile_class = jnp.minimum(jnp.sum((seg_end[None, :] <= (tile_row * tm)[:, None]).astype(jnp.int32), axis=1),
                             N_CLASSES - 1)
    tile_valid = (tile_id < used).astype(jnp.int32)
    slots = np.asarray([[g * EXPERTS_PER_GROUP + s for s in pair]
                        for g in range(N_EXPERT_GROUPS) for pair in PAIR_SLOTS], np.int32)
    tile_onehot = (tile_class[:, None] == class_ids[None, :]).astype(jnp.int32)
    expert_a = jnp.sum(tile_onehot * slots[None, :, 0], axis=1).astype(jnp.int32)
    expert_b = jnp.sum(tile_onehot * slots[None, :, 1], axis=1).astype(jnp.int32)

    def changed(e):
        return jnp.concatenate([jnp.ones((1,), jnp.int32), (e[1:] != e[:-1]).astype(jnp.int32)])

    tables = (expert_a, expert_b, changed(expert_a), changed(expert_b), tile_row, tile_valid)
    x_sorted = moe_dispatch(rows, dest, n_rows)
    y = expert_mlp(x_sorted, tables, layer, w_gate, w_up, w_down)
    return moe_combine(h, y, dest, final_gain)


def _permute_qk_columns(w_qkv):
    w = w_qkv.astype(jnp.bfloat16).reshape(D_MODEL, N_ATTN_GROUPS, 3, N_HEADS, 2, HALF_DIM)
    qk = jnp.transpose(w[:, :, 0:2], (0, 1, 2, 4, 3, 5)).reshape(D_MODEL, N_ATTN_GROUPS, 2, D_MODEL)
    v = w[:, :, 2:3].reshape(D_MODEL, N_ATTN_GROUPS, 1, D_MODEL)
    return jnp.concatenate([qk, v], axis=2).reshape(D_MODEL, QKV_WIDTH)


def kernel(x, positions, norm_mix, norm_ffn, norm_final, conv_w_in, conv_w, conv_w_out,
           attn_w_qkv, attn_w_o, w_coarse, b_coarse, w_fine, b_fine, w_gate, w_up, w_down):
    batch, seq, d = x.shape
    assert d == D_MODEL
    n = batch * seq
    depth = norm_mix.shape[0]
    bf = jnp.bfloat16
    h = x.reshape(n, d)
    cos, sin = rope_tables(positions.reshape(n))
    for i in range(depth):
        j = i // 2
        if i % 2 == 0:
            h = conv_mixer(h, batch, norm_mix[i], conv_w_in[j].astype(bf), conv_w[j], conv_w_out[j].astype(bf))
        else:
            w_qkv = _permute_qk_columns(attn_w_qkv[j])
            outs, maxima, dens = [], [], []
            for g, (window, dil) in enumerate(DILATED_GROUPS):
                assert window // dil == ATT_BLK
                w_group = w_qkv[:, g * 3 * D_MODEL:(g + 1) * 3 * D_MODEL]
                o, mx, den = dilated_attention_group(
                    qkv_project(h, batch, norm_mix[i], w_group, cos, sin, dil), dil)
                outs.append(o)
                maxima.append(mx)
                dens.append(den)
            h = merge_proj_residual(h, batch, outs, maxima, dens, attn_w_o[j].astype(bf))
        w_route = jnp.concatenate(
            [w_coarse[i], jnp.transpose(w_fine[i], (1, 0, 2)).reshape(d, N_EXPERTS),
             jnp.zeros((d, LANES - N_EXPERT_GROUPS - N_EXPERTS), jnp.float32)], axis=1)
        b_route = jnp.concatenate(
            [b_coarse[i], b_fine[i].reshape(-1),
             jnp.zeros((LANES - N_EXPERT_GROUPS - N_EXPERTS,), jnp.float32)]).reshape(1, LANES)
        h = hierarchical_moe(h, norm_ffn[i], w_route, b_route,
                             i, w_gate, w_up, w_down,
                             norm_final if i == depth - 1 else None)
    return h.reshape(batch, seq, d)
```
